```python
import math
import jax, jax.numpy as jnp
from jax import lax
import numpy as np

D_MODEL = 2048
BATCH = 16
SEQ = 2048
DEPTH = 1
DEC_BATCH = 32
DEC_SEQ = 8
PAST_LEN = 16384
PAGE_SIZE = 128

D_ATT = D_MODEL // 2
D_SSM = D_MODEL - D_ATT
D_MIX = D_ATT + D_SSM
HEAD_DIM = 64
N_HEADS_ATT = D_ATT // HEAD_DIM
N_KV = 4
HPG_ATT = N_HEADS_ATT // N_KV
ROT_DIM = HEAD_DIM // 4
ROPE_THETA = 500000.0
CMP_LEN = 32
CMP_STRIDE = 16
CMP_HID = 2 * HEAD_DIM
SLC_BLK = 64
N_SELECT = 16
WINDOW = 512
SEL_QBLK = 32
WIN_QBLK = 128
N_KV_PROJ = 6
SSM_HEAD_DIM = 64
N_HEADS_SSM = D_SSM // SSM_HEAD_DIM
SSM_GROUPS = 4
D_STATE = 128
CONV_W = 4
CONV_CH = D_SSM + 2 * SSM_GROUPS * D_STATE
SSD_CHUNK = 64
D_IN = D_ATT + N_KV_PROJ * N_KV * HEAD_DIM + 3 * N_HEADS_ATT + D_SSM + CONV_CH + N_HEADS_SSM
N_EXPERTS = 32
TOP_K = 4
D_FF = D_MODEL
SWIGLU_LIMIT = 7.0
SWIGLU_ALPHA = 1.702
DN_ALPHA = (2.0 * DEPTH) ** 0.25
DN_BETA = (8.0 * DEPTH) ** -0.25
LN_EPS = 1e-5
RMS_EPS = 1e-5
NEG = -1e30
FORCE = 1e6

kernel_name = "nsa_ssd_hymba_moe_step"


def _layernorm(x, g, b):
    xf = x.astype(jnp.float32)
    mu = jnp.mean(xf, -1, keepdims=True)
    var = jnp.mean(jnp.square(xf - mu), -1, keepdims=True)
    return ((xf - mu) * lax.rsqrt(var + LN_EPS) * g + b).astype(x.dtype)


def _ada(c, w_ada, b_ada):
    m = jax.nn.silu(c) @ w_ada + b_ada
    return jnp.split(m[:, None, :], 6, axis=-1)


def _rope(x, pos):
    half = ROT_DIM // 2
    inv = ROPE_THETA ** (-jnp.arange(half, dtype=jnp.float32) / half)
    ang = pos.astype(jnp.float32)[:, None] * inv
    shape = (1, ang.shape[0]) + (1,) * (x.ndim - 3) + (half,)
    cos, sin = jnp.cos(ang).reshape(shape), jnp.sin(ang).reshape(shape)
    xr = x[..., :ROT_DIM].astype(jnp.float32)
    x1, x2 = xr[..., :half], xr[..., half:]
    rot = jnp.concatenate([x1 * cos - x2 * sin, x2 * cos + x1 * sin], -1).astype(x.dtype)
    return jnp.concatenate([rot, x[..., ROT_DIM:]], -1)


def _in_proj(h, w_in, pos):
    b, T, _ = h.shape
    sizes = [D_ATT, N_KV_PROJ * N_KV * HEAD_DIM, 3 * N_HEADS_ATT, D_SSM, CONV_CH, N_HEADS_SSM]
    offs = np.cumsum(sizes)[:-1].tolist()
    q, kv, gt, z, xbc, dt = jnp.split(h @ w_in, offs, axis=-1)
    q_nope = q.reshape(b, T, N_KV, HPG_ATT, HEAD_DIM)
    q_rot = _rope(q_nope, pos)
    kv = kv.reshape(b, T, N_KV_PROJ, N_KV, HEAD_DIM)
    kv_rows = jnp.stack([kv[:, :, 0], kv[:, :, 1], _rope(kv[:, :, 2], pos), kv[:, :, 3]], axis=2)
    win_rows = jnp.stack([_rope(kv[:, :, 4], pos), kv[:, :, 5]], axis=2)
    gates = jax.nn.sigmoid(gt.reshape(b, T, N_KV, HPG_ATT, 3).astype(jnp.float32))
    return q_rot, q_nope, kv_rows, win_rows, gates, z, xbc, dt


def _compress(k, pe, w1, b1, w2):
    b, L, G, d = k.shape
    r = CMP_LEN // CMP_STRIDE
    n = L // CMP_STRIDE - r + 1
    ch = k.reshape(b, L // CMP_STRIDE, CMP_STRIDE, G, d)
    blk = jnp.concatenate([ch[:, i:i + n] for i in range(r)], axis=2)
    blk = blk + pe[:, None, :]
    flat = blk.transpose(0, 1, 3, 2, 4).reshape(b, n, G, CMP_LEN * d)
    return jax.nn.gelu(flat @ w1 + b1) @ w2


def _cmp_to_slc(n_cmp, n_slc):
    st = jnp.arange(n_cmp)[:, None] * CMP_STRIDE
    j0 = jnp.arange(n_slc)[None, :] * SLC_BLK
    return ((st < j0 + SLC_BLK) & (st + CMP_LEN > j0)).astype(jnp.float32)


def _nsa_cmp_slc(q_rot, q_nope, full, q_pos, prm):
    b, T, G, hpg, d = q_rot.shape
    L = full.shape[1]
    scale = d ** -0.5
    kc = _compress(full[:, :, 0], prm["cmp_pe"][0], prm["cmp_w1"][0], prm["cmp_b1"][0], prm["cmp_w2"][0])
    vc = _compress(full[:, :, 1], prm["cmp_pe"][1], prm["cmp_w1"][1], prm["cmp_b1"][1], prm["cmp_w2"][1])
    n_cmp = kc.shape[1]
    cmp_end = jnp.arange(n_cmp) * CMP_STRIDE + (CMP_LEN - 1)
    m_cmp = cmp_end[None, :] <= q_pos[:, None]
    s = jnp.einsum('btghd,bcgd->bghtc', q_nope, kc).astype(jnp.float32) * scale
    p_cmp = jnp.where(m_cmp, jax.nn.softmax(jnp.where(m_cmp, s, NEG), axis=-1), 0.0)
    o_cmp = jnp.einsum('bghtc,bcgd->btghd', p_cmp.astype(vc.dtype), vc)
    n_slc = L // SLC_BLK
    blk = jnp.arange(n_slc)[None, :]
    cur = (q_pos // SLC_BLK)[:, None]
    ps = jnp.einsum('bghtc,cs->bgts', p_cmp, _cmp_to_slc(n_cmp, n_slc))
    forced = (blk == 0) | (blk == cur) | (blk == cur - 1)
    ps = jnp.where(blk <= cur, jnp.where(forced, FORCE, ps), NEG)
    n_sel = min(N_SELECT, n_slc)
    _, idx = lax.top_k(ps, n_sel)
    kb = full[:, :, 2].reshape(b, n_slc, SLC_BLK, G, d).transpose(0, 3, 1, 2, 4)
    vb = full[:, :, 3].reshape(b, n_slc, SLC_BLK, G, d).transpose(0, 3, 1, 2, 4)
    bi = jnp.arange(b)[:, None, None, None]
    gi = jnp.arange(G)[None, :, None, None]
    qb = math.gcd(T, SEL_QBLK)
    nq = T // qb

    def sel_block(args):
        qc, ic, pc = args
        kg = kb[bi, gi, ic]
        vg = vb[bi, gi, ic]
        sc = jnp.einsum('btghd,bgtnsd->bghtns', qc, kg).astype(jnp.float32) * scale
        kpos = ic[..., None] * SLC_BLK + jnp.arange(SLC_BLK)
        msk = (kpos <= pc[None, None, :, None, None])[:, :, None]
        sc = jnp.where(msk, sc, NEG)
        pr = jax.nn.softmax(sc.reshape(sc.shape[:4] + (-1,)), axis=-1).reshape(sc.shape)
        return jnp.einsum('bghtns,bgtnsd->btghd', pr.astype(vg.dtype), vg)

    o_slc = lax.map(sel_block, (q_rot.reshape(b, nq, qb, G, hpg, d).swapaxes(0, 1),
                                idx.reshape(b, G, nq, qb, n_sel).transpose(2, 0, 1, 3, 4),
                                q_pos.reshape(nq, qb)))
    return o_cmp, o_slc.swapaxes(0, 1).reshape(b, T, G, hpg, d)


def _window_attn(q, q_pos, k, v, k_pos):
    b, T, G, hpg, d = q.shape
    qb = math.gcd(T, WIN_QBLK)
    nq = T // qb
    span = k.shape[1] - T + qb
    scale = d ** -0.5

    def blk(args):
        i, qc, pc = args
        kc = lax.dynamic_slice_in_dim(k, i * qb, span, axis=1)
        vc = lax.dynamic_slice_in_dim(v, i * qb, span, axis=1)
        kp = lax.dynamic_slice_in_dim(k_pos, i * qb, span, axis=0)
        sc = jnp.einsum('btghd,bsgd->bghts', qc, kc).astype(jnp.float32) * scale
        dl = pc[:, None] - kp[None, :]
        msk = (dl >= 0) & (dl < WINDOW) & (kp[None, :] >= 0)
        pr = jax.nn.softmax(jnp.where(msk, sc, NEG), axis=-1)
        return jnp.einsum('bghts,bsgd->btghd', pr.astype(vc.dtype), vc)

    o = lax.map(blk, (jnp.arange(nq), q.reshape(b, nq, qb, G, hpg, d).swapaxes(0, 1),
                      q_pos.reshape(nq, qb)))
    return o.swapaxes(0, 1).reshape(b, T, G, hpg, d)


def _ssd(x, dt, A, B, C, h0, chunk):
    b, T, H, P = x.shape
    G, N = B.shape[2], B.shape[3]
    hpg = H // G
    nc = T // chunk
    xd = (x.astype(jnp.float32) * dt[..., None]).reshape(b, nc, chunk, G, hpg, P)
    a = (dt * A).reshape(b, nc, chunk, G, hpg).transpose(0, 3, 4, 1, 2)
    Bc = B.astype(jnp.float32).reshape(b, nc, chunk, G, N)
    Cc = C.astype(jnp.float32).reshape(b, nc, chunk, G, N)
    a_cs = jnp.cumsum(a, axis=-1)
    seg = a_cs[..., :, None] - a_cs[..., None, :]
    causal = jnp.tril(jnp.ones((chunk, chunk), dtype=bool))
    Lm = jnp.where(causal, jnp.exp(jnp.where(causal, seg, 0.0)), 0.0)
    cb = jnp.einsum('bclgn,bcsgn->bgcls', Cc, Bc)
    y_diag = jnp.einsum('bgcls,bghcls,bcsghp->bclghp', cb, Lm, xd)
    decay = jnp.exp(a_cs[..., -1:] - a_cs)
    st = jnp.einsum('bcsgn,bghcs,bcsghp->cbghpn', Bc, decay, xd)
    a_tot = jnp.moveaxis(jnp.exp(a_cs[..., -1]), -1, 0)

    def step(h, inp):
        s_c, dec = inp
        return h * dec[..., None, None] + s_c, h

    h_last, h_prev = lax.scan(step, h0.astype(jnp.float32).reshape(b, G, hpg, P, N), (st, a_tot))
    y_off = jnp.einsum('bclgn,cbghpn,bghcl->bclghp', Cc, h_prev, jnp.exp(a_cs))
    return (y_diag + y_off).reshape(b, T, H, P), h_last.reshape(b, H, P, N)


def _ssm(z, xbc, dt_raw, conv_prev, h0, prm):
    b, T, _ = xbc.shape
    xp = jnp.concatenate([conv_prev.astype(xbc.dtype), xbc], axis=1)
    conv = sum(xp[:, k:k + T] * prm["conv_w"][k] for k in range(CONV_W)) + prm["conv_b"]
    xc = jax.nn.silu(conv)
    nbc = SSM_GROUPS * D_STATE
    xs = xc[..., :D_SSM].reshape(b, T, N_HEADS_SSM, SSM_HEAD_DIM)
    Bm = xc[..., D_SSM:D_SSM + nbc].reshape(b, T, SSM_GROUPS, D_STATE)
    Cm = xc[..., D_SSM + nbc:].reshape(b, T, SSM_GROUPS, D_STATE)
    dt = jax.nn.softplus(dt_raw.astype(jnp.float32) + prm["dt_bias"])
    A = -jnp.exp(prm["a_log"].astype(jnp.float32))
    y, h_last = _ssd(xs, dt, A, Bm, Cm, h0, math.gcd(T, SSD_CHUNK))
    y = y + prm["d_skip"].astype(jnp.float32)[:, None] * xs.astype(jnp.float32)
    y = y.reshape(b, T, D_SSM) * jax.nn.silu(z.astype(jnp.float32))
    yg = y.reshape(b, T, SSM_GROUPS, D_SSM // SSM_GROUPS)
    yg = yg * lax.rsqrt(jnp.mean(jnp.square(yg), -1, keepdims=True) + RMS_EPS)
    y = yg.reshape(b, T, D_SSM) * prm["ssm_norm_g"]
    return y, xp[:, -(CONV_W - 1):], h_last


def _merge(h, o_cmp, o_slc, o_win, gates, y_ssm, w_out):
    b, T, _ = h.shape
    o_att = gates[..., 0:1] * o_cmp + gates[..., 1:2] * o_slc + gates[..., 2:3] * o_win
    mix = jnp.concatenate([o_att.reshape(b, T, D_ATT).astype(h.dtype), y_ssm.astype(h.dtype)], -1)
    return mix @ w_out


def _mixer_prompt(h, prm):
    b, T, _ = h.shape
    pos = jnp.arange(T)
    q, qn, kv_rows, win_rows, gates, z, xbc, dt = _in_proj(h, prm["w_in"], pos)
    pad = (-T) % SLC_BLK
    full = jnp.concatenate([kv_rows, jnp.zeros((b, pad) + kv_rows.shape[2:], kv_rows.dtype)], axis=1)
    o_cmp, o_slc = _nsa_cmp_slc(q, qn, full, pos, prm)
    kw = jnp.pad(win_rows, ((0, 0), (WINDOW, 0), (0, 0), (0, 0), (0, 0)))
    o_win = _window_attn(q, pos, kw[:, :, 0], kw[:, :, 1], jnp.arange(-WINDOW, T))
    conv0 = jnp.zeros((b, CONV_W - 1, CONV_CH), xbc.dtype)
    h0 = jnp.zeros((b, N_HEADS_SSM, SSM_HEAD_DIM, D_STATE), jnp.float32)
    y_ssm, conv_new, ssm_new = _ssm(z, xbc, dt, conv0, h0, prm)
    out = _merge(h, o_cmp, o_slc, o_win, gates, y_ssm, prm["w_out"])
    return out, kv_rows, win_rows[:, -min(WINDOW, T):], conv_new, ssm_new


def _mixer_sample(h, cache_kv, page_table, cache_win, state_conv, state_ssm, prm):
    b, T, _ = h.shape
    past = page_table.shape[1] * PAGE_SIZE
    pos = past + jnp.arange(T)
    q, qn, kv_rows, win_rows, gates, z, xbc, dt = _in_proj(h, prm["w_in"], pos)
    past_rows = cache_kv[page_table].reshape((b, past) + cache_kv.shape[2:])
    pad = (-(past + T)) % SLC_BLK
    full = jnp.concatenate([past_rows.astype(kv_rows.dtype), kv_rows,
                            jnp.zeros((b, pad) + kv_rows.shape[2:], kv_rows.dtype)], axis=1)
    o_cmp, o_slc = _nsa_cmp_slc(q, qn, full, pos, prm)
    wb = cache_win.shape[1]
    wfull = jnp.concatenate([cache_win.astype(win_rows.dtype), win_rows], axis=1)
    o_win = _window_attn(q, pos, wfull[:, :, 0], wfull[:, :, 1], past - wb + jnp.arange(wb + T))
    y_ssm, conv_new, ssm_new = _ssm(z, xbc, dt, state_conv, state_ssm, prm)
    out = _merge(h, o_cmp, o_slc, o_win, gates, y_ssm, prm["w_out"])
    return out, kv_rows, wfull[:, -wb:], conv_new, ssm_new


def _moe(h, router_w, router_b, w_gu, b_gu, w_down, b_down):
    b, T, D = h.shape
    xt = h.reshape(b * T, D)
    logits = (xt @ router_w + router_b).astype(jnp.float32)
    top_v, top_i = lax.top_k(logits, TOP_K)
    wts = jax.nn.softmax(top_v, axis=-1)
    comb = jnp.einsum('nk,nke->ne', wts, jax.nn.one_hot(top_i, N_EXPERTS, dtype=jnp.float32))
    out = jnp.zeros((b * T, D), jnp.float32)
    for e in range(N_EXPERTS):
        gu = xt @ w_gu[e] + b_gu[e]
        g = jnp.minimum(gu[:, :D_FF], SWIGLU_LIMIT)
        u = jnp.clip(gu[:, D_FF:], -SWIGLU_LIMIT, SWIGLU_LIMIT)
        act = (u + 1.0) * g * jax.nn.sigmoid(SWIGLU_ALPHA * g)
        out = out + comb[:, e:e + 1] * (act @ w_down[e] + b_down[e])
    return out.reshape(b, T, D).astype(h.dtype)


def setup_inputs(seed: int = 0) -> dict:
    key = jax.random.key(seed)
    ks = jax.random.split(key, 40)
    f32 = jnp.float32

    def nrm(k, shape, scale=1.0):
        return jax.random.normal(k, shape, f32) * scale

    n_pages = PAST_LEN // PAGE_SIZE
    n_pool = (5 * DEC_BATCH * n_pages) // 4
    win_buf = min(WINDOW, PAST_LEN)
    L = DEPTH
    page_table = jax.random.permutation(ks[6], n_pool)[:DEC_BATCH * n_pages]
    page_table = page_table.reshape(DEC_BATCH, n_pages).astype(jnp.int32)
    dt0 = jnp.exp(jax.random.uniform(ks[16], (L, N_HEADS_SSM), f32, math.log(1e-3), math.log(1e-1)))
    return {
        "x_prompt": nrm(ks[0], (BATCH, SEQ, D_MODEL)),
        "x_sample": nrm(ks[1], (DEC_BATCH, DEC_SEQ, D_MODEL)),
        "cache_kv": nrm(ks[2], (L, n_pool, PAGE_SIZE, 4, N_KV, HEAD_DIM)),
        "cache_win": nrm(ks[3], (L, DEC_BATCH, win_buf, 2, N_KV, HEAD_DIM)),
        "state_conv": nrm(ks[4], (L, DEC_BATCH, CONV_W - 1, CONV_CH)),
        "state_ssm": nrm(ks[5], (L, DEC_BATCH, N_HEADS_SSM, SSM_HEAD_DIM, D_STATE), 0.1),
        "page_table": page_table,
        "c_prompt": nrm(ks[7], (BATCH, D_MODEL)),
        "c_sample": nrm(ks[8], (DEC_BATCH, D_MODEL)),
        "w_ada": nrm(ks[9], (L, D_MODEL, 6 * D_MODEL), 0.5 * D_MODEL ** -0.5),
        "b_ada": nrm(ks[10], (L, 6 * D_MODEL), 0.02),
        "w_in": nrm(ks[11], (L, D_MODEL, D_IN), D_MODEL ** -0.5),
        "cmp_pe": nrm(ks[12], (L, 2, CMP_LEN, HEAD_DIM), 0.1),
        "cmp_w1": nrm(ks[13], (L, 2, CMP_LEN * HEAD_DIM, CMP_HID), (CMP_LEN * HEAD_DIM) ** -0.5),
        "cmp_b1": nrm(ks[14], (L, 2, CMP_HID), 0.02),
        "cmp_w2": nrm(ks[15], (L, 2, CMP_HID, HEAD_DIM), CMP_HID ** -0.5),
        "conv_w": nrm(ks[17], (L, CONV_W, CONV_CH), CONV_W ** -0.5),
        "conv_b": nrm(ks[18], (L, CONV_CH), 0.02),
        "dt_bias": dt0 + jnp.log(-jnp.expm1(-dt0)),
        "a_log": jnp.log(jax.random.uniform(ks[19], (L, N_HEADS_SSM), f32, 1.0, 16.0)),
        "d_skip": 1.0 + nrm(ks[20], (L, N_HEADS_SSM), 0.02),
        "ssm_norm_g": 1.0 + nrm(ks[21], (L, D_SSM), 0.02),
        "w_out": nrm(ks[22], (L, D_MIX, D_MODEL), DN_BETA * D_MIX ** -0.5),
        "ln1_g": 1.0 + nrm(ks[23], (L, D_MODEL), 0.02),
        "ln1_b": nrm(ks[24], (L, D_MODEL), 0.02),
        "router_w": nrm(ks[25], (L, D_MODEL, N_EXPERTS), D_MODEL ** -0.5),
        "router_b": nrm(ks[26], (L, N_EXPERTS), 0.01),
        "w_gu": nrm(ks[27], (L, N_EXPERTS, D_MODEL, 2 * D_FF), D_MODEL ** -0.5),
        "b_gu": nrm(ks[28], (L, N_EXPERTS, 2 * D_FF), 0.01),
        "w_down": nrm(ks[29], (L, N_EXPERTS, D_FF, D_MODEL), DN_BETA * D_FF ** -0.5),
        "b_down": nrm(ks[30], (L, N_EXPERTS, D_MODEL), 0.01),
        "ln2_g": 1.0 + nrm(ks[31], (L, D_MODEL), 0.02),
        "ln2_b": nrm(ks[32], (L, D_MODEL), 0.02),
    }


def reference(x_prompt, x_sample, cache_kv, cache_win, state_conv, state_ssm, page_table,
              c_prompt, c_sample, w_ada, b_ada, w_in, cmp_pe, cmp_w1, cmp_b1, cmp_w2,
              conv_w, conv_b, dt_bias, a_log, d_skip, ssm_norm_g, w_out, ln1_g, ln1_b,
              router_w, router_b, w_gu, b_gu, w_down, b_down, ln2_g, ln2_b):
    xp, xs = x_prompt, x_sample
    kv_p, kv_s, win_p, win_s, cv_p, cv_s, ss_p, ss_s = [], [], [], [], [], [], [], []
    for l in range(DEPTH):
        prm = {"w_in": w_in[l], "cmp_pe": cmp_pe[l], "cmp_w1": cmp_w1[l], "cmp_b1": cmp_b1[l],
               "cmp_w2": cmp_w2[l], "conv_w": conv_w[l], "conv_b": conv_b[l],
               "dt_bias": dt_bias[l], "a_log": a_log[l], "d_skip": d_skip[l],
               "ssm_norm_g": ssm_norm_g[l], "w_out": w_out[l]}
        moe_w = (router_w[l], router_b[l], w_gu[l], b_gu[l], w_down[l], b_down[l])
        sh1, sc1, g1, sh2, sc2, g2 = _ada(c_prompt, w_ada[l], b_ada[l])
        mix, kvr, wr, cvr, ssr = _mixer_prompt(xp * (1.0 + sc1) + sh1, prm)
        xp = _layernorm(DN_ALPHA * xp + g1 * mix, ln1_g[l], ln1_b[l])
        xp = _layernorm(DN_ALPHA * xp + g2 * _moe(xp * (1.0 + sc2) + sh2, *moe_w), ln2_g[l], ln2_b[l])
        kv_p.append(kvr); win_p.append(wr); cv_p.append(cvr); ss_p.append(ssr)
        sh1, sc1, g1, sh2, sc2, g2 = _ada(c_sample, w_ada[l], b_ada[l])
        mix, kvr, wr, cvr, ssr = _mixer_sample(xs * (1.0 + sc1) + sh1, cache_kv[l], page_table,
                                               cache_win[l], state_conv[l], state_ssm[l], prm)
        xs = _layernorm(DN_ALPHA * xs + g1 * mix, ln1_g[l], ln1_b[l])
        xs = _layernorm(DN_ALPHA * xs + g2 * _moe(xs * (1.0 + sc2) + sh2, *moe_w), ln2_g[l], ln2_b[l])
        kv_s.append(kvr); win_s.append(wr); cv_s.append(cvr); ss_s.append(ssr)
    y_prompt, y_sample = xp, xs
    kv_prompt, kv_sample = jnp.stack(kv_p), jnp.stack(kv_s)
    win_prompt, win_sample = jnp.stack(win_p), jnp.stack(win_s)
    conv_prompt, conv_sample = jnp.stack(cv_p), jnp.stack(cv_s)
    ssm_prompt, ssm_sample = jnp.stack(ss_p), jnp.stack(ss_s)
    return (y_prompt, y_sample, kv_prompt, kv_sample, win_prompt, win_sample,
            conv_prompt, conv_sample, ssm_prompt, ssm_sample)
```

```python
import functools
import math

import jax
import jax.numpy as jnp
import numpy as np
from jax import lax
from jax.experimental import pallas as pl
from jax.experimental.pallas import tpu as pltpu

D_MODEL = 2048
D_ATT = 1024
D_SSM = 1024
HEAD_DIM = 64
N_HEADS_ATT = 16
N_KV = 4
HPG = 4
ROT_DIM = 16
ROPE_THETA = 500000.0
CMP_LEN = 32
CMP_STRIDE = 16
CMP_HID = 128
SLC_BLK = 64
N_SELECT = 16
WINDOW = 512
N_HEADS_SSM = 16
SSM_GROUPS = 4
D_STATE = 128
CONV_W = 4
CONV_CH = 2048
N_EXPERTS = 32
TOP_K = 4
D_FF = 2048
SWIGLU_LIMIT = 7.0
SWIGLU_ALPHA = 1.702
DN_ALPHA = 2.0 ** 0.25
LN_EPS = 1e-5
RMS_EPS = 1e-5
NEG = -1e30
FORCE = 1e6
PAGE = 128

D_KV = 6 * N_KV * HEAD_DIM
D_GT = 3 * N_HEADS_ATT
D_IN = D_ATT + D_KV + D_GT + D_SSM + CONV_CH + N_HEADS_SSM
D_INP = D_ATT + D_KV + D_SSM + CONV_CH + 128

LANES = 128
VMEM_LIMIT = 56 * 1024 * 1024

F32 = jnp.float32
BF16 = jnp.bfloat16
HI = lax.Precision.HIGHEST


def _cparams(sem, vmem=VMEM_LIMIT):
    return pltpu.CompilerParams(dimension_semantics=sem, vmem_limit_bytes=vmem)


def _dot(a, b, precision=None):
    return jnp.dot(a, b, preferred_element_type=F32, precision=precision)


def _dot_nt(a, b, precision=None):
    return lax.dot_general(a, b, (((1,), (1,)), ((), ())), preferred_element_type=F32,
                           precision=precision)


def _dot_tn(a, b, precision=None):
    return lax.dot_general(a, b, (((0,), (0,)), ((), ())), preferred_element_type=F32,
                           precision=precision)


def _sigmoid(x):
    return 1.0 / (1.0 + jnp.exp(-x))


def _silu(x):
    return x * _sigmoid(x)


def _ada_kernel(c_ref, w_ref, b_ref, o_ref):
    c = c_ref[...]
    a = _silu(c).astype(BF16)
    o_ref[...] = _dot(a, w_ref[...].astype(BF16)) + b_ref[...]


def _ada(c_all, w_ada, b_ada):
    nb = c_all.shape[0]
    tn = 1024
    n = w_ada.shape[1]
    return pl.pallas_call(
        _ada_kernel,
        out_shape=jax.ShapeDtypeStruct((nb, n), F32),
        grid=(n // tn,),
        in_specs=[pl.BlockSpec((nb, D_MODEL), lambda j: (0, 0)),
                  pl.BlockSpec((D_MODEL, tn), lambda j: (0, j)),
                  pl.BlockSpec((1, tn), lambda j: (0, j))],
        out_specs=pl.BlockSpec((nb, tn), lambda j: (0, j)),
        compiler_params=_cparams(("parallel",)),
        name="ada",
    )(c_all, w_ada, b_ada.reshape(1, n))


def _q_perm():
    idx = np.zeros(D_ATT, np.int32)
    for gp in range(2):
        for j in range(HPG):
            for side in range(2):
                g = 2 * gp + side
                for d in range(HEAD_DIM):
                    idx[gp * 512 + j * 128 + side * 64 + d] = g * 256 + j * 64 + d
    return idx


def _prep_w_in(w_in):
    o_kv = D_ATT
    o_gt = o_kv + D_KV
    o_z = o_gt + D_GT
    o_xbc = o_z + D_SSM
    o_dt = o_xbc + CONV_CH
    wq = w_in[:, :D_ATT][:, _q_perm()]
    pad = jnp.zeros((D_MODEL, 128 - D_GT - N_HEADS_SSM), w_in.dtype)
    w = jnp.concatenate([wq, w_in[:, o_kv:o_gt], w_in[:, o_z:o_xbc], w_in[:, o_xbc:o_dt],
                         w_in[:, o_gt:o_z], w_in[:, o_dt:], pad], axis=1)
    return w.astype(BF16)


def _rope_tables(pos):
    half = ROT_DIM // 2
    inv = ROPE_THETA ** (-jnp.arange(half, dtype=F32) / half)
    ang = pos.astype(F32)[:, None] * inv
    cos, sin = jnp.cos(ang), jnp.sin(ang)
    t = pos.shape[0]
    one = jnp.ones((t, HEAD_DIM - ROT_DIM), F32)
    zero = jnp.zeros((t, HEAD_DIM - ROT_DIM), F32)
    z8 = jnp.zeros((t, half), F32)
    c64 = jnp.concatenate([cos, cos, one], 1)
    s1 = jnp.concatenate([z8, sin, zero], 1)
    s2 = jnp.concatenate([-sin, z8, zero], 1)
    rep = lambda a: jnp.tile(a, (1, 4))
    return jnp.concatenate([rep(c64), rep(s1), rep(s2)], axis=1)


def _rope256(x, tab):
    n = x.shape[1]
    return (x * tab[:, 0:256] + pltpu.roll(x, 8, 1) * tab[:, 256:512]
            + pltpu.roll(x, n - 8, 1) * tab[:, 512:768])


def _inproj_kernel(x_ref, sc_ref, sh_ref, w_ref, tab_ref,
                   qn_ref, qr_ref, kv_ref, win_ref, kvb_ref, z_ref, xbc_ref, gd_ref):
    h = (x_ref[0] * (1.0 + sc_ref[0]) + sh_ref[0]).astype(BF16)
    tab = tab_ref[...]
    for c in range(4):
        q = _dot(h, w_ref[:, c * 256:(c + 1) * 256])
        qn_ref[0, :, c * 256:(c + 1) * 256] = q.astype(BF16)
        qr_ref[0, :, c * 256:(c + 1) * 256] = _rope256(q, tab).astype(BF16)
    o = D_ATT
    for p in range(6):
        y = _dot(h, w_ref[:, o + p * 256:o + (p + 1) * 256])
        if p in (2, 4):
            y = _rope256(y, tab)
        if p < 4:
            kv_ref[0, :, p * 256:(p + 1) * 256] = y
        else:
            win_ref[0, :, (p - 4) * 256:(p - 3) * 256] = y
        if p >= 2:
            kvb_ref[0, :, (p - 2) * 256:(p - 1) * 256] = y.astype(BF16)
    o += D_KV
    for c in range(2):
        z_ref[0, :, c * 512:(c + 1) * 512] = _dot(h, w_ref[:, o + c * 512:o + (c + 1) * 512]).astype(BF16)
    o += D_SSM
    for c in range(4):
        xbc_ref[0, :, c * 512:(c + 1) * 512] = _dot(h, w_ref[:, o + c * 512:o + (c + 1) * 512])
    o += CONV_CH
    gd_ref[0] = _dot(h, w_ref[:, o:o + 128])


def _in_proj(x, sc, sh, w_bf, tab, tm):
    b, t, _ = x.shape
    nt = t // tm
    row = lambda w: pl.BlockSpec((1, tm, w), lambda i, j: (i, j, 0))
    outs = [(D_ATT, BF16), (D_ATT, BF16), (1024, F32), (512, F32), (1024, BF16),
            (D_SSM, BF16), (CONV_CH, F32), (128, F32)]
    return pl.pallas_call(
        _inproj_kernel,
        out_shape=[jax.ShapeDtypeStruct((b, t, w), dt) for w, dt in outs],
        grid=(b, nt),
        in_specs=[row(D_MODEL),
                  pl.BlockSpec((1, 1, D_MODEL), lambda i, j: (i, 0, 0)),
                  pl.BlockSpec((1, 1, D_MODEL), lambda i, j: (i, 0, 0)),
                  pl.BlockSpec((D_MODEL, D_INP), lambda i, j: (0, 0), pipeline_mode=pl.Buffered(1)),
                  pl.BlockSpec((tm, 768), lambda i, j: (j, 0))],
        out_specs=[row(w) for w, _ in outs],
        compiler_params=_cparams(("parallel", "parallel")),
        name="in_proj",
    )(x, sc, sh, w_bf, tab)


def _prep_cmp_weights(cmp_pe, cmp_w1, cmp_b1, cmp_w2):
    w1 = cmp_w1.reshape(2, 2, CMP_STRIDE, HEAD_DIM, CMP_HID)
    eye = jnp.eye(2, dtype=cmp_w1.dtype)
    w1p = jnp.einsum('phsdj,ab->psadhbj', w1, eye).reshape(2, CMP_STRIDE * 128, 4 * CMP_HID)
    w2p = jnp.einsum('pjd,ab->pajbd', cmp_w2, eye).reshape(2, 2 * CMP_HID, 2 * HEAD_DIM)
    pe = cmp_pe.reshape(2, 1, CMP_LEN * HEAD_DIM)
    pe = jnp.concatenate([pe, jnp.zeros((2, 7, CMP_LEN * HEAD_DIM), pe.dtype)], axis=1)
    b1 = jnp.concatenate([cmp_b1, cmp_b1], axis=-1).reshape(2, 1, 2 * CMP_HID)
    return w1p.astype(BF16), w2p.astype(BF16), pe, cmp_w1, b1


def _gelu_tanh(x):
    return 0.5 * x * (1.0 + jnp.tanh(0.7978845608028654 * (x + 0.044715 * x * x * x)))


def _compress_core(load, nrows, w1p_ref, w2p_ref, pe_ref, w1_ref, b1_ref, store):
    for p in range(2):
        pe_bias = _dot(pe_ref[p], w1_ref[p], precision=HI)[0:1]
        pe_bias = jnp.concatenate([pe_bias, pe_bias], axis=1)
        for pr in range(2):
            xs = jnp.concatenate([load(s, 2 * p + pr).astype(BF16) for s in range(CMP_STRIDE)], axis=1)
            ab = _dot(xs, w1p_ref[p])
            a, bb = ab[:, 0:256], ab[:, 256:512]
            pre = a + pltpu.roll(bb, nrows - 1, 0) + pe_bias + b1_ref[p]
            hid = _gelu_tanh(pre).astype(BF16)
            store(p, pr, _dot(hid, w2p_ref[p]))


def _compress_prompt_kernel(b0, b1, b2, b3, w1p_ref, w2p_ref, pe_ref, w1_ref, b1_ref, kc_ref, vc_ref):
    blocks = (b0, b1, b2, b3)
    n = b0.shape[1] // CMP_STRIDE
    outs = (kc_ref, vc_ref)

    def load(s, lb):
        return blocks[lb][0, pl.ds(s, n, stride=CMP_STRIDE), :]

    def store(p, pr, val):
        outs[p][0, :, pr * 128:(pr + 1) * 128] = val.astype(BF16)

    _compress_core(load, n, w1p_ref, w2p_ref, pe_ref, w1_ref, b1_ref, store)


def _wspecs():
    z3 = lambda *a: (0, 0, 0)
    return [pl.BlockSpec((2, CMP_STRIDE * 128, 4 * CMP_HID), z3),
            pl.BlockSpec((2, 2 * CMP_HID, 2 * HEAD_DIM), z3),
            pl.BlockSpec((2, 8, CMP_LEN * HEAD_DIM), z3),
            pl.BlockSpec((2, CMP_LEN * HEAD_DIM, CMP_HID), z3),
            pl.BlockSpec((2, 1, 2 * CMP_HID), z3)]


def _compress_prompt(kv_f32, cw):
    b, t, _ = kv_f32.shape
    n = t // CMP_STRIDE
    lane_blk = lambda lb: pl.BlockSpec((1, t, 128), lambda i: (i, 0, lb))
    return pl.pallas_call(
        _compress_prompt_kernel,
        out_shape=[jax.ShapeDtypeStruct((b, n, 256), BF16)] * 2,
        grid=(b,),
        in_specs=[lane_blk(lb) for lb in range(4)] + _wspecs(),
        out_specs=[pl.BlockSpec((1, n, 256), lambda i: (i, 0, 0))] * 2,
        compiler_params=_cparams(("parallel",)),
        name="compress_prompt",
    )(kv_f32, kv_f32, kv_f32, kv_f32, *cw)


CMP_PAGES = 32


def _compress_sample_kernel(pt_ref, *refs):
    npg = CMP_PAGES + 1
    pages = refs[:4 * npg]
    w1p_ref, w2p_ref, pe_ref, w1_ref, b1_ref, kc_ref, vc_ref = refs[4 * npg:]
    cpp = PAGE // CMP_STRIDE
    nrows = CMP_PAGES * cpp
    outs = (kc_ref, vc_ref)

    def load(s, lb):
        return jnp.concatenate(
            [pages[4 * k + lb][0, pl.ds(s, cpp, stride=CMP_STRIDE), :] for k in range(npg)], axis=0)

    def store(p, pr, val):
        outs[p][0, :, pr * 128:(pr + 1) * 128] = val[:nrows].astype(BF16)

    _compress_core(load, nrows + cpp, w1p_ref, w2p_ref, pe_ref, w1_ref, b1_ref, store)


def _compress_sample(cache3, page_table, cw):
    b, n_pages = page_table.shape
    steps = n_pages // CMP_PAGES
    nrows = CMP_PAGES * (PAGE // CMP_STRIDE)

    def pspec(k, lb):
        return pl.BlockSpec(
            (1, PAGE, 128),
            lambda i, j, pt: (pt[i, jnp.minimum(j * CMP_PAGES + k, n_pages - 1)], 0, lb))

    gs = pltpu.PrefetchScalarGridSpec(
        num_scalar_prefetch=1,
        grid=(b, steps),
        in_specs=[pspec(k, lb) for k in range(CMP_PAGES + 1) for lb in range(4)]
                 + [pl.BlockSpec(s.block_shape, lambda i, j, pt: (0, 0, 0)) for s in _wspecs()],
        out_specs=[pl.BlockSpec((1, nrows, 256), lambda i, j, pt: (i, j, 0))] * 2,
    )
    return pl.pallas_call(
        _compress_sample_kernel,
        out_shape=[jax.ShapeDtypeStruct((b, n_pages * 8, 256), BF16)] * 2,
        grid_spec=gs,
        compiler_params=_cparams(("parallel", "parallel")),
        name="compress_sample",
    )(page_table, *([cache3] * (4 * (CMP_PAGES + 1))), *cw)


DT_LANE = D_GT


def _prep_ssm_params(conv_w, conv_b, dt_bias, a_log, d_skip, ssm_norm_g):
    cw = jnp.concatenate([conv_w, jnp.zeros((8 - CONV_W, CONV_CH), conv_w.dtype)], axis=0)
    lane = lambda v: jnp.zeros((1, 128), F32).at[0, DT_LANE:DT_LANE + N_HEADS_SSM].set(v)
    dsk = jnp.repeat(d_skip, HEAD_DIM).reshape(1, D_SSM)
    return (cw, conv_b.reshape(1, CONV_CH), lane(dt_bias), lane(a_log), dsk,
            ssm_norm_g.reshape(1, D_SSM))


def _ssd_kernel(xbc_ref, z_ref, gd_ref, cprev_ref, h0_ref, cw_ref, cb_ref, dtb_ref, alog_ref,
                dsk_ref, ng_ref, y_ref, st_ref, xpad_ref):
    q = xbc_ref.shape[1]
    c = pl.program_id(1)

    @pl.when(c == 0)
    def _():
        st_ref[0] = h0_ref[0]
        xpad_ref[0:8, :] = cprev_ref[0]

    xpad_ref[8:8 + q, :] = xbc_ref[0]
    conv = cb_ref[...] + xpad_ref[pl.ds(5, q), :] * cw_ref[0:1, :]
    for k in range(1, CONV_W):
        conv = conv + xpad_ref[pl.ds(5 + k, q), :] * cw_ref[k:k + 1, :]
    xpad_ref[0:8, :] = xpad_ref[q:q + 8, :]
    xc = _silu(conv)
    xs = xc[:, 0:D_SSM]

    lane = lax.broadcasted_iota(jnp.int32, (1, 128), 1)
    in_dt = (lane >= DT_LANE) & (lane < DT_LANE + N_HEADS_SSM)
    v = gd_ref[0] + dtb_ref[...]
    dt = jnp.maximum(v, 0.0) + jnp.log1p(jnp.exp(-jnp.abs(v)))
    a = jnp.where(in_dt, -jnp.exp(alog_ref[...]), 0.0) * dt
    er = lax.broadcasted_iota(jnp.int32, (128, D_SSM), 0)
    ec = lax.broadcasted_iota(jnp.int32, (128, D_SSM), 1)
    expand = (er - DT_LANE == ec // HEAD_DIM).astype(F32)
    ri = lax.broadcasted_iota(jnp.int32, (q, q), 0)
    ci = lax.broadcasted_iota(jnp.int32, (q, q), 1)
    causal = ri >= ci
    acs = _dot(causal.astype(F32), a, precision=HI)
    acs_t = acs.T
    dt_x = _dot(jnp.where(in_dt, dt, 0.0), expand, precision=HI)
    acs_x = _dot(acs, expand, precision=HI)
    last_x = acs_x[q - 1:q, :]
    grow_x = jnp.exp(acs_x)
    decay_x = jnp.exp(last_x - acs_x)
    er2 = lax.broadcasted_iota(jnp.int32, (D_SSM, 128), 0)
    ec2 = lax.broadcasted_iota(jnp.int32, (D_SSM, 128), 1)
    expand_t = (ec2 - DT_LANE == er2 // HEAD_DIM).astype(F32)
    tot_col = jnp.exp(_dot(expand_t, acs_t, precision=HI)[:, q - 1:q])

    xd = xs * dt_x
    xdd = (xd * decay_x).astype(BF16)
    xd_b = xd.astype(BF16)
    half = lax.broadcasted_iota(jnp.int32, (1, 128), 1) < HEAD_DIM
    ys = []
    for g in range(SSM_GROUPS):
        bg = xc[:, D_SSM + g * D_STATE:D_SSM + (g + 1) * D_STATE].astype(BF16)
        cg = xc[:, D_SSM + (SSM_GROUPS + g) * D_STATE:D_SSM + (SSM_GROUPS + g + 1) * D_STATE].astype(BF16)
        cbm = _dot_nt(cg, bg)
        for m in (2 * g, 2 * g + 1):
            sl = slice(128 * m, 128 * (m + 1))
            yh = []
            for hh in (2 * m, 2 * m + 1):
                col = acs[:, DT_LANE + hh:DT_LANE + hh + 1]
                row = acs_t[DT_LANE + hh:DT_LANE + hh + 1, :]
                lm = jnp.where(causal, jnp.exp(jnp.where(causal, col - row, 0.0)), 0.0)
                yh.append(_dot((cbm * lm).astype(BF16), xd_b[:, sl]))
            y_diag = jnp.where(half, yh[0], yh[1])
            st = st_ref[0, sl, :]
            y_off = _dot_nt(cg, st.astype(BF16)) * grow_x[:, sl]
            st_ref[0, sl, :] = st * tot_col[sl, :] + _dot_tn(xdd[:, sl], bg)
            ys.append(y_diag + y_off)
    y = jnp.concatenate(ys, axis=1) + dsk_ref[...] * xs
    y = y * _silu(z_ref[0].astype(F32))
    gw = D_SSM // SSM_GROUPS
    outs = []
    for g in range(SSM_GROUPS):
        blk = y[:, g * gw:(g + 1) * gw]
        ms = jnp.mean(blk * blk, axis=1, keepdims=True)
        outs.append(blk * lax.rsqrt(ms + RMS_EPS))
    y_ref[0] = (jnp.concatenate(outs, axis=1) * ng_ref[...]).astype(BF16)


def _ssd(xbc, z, gd, conv_prev8, h0, sp, q):
    b, t, _ = xbc.shape
    nc = t // q
    row = lambda w: pl.BlockSpec((1, q, w), lambda i, j: (i, j, 0))
    per_b = lambda r, w: pl.BlockSpec((1, r, w), lambda i, j: (i, 0, 0))
    par = lambda r, w: pl.BlockSpec((r, w), lambda i, j: (0, 0))
    return pl.pallas_call(
        _ssd_kernel,
        out_shape=[jax.ShapeDtypeStruct((b, t, D_SSM), BF16),
                   jax.ShapeDtypeStruct((b, D_SSM, D_STATE), F32)],
        grid=(b, nc),
        in_specs=[row(CONV_CH), row(D_SSM), row(128), per_b(8, CONV_CH), per_b(D_SSM, D_STATE),
                  par(8, CONV_CH), par(1, CONV_CH), par(1, 128), par(1, 128), par(1, D_SSM),
                  par(1, D_SSM)],
        out_specs=[row(D_SSM), per_b(D_SSM, D_STATE)],
        scratch_shapes=[pltpu.VMEM((q + 8, CONV_CH), F32)],
        compiler_params=_cparams(("parallel", "arbitrary")),
        name="ssd",
    )(xbc, z, gd, conv_prev8, h0, *sp)


ATT_TQ = 256
SCALE = HEAD_DIM ** -0.5


def _select_blocks(ps, n_blk):
    lane = lax.broadcasted_iota(jnp.int32, ps.shape, 1)
    rank = jnp.zeros(ps.shape, F32)
    for i in range(n_blk):
        vi = ps[:, i:i + 1]
        ahead = (vi > ps) | ((vi == ps) & (lane > i))
        rank = rank + ahead.astype(F32)
    return jnp.where((rank < N_SELECT) & (lane < n_blk), 1.0, 0.0)


def _softmax_update(s, mask, v, m_ref, l_ref, acc_ref, h):
    s = jnp.where(mask, s, NEG)
    m_old = m_ref[h]
    m_new = jnp.maximum(m_old, jnp.max(s, axis=1, keepdims=True))
    p = jnp.where(mask, jnp.exp(s - m_new), 0.0)
    alpha = jnp.exp(m_old - m_new)
    l_ref[h] = alpha * l_ref[h] + jnp.sum(p, axis=1, keepdims=True)
    acc_ref[h] = alpha * acc_ref[h] + _dot(p.astype(BF16), v)
    m_ref[h] = m_new


def _attn_prompt_kernel(qn_ref, qr_ref, kc_ref, vc_ref, ks_ref, vs_ref, kw_ref, vw_ref, gd_ref,
                        o_ref, m_ref, l_ref, acc_ref):
    tq = qn_ref.shape[1]
    t_all = ks_ref.shape[1]
    n_cmp = t_all // CMP_STRIDE - 1
    n_blk = t_all // SLC_BLK
    gp = pl.program_id(1)
    qi = pl.program_id(2)
    pos = qi * tq + lax.broadcasted_iota(jnp.int32, (tq, 1), 0)
    lane = lax.broadcasted_iota(jnp.int32, (1, 128), 1)
    sides = (lane < HEAD_DIM, lane >= HEAD_DIM)
    gates = _sigmoid(gd_ref[0])

    def qz(ref, j, side):
        col = ref[0, :, j * 128:(j + 1) * 128]
        return jnp.where(sides[side], col, jnp.zeros_like(col))

    kc = kc_ref[0]
    vc = vc_ref[0]
    vis = (CMP_STRIDE * lane + (CMP_LEN - 1) <= pos) & (lane < n_cmp)
    cr = lax.broadcasted_iota(jnp.int32, (128, 128), 0) * CMP_STRIDE
    j0 = lax.broadcasted_iota(jnp.int32, (128, 128), 1) * SLC_BLK
    overlap = ((cr < j0 + SLC_BLK) & (cr + CMP_LEN > j0)).astype(F32)
    cur = pos // SLC_BLK
    forced = (lane == 0) | (lane == cur) | (lane == cur - 1)
    o_cmp = [[None] * HPG for _ in range(2)]
    sel = []
    for side in range(2):
        psum = jnp.zeros((tq, 128), F32)
        for j in range(HPG):
            s = jnp.where(vis, _dot_nt(qz(qn_ref, j, side), kc) * SCALE, NEG)
            e = jnp.exp(s - jnp.max(s, axis=1, keepdims=True))
            p = jnp.where(vis, e / jnp.sum(e, axis=1, keepdims=True), 0.0)
            psum = psum + p
            o_cmp[side][j] = _dot(p.astype(BF16), vc)
        ps = _dot(psum, overlap, precision=HI)
        ps = jnp.where(lane <= cur, jnp.where(forced, FORCE, ps), NEG)
        sel.append(_select_blocks(ps, n_blk).astype(BF16))

    def reset():
        m_ref[...] = jnp.full(m_ref.shape, NEG, F32)
        l_ref[...] = jnp.zeros(l_ref.shape, F32)
        acc_ref[...] = jnp.zeros(acc_ref.shape, F32)

    def finish(side, j):
        h = side * HPG + j
        return acc_ref[h] / l_ref[h]

    reset()
    brow = lax.broadcasted_iota(jnp.int32, (128, tq), 0)
    bcol = lax.broadcasted_iota(jnp.int32, (128, tq), 1)

    def slc_step(t, carry):
        k0 = pl.multiple_of(t * tq, tq)
        k = ks_ref[0, pl.ds(k0, tq), :]
        v = vs_ref[0, pl.ds(k0, tq), :]
        kpos = k0 + lax.broadcasted_iota(jnp.int32, (1, tq), 1)
        expand = (brow == (k0 + bcol) // SLC_BLK).astype(BF16)
        for side in range(2):
            mask = (_dot(sel[side], expand) > 0.5) & (kpos <= pos)
            for j in range(HPG):
                s = _dot_nt(qz(qr_ref, j, side), k) * SCALE
                _softmax_update(s, mask, v, m_ref, l_ref, acc_ref, side * HPG + j)
        return carry

    lax.fori_loop(0, qi + 1, slc_step, 0)
    o_slc = [[finish(side, j) for j in range(HPG)] for side in range(2)]

    reset()

    def win_step(t, carry):
        k0 = pl.multiple_of(t * tq, tq)
        k = kw_ref[0, pl.ds(k0, tq), :]
        v = vw_ref[0, pl.ds(k0, tq), :]
        dl = pos - (k0 + lax.broadcasted_iota(jnp.int32, (1, tq), 1))
        mask = (dl >= 0) & (dl < WINDOW)
        for side in range(2):
            for j in range(HPG):
                s = _dot_nt(qz(qr_ref, j, side), k) * SCALE
                _softmax_update(s, mask, v, m_ref, l_ref, acc_ref, side * HPG + j)
        return carry

    lax.fori_loop(jnp.maximum(qi - WINDOW // tq, 0), qi + 1, win_step, 0)

    for j in range(HPG):
        cols = []
        for side in range(2):
            h = side * HPG + j
            gi = (2 * gp + side) * (3 * HPG) + 3 * j
            gsel = [jnp.sum(jnp.where(lane == gi + k, gates, 0.0), axis=1, keepdims=True)
                    for k in range(3)]
            cols.append(gsel[0] * o_cmp[side][j] + gsel[1] * o_slc[side][j]
                        + gsel[2] * (acc_ref[h] / l_ref[h]))
        o_ref[0, :, j * 128:(j + 1) * 128] = jnp.where(sides[0], cols[0], cols[1]).astype(BF16)


def _attn_prompt(qn, qr, kc, vc, kvb, gd):
    b, t, _ = qn.shape
    tq = ATT_TQ
    qspec = pl.BlockSpec((1, tq, 512), lambda i, g, q: (i, q, g))
    cspec = pl.BlockSpec((1, kc.shape[1], 128), lambda i, g, q: (i, 0, g))
    kvspec = lambda base: pl.BlockSpec((1, t, 128), lambda i, g, q: (i, 0, base + g))
    return pl.pallas_call(
        _attn_prompt_kernel,
        out_shape=jax.ShapeDtypeStruct((b, t, D_ATT), BF16),
        grid=(b, 2, t // tq),
        in_specs=[qspec, qspec, cspec, cspec, kvspec(0), kvspec(2), kvspec(4), kvspec(6),
                  pl.BlockSpec((1, tq, 128), lambda i, g, q: (i, q, 0))],
        out_specs=qspec,
        scratch_shapes=[pltpu.VMEM((2 * HPG, tq, 1), F32), pltpu.VMEM((2 * HPG, tq, 1), F32),
                        pltpu.VMEM((2 * HPG, tq, 128), F32)],
        compiler_params=_cparams(("parallel", "parallel", "arbitrary")),
        name="attn_prompt",
    )(qn, qr, kc, vc, kvb, kvb, kvb, kvb, gd)


ATS_PAGES = 16
N_ROWS_S = 128


def _attn_sample_kernel(pt_ref, *refs, n_steps, past, t_dec):
    npg = ATS_PAGES
    kpages = refs[0:2 * npg:2]
    vpages = refs[1:2 * npg:2]
    (qn_ref, qr_ref, kc_ref, vc_ref, kvn_ref, cwin_ref, wnew_ref, gt_ref,
     o_ref, sel_ref, ocmp_ref, m_ref, l_ref, acc_ref) = refs[2 * npg:]
    s_id = pl.program_id(1)
    nr = N_ROWS_S
    rows = lax.broadcasted_iota(jnp.int32, (nr, 1), 0)
    t_row = (rows % (t_dec * HPG)) // HPG
    pos = past + t_row
    qr = qr_ref[0]

    @pl.when(s_id == 0)
    def _():
        n_c = kc_ref.shape[1]
        n_blk = (past + t_dec + SLC_BLK - 1) // SLC_BLK
        bps = ATS_PAGES * PAGE // SLC_BLK
        nbl = -(-(n_steps * bps + 128) // 128) * 128
        cl = lax.broadcasted_iota(jnp.int32, (1, n_c), 1)
        vis = (CMP_STRIDE * cl + (CMP_LEN - 1) <= pos) & (cl < n_c - 1)
        s = jnp.where(vis, _dot_nt(qn_ref[0], kc_ref[0]) * SCALE, NEG)
        e = jnp.exp(s - jnp.max(s, axis=1, keepdims=True))
        p = jnp.where(vis, e / jnp.sum(e, axis=1, keepdims=True), 0.0)
        ocmp_ref[...] = _dot(p.astype(BF16), vc_ref[0])
        ng = nr // HPG
        gsum = (lax.broadcasted_iota(jnp.int32, (ng, nr), 1) // HPG
                == lax.broadcasted_iota(jnp.int32, (ng, nr), 0)).astype(F32)
        cr = lax.broadcasted_iota(jnp.int32, (n_c, nbl), 0) * CMP_STRIDE
        j0 = lax.broadcasted_iota(jnp.int32, (n_c, nbl), 1) * SLC_BLK
        overlap = ((cr < j0 + SLC_BLK) & (cr + CMP_LEN > j0)).astype(F32)
        ps = _dot(_dot(gsum, p, precision=HI), overlap, precision=HI)
        bl = lax.broadcasted_iota(jnp.int32, (1, nbl), 1)
        g_rows = lax.broadcasted_iota(jnp.int32, (ng, 1), 0)
        cur = (past + g_rows % t_dec) // SLC_BLK
        forced = (bl == 0) | (bl == cur) | (bl == cur - 1)
        ps = jnp.where(bl <= cur, jnp.where(forced, FORCE, ps), NEG)
        rank = jnp.zeros(ps.shape, F32)
        for i in range(n_blk):
            vi = ps[:, i:i + 1]
            rank = rank + ((vi > ps) | ((vi == ps) & (bl > i))).astype(F32)
        sel = jnp.where((rank < N_SELECT) & (bl < n_blk), 1.0, 0.0)
        gexp = (lax.broadcasted_iota(jnp.int32, (nr, ng), 0) // HPG
                == lax.broadcasted_iota(jnp.int32, (nr, ng), 1)).astype(F32)
        sel_rows = _dot(gexp, sel)
        for w in range(sel_ref.shape[0]):
            sel_ref[w] = sel_rows[:, bps * w:bps * w + 128].astype(BF16)
        m_ref[...] = jnp.full(m_ref.shape, NEG, F32)
        l_ref[...] = jnp.zeros(l_ref.shape, F32)
        acc_ref[...] = jnp.zeros(acc_ref.shape, F32)

    def update(s, mask, v):
        s = jnp.where(mask, s, NEG)
        m_old = m_ref[...]
        m_new = jnp.maximum(m_old, jnp.max(s, axis=1, keepdims=True))
        p = jnp.where(mask, jnp.exp(s - m_new), 0.0)
        alpha = jnp.exp(m_old - m_new)
        l_ref[...] = alpha * l_ref[...] + jnp.sum(p, axis=1, keepdims=True)
        acc_ref[...] = alpha * acc_ref[...] + _dot(p.astype(BF16), v)
        m_ref[...] = m_new

    nk = npg * PAGE
    k = jnp.concatenate([r[0] for r in kpages], axis=0).astype(BF16)
    v = jnp.concatenate([r[0] for r in vpages], axis=0).astype(BF16)
    expand = (lax.broadcasted_iota(jnp.int32, (128, nk), 0)
              == lax.broadcasted_iota(jnp.int32, (128, nk), 1) // SLC_BLK).astype(BF16)
    kpos = s_id * nk + lax.broadcasted_iota(jnp.int32, (1, nk), 1)
    mask = (_dot(sel_ref[s_id], expand) > 0.5) & (kpos <= pos)
    update(_dot_nt(qr, k) * SCALE, mask, v)

    @pl.when(s_id == n_steps - 1)
    def _():
        kn = kvn_ref[0, :, 512:768].astype(BF16)
        vn = kvn_ref[0, :, 768:1024].astype(BF16)
        tk = lax.broadcasted_iota(jnp.int32, (1, t_dec), 1)
        cur_sel = sel_ref[n_steps][:, 0:1].astype(F32) > 0.5
        update(_dot_nt(qr, kn) * SCALE, cur_sel & (past + tk <= pos), vn)
        o_slc = acc_ref[...] / l_ref[...]
        wb = cwin_ref.shape[1]
        kw = cwin_ref[0, :, 0:256].astype(BF16)
        vw = cwin_ref[0, :, 256:512].astype(BF16)
        kwn = wnew_ref[0, :, 0:256].astype(BF16)
        vwn = wnew_ref[0, :, 256:512].astype(BF16)
        d1 = pos - (past - wb + lax.broadcasted_iota(jnp.int32, (1, wb), 1))
        d2 = pos - (past + tk)
        m1 = (d1 >= 0) & (d1 < WINDOW)
        m2 = (d2 >= 0) & (d2 < WINDOW)
        s1 = jnp.where(m1, _dot_nt(qr, kw) * SCALE, NEG)
        s2 = jnp.where(m2, _dot_nt(qr, kwn) * SCALE, NEG)
        mx = jnp.maximum(jnp.max(s1, axis=1, keepdims=True), jnp.max(s2, axis=1, keepdims=True))
        p1 = jnp.where(m1, jnp.exp(s1 - mx), 0.0)
        p2 = jnp.where(m2, jnp.exp(s2 - mx), 0.0)
        den = jnp.sum(p1, axis=1, keepdims=True) + jnp.sum(p2, axis=1, keepdims=True)
        o_win = (_dot(p1.astype(BF16), vw) + _dot(p2.astype(BF16), vwn)) / den
        g = _sigmoid(gt_ref[0])
        o_ref[0] = g[:, 0:1] * ocmp_ref[...] + g[:, 1:2] * o_slc + g[:, 2:3] * o_win


def _attn_sample(cache3, page_table, qn_x, qr_x, kc, vc, kv_new, cache_win2, win_new, gt_rows, t_dec):
    b, n_pages = page_table.shape
    past = n_pages * PAGE
    n_steps = n_pages // ATS_PAGES
    nr = N_ROWS_S

    def pspec(k, blk):
        return pl.BlockSpec((1, PAGE, 256), lambda i, s, pt: (pt[i, s * ATS_PAGES + k], 0, blk))

    per_b = lambda r, w: pl.BlockSpec((1, r, w), lambda i, s, pt: (i, 0, 0))
    page_specs = []
    for k in range(ATS_PAGES):
        page_specs += [pspec(k, 2), pspec(k, 3)]
    gs = pltpu.PrefetchScalarGridSpec(
        num_scalar_prefetch=1,
        grid=(b, n_steps),
        in_specs=page_specs + [per_b(nr, 256), per_b(nr, 256), per_b(kc.shape[1], 256),
                               per_b(kc.shape[1], 256), per_b(t_dec, 1024),
                               per_b(cache_win2.shape[1], 512), per_b(t_dec, 512), per_b(nr, 128)],
        out_specs=per_b(nr, 256),
        scratch_shapes=[pltpu.VMEM((n_steps + 1, nr, 128), BF16), pltpu.VMEM((nr, 256), F32),
                        pltpu.VMEM((nr, 1), F32), pltpu.VMEM((nr, 1), F32),
                        pltpu.VMEM((nr, 256), F32)],
    )
    return pl.pallas_call(
        functools.partial(_attn_sample_kernel, n_steps=n_steps, past=past, t_dec=t_dec),
        out_shape=jax.ShapeDtypeStruct((b, nr, 256), F32),
        grid_spec=gs,
        compiler_params=_cparams(("parallel", "arbitrary")),
        name="attn_sample",
    )(page_table, *([cache3] * (2 * ATS_PAGES)), qn_x, qr_x, kc, vc, kv_new, cache_win2, win_new,
      gt_rows)


def _expand_rows(q):
    b, t, _ = q.shape
    q5 = q.reshape(b, t, N_KV, HPG, HEAD_DIM)
    eye = jnp.eye(N_KV, dtype=q.dtype)
    return jnp.einsum('btghd,gk->bgthkd', q5, eye).reshape(b, N_KV * t * HPG, N_KV * HEAD_DIM)


def _collapse_rows(o, t):
    b = o.shape[0]
    o6 = o.reshape(b, N_KV, t, HPG, N_KV, HEAD_DIM)
    return jnp.einsum('bgthkd,gk->btghd', o6, jnp.eye(N_KV, dtype=o.dtype)).reshape(b, t, D_ATT)


def _layernorm(x, g, b):
    mu = jnp.mean(x, axis=-1, keepdims=True)
    xc = x - mu
    var = jnp.mean(xc * xc, axis=-1, keepdims=True)
    return xc * lax.rsqrt(var + LN_EPS) * g + b


def _outproj_kernel(oa_ref, ys_ref, x_ref, g1_ref, sc2_ref, sh2_ref, w_ref, lg_ref, lb_ref,
                    rw_ref, rb_ref, x1_ref, h2_ref, ti_ref, tw_ref):
    mix = _dot(oa_ref[0], w_ref[0:D_ATT, :]) + _dot(ys_ref[0], w_ref[D_ATT:, :])
    x1 = _layernorm(DN_ALPHA * x_ref[0] + g1_ref[0] * mix, lg_ref[...], lb_ref[...])
    x1_ref[0] = x1
    h2 = x1 * (1.0 + sc2_ref[0]) + sh2_ref[0]
    h2_ref[0] = h2.astype(BF16)
    logits = _dot(h2, rw_ref[...], precision=HI) + rb_ref[...]
    lane = lax.broadcasted_iota(jnp.int32, logits.shape, 1)
    vals, ids = [], []
    for _ in range(TOP_K):
        mx = jnp.max(logits, axis=1, keepdims=True)
        ix = jnp.min(jnp.where(logits == mx, lane, 128), axis=1, keepdims=True)
        vals.append(mx)
        ids.append(ix)
        logits = jnp.where(lane == ix, -jnp.inf, logits)
    es = [jnp.exp(v - vals[0]) for v in vals]
    inv = 1.0 / (es[0] + es[1] + es[2] + es[3])
    ti = jnp.zeros(lane.shape, jnp.int32)
    tw = jnp.zeros(lane.shape, F32)
    for k in range(TOP_K):
        ti = jnp.where(lane == k, ids[k], ti)
        tw = jnp.where(lane == k, es[k] * inv, tw)
    ti_ref[0] = ti
    tw_ref[0] = tw


def _out_proj(o_att, y_ssm, x, g1, sc2, sh2, w_out_bf, ln_g, ln_b, rw, rb, tm):
    b, t, _ = x.shape
    row = lambda w: pl.BlockSpec((1, tm, w), lambda i, j: (i, j, 0))
    per_b = pl.BlockSpec((1, 1, D_MODEL), lambda i, j: (i, 0, 0))
    par = lambda r, w, **kw: pl.BlockSpec((r, w), lambda i, j: (0, 0), **kw)
    return pl.pallas_call(
        _outproj_kernel,
        out_shape=[jax.ShapeDtypeStruct((b, t, D_MODEL), F32),
                   jax.ShapeDtypeStruct((b, t, D_MODEL), BF16),
                   jax.ShapeDtypeStruct((b, t, 128), jnp.int32),
                   jax.ShapeDtypeStruct((b, t, 128), F32)],
        grid=(b, t // tm),
        in_specs=[row(D_ATT), row(D_SSM), row(D_MODEL), per_b, per_b, per_b,
                  par(D_MODEL, D_MODEL, pipeline_mode=pl.Buffered(1)),
                  par(1, D_MODEL), par(1, D_MODEL), par(D_MODEL, 128), par(1, 128)],
        out_specs=[row(D_MODEL), row(D_MODEL), row(128), row(128)],
        compiler_params=_cparams(("parallel", "parallel")),
        name="out_proj",
    )(o_att, y_ssm, x, g1, sc2, sh2, w_out_bf, ln_g, ln_b, rw, rb)


MOE_TM = 1024
MOE_TN = 512
ROUTE_TM = 256


def _route_kernel(ti_ref, rank_ref, cnt_ref):
    @pl.when(pl.program_id(0) == 0)
    def _():
        cnt_ref[...] = jnp.zeros_like(cnt_ref)

    ti = ti_ref[...]
    tm = ti.shape[0]
    lane = lax.broadcasted_iota(jnp.int32, (tm, 128), 1)
    hits = [lane == ti[:, k:k + 1] for k in range(TOP_K)]
    oh = jnp.zeros((tm, 128), F32)
    for h in hits:
        oh = oh + h.astype(F32)
    ri = lax.broadcasted_iota(jnp.int32, (tm, tm), 0)
    ci = lax.broadcasted_iota(jnp.int32, (tm, tm), 1)
    before = _dot((ri > ci).astype(BF16), oh.astype(BF16)) + cnt_ref[0:1, :]
    rank = jnp.zeros((tm, 128), jnp.int32)
    for k, h in enumerate(hits):
        rk = jnp.sum(jnp.where(h, before, 0.0), axis=1, keepdims=True).astype(jnp.int32)
        rank = jnp.where(lane == k, rk, rank)
    rank_ref[...] = rank
    cnt_ref[...] = cnt_ref[...] + jnp.sum(oh, axis=0, keepdims=True)


def _route(ti_all):
    n = ti_all.shape[0]
    return pl.pallas_call(
        _route_kernel,
        out_shape=[jax.ShapeDtypeStruct((n, 128), jnp.int32),
                   jax.ShapeDtypeStruct((8, 128), F32)],
        grid=(n // ROUTE_TM,),
        in_specs=[pl.BlockSpec((ROUTE_TM, 128), lambda i: (i, 0))],
        out_specs=[pl.BlockSpec((ROUTE_TM, 128), lambda i: (i, 0)),
                   pl.BlockSpec((8, 128), lambda i: (0, 0))],
        compiler_params=_cparams(("arbitrary",)),
        name="moe_route",
    )(ti_all)


def _dispatch_kernel(pos_ref, h_ref, xs_in_ref, xs_ref, sem):
    del xs_in_ref
    i = pl.program_id(0)
    n = ROUTE_TM * TOP_K

    def copy(r):
        tok = i * ROUTE_TM + r // TOP_K
        return pltpu.make_async_copy(h_ref.at[tok], xs_ref.at[pos_ref[0, 0, r]], sem)

    def start(r, c):
        copy(r).start()
        return c

    def wait(r, c):
        copy(r).wait()
        return c

    lax.fori_loop(0, n, start, 0)
    lax.fori_loop(0, n, wait, 0)


def _dispatch(pos, h_all, n_rows):
    n = h_all.shape[0]
    steps = n // ROUTE_TM
    s = D_MODEL // 128
    xs0 = jnp.zeros((n_rows, s, 128), BF16)
    out = pl.pallas_call(
        _dispatch_kernel,
        out_shape=jax.ShapeDtypeStruct((n_rows, s, 128), BF16),
        grid=(steps,),
        in_specs=[pl.BlockSpec((1, 1, ROUTE_TM * TOP_K), lambda i: (i, 0, 0), memory_space=pltpu.SMEM),
                  pl.BlockSpec(memory_space=pl.ANY),
                  pl.BlockSpec(memory_space=pl.ANY)],
        out_specs=pl.BlockSpec(memory_space=pl.ANY),
        scratch_shapes=[pltpu.SemaphoreType.DMA(())],
        input_output_aliases={2: 0},
        compiler_params=_cparams(("arbitrary",)),
        name="moe_dispatch",
    )(pos.reshape(steps, 1, ROUTE_TM * TOP_K), h_all.reshape(n, s, 128), xs0)
    return out.reshape(n_rows, D_MODEL)


def _moe_up_kernel(te_ref, nt_ref, xs_ref, wg_ref, wu_ref, bg_ref, bu_ref, act_ref):
    @pl.when(pl.program_id(0) < nt_ref[0])
    def _():
        x = xs_ref[...]
        g = _dot(x, wg_ref[0].astype(BF16)) + bg_ref[0]
        u = _dot(x, wu_ref[0].astype(BF16)) + bu_ref[0]
        g = jnp.minimum(g, SWIGLU_LIMIT)
        u = jnp.clip(u, -SWIGLU_LIMIT, SWIGLU_LIMIT)
        act_ref[...] = ((u + 1.0) * g * _sigmoid(SWIGLU_ALPHA * g)).astype(BF16)

    @pl.when(pl.program_id(0) >= nt_ref[0])
    def _():
        act_ref[...] = jnp.zeros_like(act_ref)


def _moe_up(tile_e, n_used, xs, w_gu, b_gu3):
    n_rows = xs.shape[0]
    nj = D_FF // MOE_TN
    row_i = lambda i, j, te, nt: (jnp.minimum(i, nt[0] - 1), 0)
    jc = lambda i, j, nt: jnp.where(i < nt[0], j, nj - 1)
    gs = pltpu.PrefetchScalarGridSpec(
        num_scalar_prefetch=2,
        grid=(n_rows // MOE_TM, nj),
        in_specs=[pl.BlockSpec((MOE_TM, D_MODEL), row_i),
                  pl.BlockSpec((1, D_MODEL, MOE_TN), lambda i, j, te, nt: (te[i], 0, jc(i, j, nt))),
                  pl.BlockSpec((1, D_MODEL, MOE_TN), lambda i, j, te, nt: (te[i], 0, jc(i, j, nt) + nj)),
                  pl.BlockSpec((1, 1, MOE_TN), lambda i, j, te, nt: (te[i], 0, jc(i, j, nt))),
                  pl.BlockSpec((1, 1, MOE_TN), lambda i, j, te, nt: (te[i], 0, jc(i, j, nt) + nj))],
        out_specs=pl.BlockSpec((MOE_TM, MOE_TN), lambda i, j, te, nt: (i, j)),
    )
    return pl.pallas_call(
        _moe_up_kernel,
        out_shape=jax.ShapeDtypeStruct((n_rows, D_FF), BF16),
        grid_spec=gs,
        compiler_params=_cparams(("parallel", "arbitrary")),
        name="moe_up",
    )(tile_e, n_used, xs, w_gu, w_gu, b_gu3, b_gu3)


def _moe_down_kernel(te_ref, nt_ref, a_ref, w_ref, b_ref, y_ref):
    @pl.when(pl.program_id(0) < nt_ref[0])
    def _():
        y_ref[...] = _dot(a_ref[...], w_ref[0].astype(BF16)) + b_ref[0]

    @pl.when(pl.program_id(0) >= nt_ref[0])
    def _():
        y_ref[...] = jnp.zeros_like(y_ref)


def _moe_down(tile_e, n_used, act, w_down, b_down3):
    n_rows = act.shape[0]
    nj = D_MODEL // MOE_TN
    row_i = lambda i, j, te, nt: (jnp.minimum(i, nt[0] - 1), 0)
    jc = lambda i, j, nt: jnp.where(i < nt[0], j, nj - 1)
    gs = pltpu.PrefetchScalarGridSpec(
        num_scalar_prefetch=2,
        grid=(n_rows // MOE_TM, nj),
        in_specs=[pl.BlockSpec((MOE_TM, D_FF), row_i),
                  pl.BlockSpec((1, D_FF, MOE_TN), lambda i, j, te, nt: (te[i], 0, jc(i, j, nt))),
                  pl.BlockSpec((1, 1, MOE_TN), lambda i, j, te, nt: (te[i], 0, jc(i, j, nt)))],
        out_specs=pl.BlockSpec((MOE_TM, MOE_TN), lambda i, j, te, nt: (i, j)),
    )
    return pl.pallas_call(
        _moe_down_kernel,
        out_shape=jax.ShapeDtypeStruct((n_rows, D_MODEL), F32),
        grid_spec=gs,
        compiler_params=_cparams(("parallel", "arbitrary")),
        name="moe_down",
    )(tile_e, n_used, act, w_down, b_down3)


def _combine_kernel(pos_ref, yd_ref, x1_ref, tw_ref, g2_ref, lg_ref, lb_ref, o_ref, buf, sem):
    n = buf.shape[1] * TOP_K

    def copy(r):
        k = r % TOP_K
        t = r // TOP_K
        return pltpu.make_async_copy(yd_ref.at[pl.ds(pos_ref[0, 0, r], 1), :],
                                     buf.at[k, pl.ds(t, 1), :], sem)

    def start(r, c):
        copy(r).start()
        return c

    def wait(r, c):
        copy(r).wait()
        return c

    lax.fori_loop(0, n, start, 0)
    lax.fori_loop(0, n, wait, 0)
    tw = tw_ref[0]
    moe = buf[0] * tw[:, 0:1]
    for k in range(1, TOP_K):
        moe = moe + buf[k] * tw[:, k:k + 1]
    o_ref[0] = _layernorm(DN_ALPHA * x1_ref[0] + g2_ref[0] * moe, lg_ref[...], lb_ref[...])


def _combine(pos, yd, x1, tw, g2, ln_g, ln_b, tok0):
    b, t, _ = x1.shape
    tm = min(ROUTE_TM, t)
    per = t // tm
    step0 = tok0 // tm
    row = lambda w: pl.BlockSpec((1, tm, w), lambda i, j: (i, j, 0))
    par = pl.BlockSpec((1, D_MODEL), lambda i, j: (0, 0))
    return pl.pallas_call(
        functools.partial(_combine_kernel),
        out_shape=jax.ShapeDtypeStruct((b, t, D_MODEL), F32),
        grid=(b, per),
        in_specs=[pl.BlockSpec((1, 1, tm * TOP_K), lambda i, j: (step0 + i * per + j, 0, 0),
                               memory_space=pltpu.SMEM),
                  pl.BlockSpec(memory_space=pl.ANY),
                  row(D_MODEL), row(128),
                  pl.BlockSpec((1, 1, D_MODEL), lambda i, j: (i, 0, 0)), par, par],
        out_specs=row(D_MODEL),
        scratch_shapes=[pltpu.VMEM((TOP_K, tm, D_MODEL), F32), pltpu.SemaphoreType.DMA(())],
        compiler_params=_cparams(("arbitrary", "arbitrary")),
        name="moe_combine",
    )(pos.reshape(-1, 1, tm * TOP_K), yd, x1, tw, g2, ln_g, ln_b)


IN_TM = 256
SSD_Q = 256
OUT_TM = 512


def kernel(x_prompt, x_sample, cache_kv, cache_win, state_conv, state_ssm, page_table,
           c_prompt, c_sample, w_ada, b_ada, w_in, cmp_pe, cmp_w1, cmp_b1, cmp_w2,
           conv_w, conv_b, dt_bias, a_log, d_skip, ssm_norm_g, w_out, ln1_g, ln1_b,
           router_w, router_b, w_gu, b_gu, w_down, b_down, ln2_g, ln2_b):
    bp, tp, _ = x_prompt.shape
    bs, ts, _ = x_sample.shape
    past = page_table.shape[1] * PAGE

    m = _ada(jnp.concatenate([c_prompt, c_sample], axis=0), w_ada[0], b_ada[0])
    mod = [m[:, None, k * D_MODEL:(k + 1) * D_MODEL] for k in range(6)]
    mod_p = [a[:bp] for a in mod]
    mod_s = [a[bp:] for a in mod]

    perm = _q_perm()
    inv_perm = np.argsort(perm)
    w_bf = _prep_w_in(w_in[0])
    cw = _prep_cmp_weights(cmp_pe[0], cmp_w1[0], cmp_b1[0], cmp_w2[0])
    sp = _prep_ssm_params(conv_w[0], conv_b[0], dt_bias[0], a_log[0], d_skip[0], ssm_norm_g[0])
    w_out_bf = jnp.concatenate([w_out[0][:D_ATT][perm], w_out[0][D_ATT:]], axis=0).astype(BF16)
    rw = jnp.concatenate([router_w[0], jnp.zeros((D_MODEL, 128 - N_EXPERTS), F32)], axis=1)
    rb = jnp.concatenate([router_b[0], jnp.full((128 - N_EXPERTS,), NEG, F32)]).reshape(1, 128)
    ln1 = (ln1_g[0].reshape(1, D_MODEL), ln1_b[0].reshape(1, D_MODEL))
    ln2 = (ln2_g[0].reshape(1, D_MODEL), ln2_b[0].reshape(1, D_MODEL))

    tab_p = _rope_tables(jnp.arange(tp))
    qn, qr, kv_p, win_p, kvb, z, xbc, gd = _in_proj(x_prompt, mod_p[1], mod_p[0], w_bf, tab_p, IN_TM)
    kc, vc = _compress_prompt(kv_p, cw)
    o_att = _attn_prompt(qn, qr, kc, vc, kvb, gd)
    y_ssm, ssm_p = _ssd(xbc, z, gd, jnp.zeros((bp, 8, CONV_CH), F32),
                        jnp.zeros((bp, D_SSM, D_STATE), F32), sp, SSD_Q)
    x1_p, h2_p, ti_p, tw_p = _out_proj(o_att, y_ssm, x_prompt, mod_p[2], mod_p[4], mod_p[3],
                                       w_out_bf, *ln1, rw, rb, OUT_TM)

    tab_s = _rope_tables(past + jnp.arange(ts))
    qn_s, qr_s, kv_s, win_s, _, z_s, xbc_s, gd_s = _in_proj(
        x_sample, mod_s[1], mod_s[0], w_bf, tab_s, ts)
    cache3 = cache_kv[0].reshape(cache_kv.shape[1], PAGE, 4 * N_KV * HEAD_DIM)
    kc_s, vc_s = _compress_sample(cache3, page_table, cw)
    wb = cache_win.shape[2]
    cwin2 = cache_win[0].reshape(bs, wb, 2 * N_KV * HEAD_DIM)
    gt_rows = gd_s[:, :, :D_GT].reshape(bs, ts, N_KV, HPG, 3).transpose(0, 2, 1, 3, 4)
    gt_rows = jnp.concatenate([gt_rows.reshape(bs, N_ROWS_S, 3),
                               jnp.zeros((bs, N_ROWS_S, 125), F32)], axis=-1)
    o_rows = _attn_sample(cache3, page_table, _expand_rows(qn_s[:, :, inv_perm]),
                          _expand_rows(qr_s[:, :, inv_perm]), kc_s, vc_s, kv_s, cwin2, win_s,
                          gt_rows, ts)
    o_att_s = _collapse_rows(o_rows, ts)[:, :, perm].astype(BF16)
    cprev = jnp.concatenate([jnp.zeros((bs, 5, CONV_CH), F32), state_conv[0]], axis=1)
    y_ssm_s, ssm_s = _ssd(xbc_s, z_s, gd_s, cprev, state_ssm[0].reshape(bs, D_SSM, D_STATE), sp, ts)
    x1_s, h2_s, ti_s, tw_s = _out_proj(o_att_s, y_ssm_s, x_sample, mod_s[2], mod_s[4], mod_s[3],
                                       w_out_bf, *ln1, rw, rb, ts)

    n_tok = bp * tp + bs * ts
    n_tiles = -(-(n_tok * TOP_K + N_EXPERTS * (MOE_TM - 1)) // MOE_TM)
    ti_all = jnp.concatenate([ti_p.reshape(-1, 128), ti_s.reshape(-1, 128)], axis=0)
    rank, cnt = _route(ti_all)
    counts = cnt[0, :N_EXPERTS].astype(jnp.int32)
    padded = (counts + MOE_TM - 1) // MOE_TM * MOE_TM
    ends = jnp.cumsum(padded)
    offs = ends - padded
    n_used = (ends[-1] // MOE_TM).astype(jnp.int32).reshape(1)
    tiles = jnp.minimum(jnp.arange(n_tiles, dtype=jnp.int32), n_used[0] - 1)
    tile_e = jnp.sum((tiles[:, None] * MOE_TM >= ends[None, :]).astype(jnp.int32), axis=1)
    tile_e = jnp.minimum(tile_e, N_EXPERTS - 1).astype(jnp.int32)
    pos = (offs[ti_all[:, :TOP_K]] + rank[:, :TOP_K]).astype(jnp.int32)
    h2_all = jnp.concatenate([h2_p.reshape(-1, D_MODEL), h2_s.reshape(-1, D_MODEL)], axis=0)
    xs = _dispatch(pos, h2_all, n_tiles * MOE_TM)
    act = _moe_up(tile_e, n_used, xs, w_gu[0], b_gu[0].reshape(N_EXPERTS, 1, 2 * D_FF))
    yd = _moe_down(tile_e, n_used, act, w_down[0], b_down[0].reshape(N_EXPERTS, 1, D_MODEL))
    y_p = _combine(pos, yd, x1_p, tw_p, mod_p[5], *ln2, 0)
    y_s = _combine(pos, yd, x1_s, tw_s, mod_s[5], *ln2, bp * tp)

    kv_shape = (4, N_KV, HEAD_DIM)
    win_shape = (2, N_KV, HEAD_DIM)
    win_prompt = win_p[:, tp - min(WINDOW, tp):].reshape((1, bp, min(WINDOW, tp)) + win_shape)
    win_sample = jnp.concatenate([cwin2, win_s], axis=1)[:, -wb:].reshape((1, bs, wb) + win_shape)
    return (y_p, y_s,
            kv_p.reshape((1, bp, tp) + kv_shape), kv_s.reshape((1, bs, ts) + kv_shape),
            win_prompt, win_sample,
            xbc[None, :, tp - (CONV_W - 1):], xbc_s[None, :, ts - (CONV_W - 1):],
            ssm_p.reshape(1, bp, N_HEADS_SSM, HEAD_DIM, D_STATE),
            ssm_s.reshape(1, bs, N_HEADS_SSM, HEAD_DIM, D_STATE))
```

```python
import functools
import math

import jax
import jax.numpy as jnp
import numpy as np
from jax import lax
from jax.experimental import pallas as pl
from jax.experimental.pallas import tpu as pltpu

D_MODEL = 2048
D_ATT = 1024
D_SSM = 1024
HEAD_DIM = 64
N_HEADS_ATT = 16
N_KV = 4
HPG = 4
ROT_DIM = 16
ROPE_THETA = 500000.0
CMP_LEN = 32
CMP_STRIDE = 16
CMP_HID = 128
SLC_BLK = 64
N_SELECT = 16
WINDOW = 512
N_HEADS_SSM = 16
SSM_GROUPS = 4
D_STATE = 128
CONV_W = 4
CONV_CH = 2048
N_EXPERTS = 32
TOP_K = 4
D_FF = 2048
SWIGLU_LIMIT = 7.0
SWIGLU_ALPHA = 1.702
DN_ALPHA = 2.0 ** 0.25
LN_EPS = 1e-5
RMS_EPS = 1e-5
NEG = -1e30
FORCE = 1e6
PAGE = 128

D_KV = 6 * N_KV * HEAD_DIM
D_GT = 3 * N_HEADS_ATT
D_IN = D_ATT + D_KV + D_GT + D_SSM + CONV_CH + N_HEADS_SSM
D_INP = D_ATT + D_KV + D_SSM + CONV_CH + 128

LANES = 128
VMEM_LIMIT = 56 * 1024 * 1024

F32 = jnp.float32
BF16 = jnp.bfloat16
HI = lax.Precision.HIGHEST


def _cparams(sem, vmem=VMEM_LIMIT, no_bounds_checks=False):
    return pltpu.CompilerParams(dimension_semantics=sem, vmem_limit_bytes=vmem,
                                disable_bounds_checks=no_bounds_checks)


def _dot(a, b, precision=None):
    return jnp.dot(a, b, preferred_element_type=F32, precision=precision)


def _dot_nt(a, b, precision=None):
    return lax.dot_general(a, b, (((1,), (1,)), ((), ())), preferred_element_type=F32,
                           precision=precision)


def _dot_tn(a, b, precision=None):
    return lax.dot_general(a, b, (((0,), (0,)), ((), ())), preferred_element_type=F32,
                           precision=precision)


def _sigmoid(x):
    return 1.0 / (1.0 + jnp.exp(-x))


def _silu(x):
    return x * _sigmoid(x)


def _ada_kernel(c_ref, w_ref, b_ref, o_ref):
    c = c_ref[...]
    a = _silu(c).astype(BF16)
    o_ref[...] = _dot(a, w_ref[...].astype(BF16)) + b_ref[...]


def _ada(c_all, w_ada, b_ada):
    nb = c_all.shape[0]
    tn = 1024
    n = w_ada.shape[1]
    return pl.pallas_call(
        _ada_kernel,
        out_shape=jax.ShapeDtypeStruct((nb, n), F32),
        grid=(n // tn,),
        in_specs=[pl.BlockSpec((nb, D_MODEL), lambda j: (0, 0)),
                  pl.BlockSpec((D_MODEL, tn), lambda j: (0, j)),
                  pl.BlockSpec((1, tn), lambda j: (0, j))],
        out_specs=pl.BlockSpec((nb, tn), lambda j: (0, j)),
        compiler_params=_cparams(("parallel",)),
        name="ada",
    )(c_all, w_ada, b_ada.reshape(1, n))


def _q_perm():
    idx = np.zeros(D_ATT, np.int32)
    for gp in range(2):
        for j in range(HPG):
            for side in range(2):
                g = 2 * gp + side
                for d in range(HEAD_DIM):
                    idx[gp * 512 + j * 128 + side * 64 + d] = g * 256 + j * 64 + d
    return idx


def _prep_w_in(w_in):
    o_kv = D_ATT
    o_gt = o_kv + D_KV
    o_z = o_gt + D_GT
    o_xbc = o_z + D_SSM
    o_dt = o_xbc + CONV_CH
    wq = w_in[:, :D_ATT][:, _q_perm()]
    pad = jnp.zeros((D_MODEL, 128 - D_GT - N_HEADS_SSM), w_in.dtype)
    w = jnp.concatenate([wq, w_in[:, o_kv:o_gt], w_in[:, o_z:o_xbc], w_in[:, o_xbc:o_dt],
                         w_in[:, o_gt:o_z], w_in[:, o_dt:], pad], axis=1)
    return w.astype(BF16)


def _rope_tables(pos):
    half = ROT_DIM // 2
    inv = ROPE_THETA ** (-jnp.arange(half, dtype=F32) / half)
    ang = pos.astype(F32)[:, None] * inv
    cos, sin = jnp.cos(ang), jnp.sin(ang)
    t = pos.shape[0]
    one = jnp.ones((t, HEAD_DIM - ROT_DIM), F32)
    zero = jnp.zeros((t, HEAD_DIM - ROT_DIM), F32)
    z8 = jnp.zeros((t, half), F32)
    c64 = jnp.concatenate([cos, cos, one], 1)
    s1 = jnp.concatenate([z8, sin, zero], 1)
    s2 = jnp.concatenate([-sin, z8, zero], 1)
    rep = lambda a: jnp.tile(a, (1, 4))
    return jnp.concatenate([rep(c64), rep(s1), rep(s2)], axis=1)


def _rope256(x, tab):
    n = x.shape[1]
    return (x * tab[:, 0:256] + pltpu.roll(x, 8, 1) * tab[:, 256:512]
            + pltpu.roll(x, n - 8, 1) * tab[:, 512:768])


def _inproj_kernel(x_ref, sc_ref, sh_ref, w_ref, tab_ref,
                   qn_ref, qr_ref, kv_ref, win_ref, kvb_ref, z_ref, xbc_ref, gd_ref):
    h = (x_ref[0] * (1.0 + sc_ref[0]) + sh_ref[0]).astype(BF16)
    tab = tab_ref[...]
    for c in range(4):
        q = _dot(h, w_ref[:, c * 256:(c + 1) * 256])
        qn_ref[0, :, c * 256:(c + 1) * 256] = q.astype(BF16)
        qr_ref[0, :, c * 256:(c + 1) * 256] = _rope256(q, tab).astype(BF16)
    o = D_ATT
    for p in range(6):
        y = _dot(h, w_ref[:, o + p * 256:o + (p + 1) * 256])
        if p in (2, 4):
            y = _rope256(y, tab)
        if p < 4:
            kv_ref[0, :, p * 256:(p + 1) * 256] = y
        else:
            win_ref[0, :, (p - 4) * 256:(p - 3) * 256] = y
        if p >= 2:
            kvb_ref[0, :, (p - 2) * 256:(p - 1) * 256] = y.astype(BF16)
    o += D_KV
    for c in range(2):
        z_ref[0, :, c * 512:(c + 1) * 512] = _dot(h, w_ref[:, o + c * 512:o + (c + 1) * 512]).astype(BF16)
    o += D_SSM
    for c in range(4):
        xbc_ref[0, :, c * 512:(c + 1) * 512] = _dot(h, w_ref[:, o + c * 512:o + (c + 1) * 512])
    o += CONV_CH
    gd_ref[0] = _dot(h, w_ref[:, o:o + 128])


def _in_proj(x, sc, sh, w_bf, tab, tm):
    b, t, _ = x.shape
    nt = t // tm
    row = lambda w: pl.BlockSpec((1, tm, w), lambda i, j: (i, j, 0))
    outs = [(D_ATT, BF16), (D_ATT, BF16), (1024, F32), (512, F32), (1024, BF16),
            (D_SSM, BF16), (CONV_CH, F32), (128, F32)]
    return pl.pallas_call(
        _inproj_kernel,
        out_shape=[jax.ShapeDtypeStruct((b, t, w), dt) for w, dt in outs],
        grid=(b, nt),
        in_specs=[row(D_MODEL),
                  pl.BlockSpec((1, 1, D_MODEL), lambda i, j: (i, 0, 0)),
                  pl.BlockSpec((1, 1, D_MODEL), lambda i, j: (i, 0, 0)),
                  pl.BlockSpec((D_MODEL, D_INP), lambda i, j: (0, 0), pipeline_mode=pl.Buffered(1)),
                  pl.BlockSpec((tm, 768), lambda i, j: (j, 0))],
        out_specs=[row(w) for w, _ in outs],
        compiler_params=_cparams(("parallel", "parallel")),
        name="in_proj",
    )(x, sc, sh, w_bf, tab)


def _prep_cmp_weights(cmp_pe, cmp_w1, cmp_b1, cmp_w2):
    w1 = cmp_w1.reshape(2, 2, CMP_STRIDE, HEAD_DIM, CMP_HID)
    eye = jnp.eye(2, dtype=cmp_w1.dtype)
    w1p = jnp.einsum('phsdj,ab->psadhbj', w1, eye).reshape(2, CMP_STRIDE * 128, 4 * CMP_HID)
    w2p = jnp.einsum('pjd,ab->pajbd', cmp_w2, eye).reshape(2, 2 * CMP_HID, 2 * HEAD_DIM)
    pe = cmp_pe.reshape(2, 1, CMP_LEN * HEAD_DIM)
    pe = jnp.concatenate([pe, jnp.zeros((2, 7, CMP_LEN * HEAD_DIM), pe.dtype)], axis=1)
    b1 = jnp.concatenate([cmp_b1, cmp_b1], axis=-1).reshape(2, 1, 2 * CMP_HID)
    return w1p.astype(BF16), w2p.astype(BF16), pe, cmp_w1, b1


def _gelu_tanh(x):
    return 0.5 * x * (1.0 + jnp.tanh(0.7978845608028654 * (x + 0.044715 * x * x * x)))


def _compress_core(load, nrows, w1p_ref, w2p_ref, pe_ref, w1_ref, b1_ref, store):
    for p in range(2):
        pe_bias = _dot(pe_ref[p], w1_ref[p], precision=HI)[0:1]
        pe_bias = jnp.concatenate([pe_bias, pe_bias], axis=1)
        for pr in range(2):
            xs = jnp.concatenate([load(s, 2 * p + pr).astype(BF16) for s in range(CMP_STRIDE)], axis=1)
            ab = _dot(xs, w1p_ref[p])
            a, bb = ab[:, 0:256], ab[:, 256:512]
            pre = a + pltpu.roll(bb, nrows - 1, 0) + pe_bias + b1_ref[p]
            hid = _gelu_tanh(pre).astype(BF16)
            store(p, pr, _dot(hid, w2p_ref[p]))


def _compress_prompt_kernel(b0, b1, b2, b3, w1p_ref, w2p_ref, pe_ref, w1_ref, b1_ref, kc_ref, vc_ref):
    blocks = (b0, b1, b2, b3)
    n = b0.shape[1] // CMP_STRIDE
    outs = (kc_ref, vc_ref)

    def load(s, lb):
        return blocks[lb][0, pl.ds(s, n, stride=CMP_STRIDE), :]

    def store(p, pr, val):
        outs[p][0, :, pr * 128:(pr + 1) * 128] = val.astype(BF16)

    _compress_core(load, n, w1p_ref, w2p_ref, pe_ref, w1_ref, b1_ref, store)


def _wspecs():
    z3 = lambda *a: (0, 0, 0)
    return [pl.BlockSpec((2, CMP_STRIDE * 128, 4 * CMP_HID), z3),
            pl.BlockSpec((2, 2 * CMP_HID, 2 * HEAD_DIM), z3),
            pl.BlockSpec((2, 8, CMP_LEN * HEAD_DIM), z3),
            pl.BlockSpec((2, CMP_LEN * HEAD_DIM, CMP_HID), z3),
            pl.BlockSpec((2, 1, 2 * CMP_HID), z3)]


def _compress_prompt(kv_f32, cw):
    b, t, _ = kv_f32.shape
    n = t // CMP_STRIDE
    lane_blk = lambda lb: pl.BlockSpec((1, t, 128), lambda i: (i, 0, lb))
    return pl.pallas_call(
        _compress_prompt_kernel,
        out_shape=[jax.ShapeDtypeStruct((b, n, 256), BF16)] * 2,
        grid=(b,),
        in_specs=[lane_blk(lb) for lb in range(4)] + _wspecs(),
        out_specs=[pl.BlockSpec((1, n, 256), lambda i: (i, 0, 0))] * 2,
        compiler_params=_cparams(("parallel",)),
        name="compress_prompt",
    )(kv_f32, kv_f32, kv_f32, kv_f32, *cw)


CMP_PAGES = 32


def _compress_sample_kernel(pt_ref, *refs):
    npg = CMP_PAGES + 1
    pages = refs[:npg]
    w1p_ref, w2p_ref, pe_ref, w1_ref, b1_ref, kc_ref, vc_ref, tok_ref = refs[npg:]
    cpp = PAGE // CMP_STRIDE
    nrows = CMP_PAGES * cpp
    outs = (kc_ref, vc_ref)
    for k in range(npg):
        for lb in range(4):
            tok_ref[lb, k * PAGE:(k + 1) * PAGE, :] = pages[k][0, lb * 128:(lb + 1) * 128, :].T

    def load(s, lb):
        return tok_ref[lb, pl.ds(s, nrows + cpp, stride=CMP_STRIDE), :]

    def store(p, pr, val):
        outs[p][0, :, pr * 128:(pr + 1) * 128] = val[:nrows].astype(BF16)

    _compress_core(load, nrows + cpp, w1p_ref, w2p_ref, pe_ref, w1_ref, b1_ref, store)


def _compress_sample(cache_t, page_table, cw):
    b, n_pages = page_table.shape
    steps = n_pages // CMP_PAGES
    nrows = CMP_PAGES * (PAGE // CMP_STRIDE)

    def pspec(k):
        return pl.BlockSpec(
            (1, 512, PAGE),
            lambda i, j, pt: (pt[i, jnp.minimum(j * CMP_PAGES + k, n_pages - 1)], 0, 0))

    gs = pltpu.PrefetchScalarGridSpec(
        num_scalar_prefetch=1,
        grid=(b, steps),
        in_specs=[pspec(k) for k in range(CMP_PAGES + 1)]
                 + [pl.BlockSpec(s.block_shape, lambda i, j, pt: (0, 0, 0)) for s in _wspecs()],
        out_specs=[pl.BlockSpec((1, nrows, 256), lambda i, j, pt: (i, j, 0))] * 2,
        scratch_shapes=[pltpu.VMEM((4, (CMP_PAGES + 1) * PAGE, 128), F32)],
    )
    return pl.pallas_call(
        _compress_sample_kernel,
        out_shape=[jax.ShapeDtypeStruct((b, n_pages * 8, 256), BF16)] * 2,
        grid_spec=gs,
        compiler_params=_cparams(("parallel", "parallel")),
        name="compress_sample",
    )(page_table, *([cache_t] * (CMP_PAGES + 1)), *cw)


DT_LANE = D_GT


def _prep_ssm_params(conv_w, conv_b, dt_bias, a_log, d_skip, ssm_norm_g):
    cw = jnp.concatenate([conv_w, jnp.zeros((8 - CONV_W, CONV_CH), conv_w.dtype)], axis=0)
    lane = lambda v: jnp.zeros((1, 128), F32).at[0, DT_LANE:DT_LANE + N_HEADS_SSM].set(v)
    dsk = jnp.repeat(d_skip, HEAD_DIM).reshape(1, D_SSM)
    return (cw, conv_b.reshape(1, CONV_CH), lane(dt_bias), lane(a_log), dsk,
            ssm_norm_g.reshape(1, D_SSM))


def _ssd_kernel(xbc_ref, z_ref, gd_ref, cprev_ref, h0_ref, cw_ref, cb_ref, dtb_ref, alog_ref,
                dsk_ref, ng_ref, y_ref, st_ref, xpad_ref):
    q = xbc_ref.shape[1]
    c = pl.program_id(1)

    @pl.when(c == 0)
    def _():
        st_ref[0] = h0_ref[0]
        xpad_ref[0:8, :] = cprev_ref[0]

    xpad_ref[8:8 + q, :] = xbc_ref[0]
    conv = cb_ref[...] + xpad_ref[pl.ds(5, q), :] * cw_ref[0:1, :]
    for k in range(1, CONV_W):
        conv = conv + xpad_ref[pl.ds(5 + k, q), :] * cw_ref[k:k + 1, :]
    xpad_ref[0:8, :] = xpad_ref[q:q + 8, :]
    xc = _silu(conv)
    xs = xc[:, 0:D_SSM]

    lane = lax.broadcasted_iota(jnp.int32, (1, 128), 1)
    in_dt = (lane >= DT_LANE) & (lane < DT_LANE + N_HEADS_SSM)
    v = gd_ref[0] + dtb_ref[...]
    dt = jnp.maximum(v, 0.0) + jnp.log1p(jnp.exp(-jnp.abs(v)))
    a = jnp.where(in_dt, -jnp.exp(alog_ref[...]), 0.0) * dt
    er = lax.broadcasted_iota(jnp.int32, (128, D_SSM), 0)
    ec = lax.broadcasted_iota(jnp.int32, (128, D_SSM), 1)
    expand = (er - DT_LANE == ec // HEAD_DIM).astype(F32)
    ri = lax.broadcasted_iota(jnp.int32, (q, q), 0)
    ci = lax.broadcasted_iota(jnp.int32, (q, q), 1)
    causal = ri >= ci
    acs = _dot(causal.astype(F32), a, precision=HI)
    acs_t = acs.T
    dt_x = _dot(jnp.where(in_dt, dt, 0.0), expand, precision=HI)
    acs_x = _dot(acs, expand, precision=HI)
    last_x = acs_x[q - 1:q, :]
    grow_x = jnp.exp(acs_x)
    decay_x = jnp.exp(last_x - acs_x)
    er2 = lax.broadcasted_iota(jnp.int32, (D_SSM, 128), 0)
    ec2 = lax.broadcasted_iota(jnp.int32, (D_SSM, 128), 1)
    expand_t = (ec2 - DT_LANE == er2 // HEAD_DIM).astype(F32)
    tot_col = jnp.exp(_dot(expand_t, acs_t, precision=HI)[:, q - 1:q])

    xd = xs * dt_x
    xdd = (xd * decay_x).astype(BF16)
    xd_b = xd.astype(BF16)
    half = lax.broadcasted_iota(jnp.int32, (1, 128), 1) < HEAD_DIM
    ys = []
    for g in range(SSM_GROUPS):
        bg = xc[:, D_SSM + g * D_STATE:D_SSM + (g + 1) * D_STATE].astype(BF16)
        cg = xc[:, D_SSM + (SSM_GROUPS + g) * D_STATE:D_SSM + (SSM_GROUPS + g + 1) * D_STATE].astype(BF16)
        cbm = _dot_nt(cg, bg)
        for m in (2 * g, 2 * g + 1):
            sl = slice(128 * m, 128 * (m + 1))
            yh = []
            for hh in (2 * m, 2 * m + 1):
                col = acs[:, DT_LANE + hh:DT_LANE + hh + 1]
                row = acs_t[DT_LANE + hh:DT_LANE + hh + 1, :]
                lm = jnp.where(causal, jnp.exp(jnp.where(causal, col - row, 0.0)), 0.0)
                yh.append(_dot((cbm * lm).astype(BF16), xd_b[:, sl]))
            y_diag = jnp.where(half, yh[0], yh[1])
            st = st_ref[0, sl, :]
            y_off = _dot_nt(cg, st.astype(BF16)) * grow_x[:, sl]
            st_ref[0, sl, :] = st * tot_col[sl, :] + _dot_tn(xdd[:, sl], bg)
            ys.append(y_diag + y_off)
    y = jnp.concatenate(ys, axis=1) + dsk_ref[...] * xs
    y = y * _silu(z_ref[0].astype(F32))
    gw = D_SSM // SSM_GROUPS
    outs = []
    for g in range(SSM_GROUPS):
        blk = y[:, g * gw:(g + 1) * gw]
        ms = jnp.mean(blk * blk, axis=1, keepdims=True)
        outs.append(blk * lax.rsqrt(ms + RMS_EPS))
    y_ref[0] = (jnp.concatenate(outs, axis=1) * ng_ref[...]).astype(BF16)


def _ssd(xbc, z, gd, conv_prev8, h0, sp, q):
    b, t, _ = xbc.shape
    nc = t // q
    row = lambda w: pl.BlockSpec((1, q, w), lambda i, j: (i, j, 0))
    per_b = lambda r, w: pl.BlockSpec((1, r, w), lambda i, j: (i, 0, 0))
    par = lambda r, w: pl.BlockSpec((r, w), lambda i, j: (0, 0))
    return pl.pallas_call(
        _ssd_kernel,
        out_shape=[jax.ShapeDtypeStruct((b, t, D_SSM), BF16),
                   jax.ShapeDtypeStruct((b, D_SSM, D_STATE), F32)],
        grid=(b, nc),
        in_specs=[row(CONV_CH), row(D_SSM), row(128), per_b(8, CONV_CH), per_b(D_SSM, D_STATE),
                  par(8, CONV_CH), par(1, CONV_CH), par(1, 128), par(1, 128), par(1, D_SSM),
                  par(1, D_SSM)],
        out_specs=[row(D_SSM), per_b(D_SSM, D_STATE)],
        scratch_shapes=[pltpu.VMEM((q + 8, CONV_CH), F32)],
        compiler_params=_cparams(("parallel", "arbitrary")),
        name="ssd",
    )(xbc, z, gd, conv_prev8, h0, *sp)


ATT_TQ = 256
SCALE = HEAD_DIM ** -0.5


def _select_blocks_t(ps, n_blk):
    blk = lax.broadcasted_iota(jnp.int32, ps.shape, 0)
    rank = jnp.zeros(ps.shape, F32)
    for i in range(n_blk):
        vi = ps[i:i + 1, :]
        rank = rank + jnp.where(vi > ps, 1.0, jnp.where((vi == ps) & (blk > i), 1.0, 0.0))
    return jnp.where(rank < N_SELECT, 1.0, 0.0)


def _attn_prompt_kernel(qn_ref, qr_ref, kc_ref, vc_ref, ks_ref, vs_ref, kw_ref, vw_ref, gd_ref,
                        o_ref, qz_ref, sel_ref, ocmp_ref, m_ref, acc_ref):
    tq = qn_ref.shape[1]
    t_all = ks_ref.shape[1]
    n_cmp = t_all // CMP_STRIDE - 1
    n_blk = t_all // SLC_BLK
    gp = pl.program_id(1)
    qi = pl.program_id(2)
    rep = lambda a: jnp.concatenate([a] * HPG, axis=1)
    pos = qi * tq + lax.broadcasted_iota(jnp.int32, (1, tq), 1)
    pos4 = rep(pos)
    lane = lax.broadcasted_iota(jnp.int32, (1, 128), 1)
    row = lax.broadcasted_iota(jnp.int32, (128, 1), 0)
    lane_side = (lane < HEAD_DIM, lane >= HEAD_DIM)
    row_side = (row < HEAD_DIM, row >= HEAD_DIM)

    for r, ref in enumerate((qn_ref, qr_ref)):
        for side in range(2):
            for j in range(HPG):
                col = ref[0, :, j * 128:(j + 1) * 128]
                col = jnp.where(lane_side[side], col, jnp.zeros_like(col))
                qz_ref[2 * r + side, j * tq:(j + 1) * tq, :] = col * SCALE

    def v_sides(v):
        vt = v.astype(F32).T
        return [jnp.where(row_side[s], vt, 1.0).astype(BF16) for s in range(2)]

    kc = kc_ref[0]
    vct = vc_ref[0].astype(F32).T.astype(BF16)
    vis = (CMP_STRIDE * row + (CMP_LEN - 1) <= pos4) & (row < n_cmp)
    br = lax.broadcasted_iota(jnp.int32, (128, 128), 0) * SLC_BLK
    cc = lax.broadcasted_iota(jnp.int32, (128, 128), 1) * CMP_STRIDE
    overlap_t = ((cc < br + SLC_BLK) & (cc + CMP_LEN > br)).astype(F32)
    cur = pos // SLC_BLK
    forced = (row == 0) | (row == cur) | (row == cur - 1)
    for side in range(2):
        s = jnp.where(vis, _dot_nt(kc, qz_ref[side]), NEG)
        e = jnp.exp(s - jnp.max(s, axis=0, keepdims=True))
        p = jnp.where(vis, e / jnp.sum(e, axis=0, keepdims=True), 0.0)
        ocmp_ref[side] = _dot(vct, p.astype(BF16))
        psum = p[:, 0:tq]
        for j in range(1, HPG):
            psum = psum + p[:, j * tq:(j + 1) * tq]
        ps = _dot(overlap_t, psum, precision=HI)
        ps = jnp.where(row <= cur, jnp.where(forced, FORCE, ps), NEG)
        sel = _select_blocks_t(ps[0:n_blk, :], n_blk)
        sel_ref[side] = jnp.concatenate([sel, jnp.zeros((128 - n_blk, tq), F32)], axis=0).astype(BF16)

    m_ref[...] = jnp.full(m_ref.shape, NEG, F32)
    acc_ref[...] = jnp.zeros(acc_ref.shape, F32)

    def update(idx, s, vts):
        m_old = m_ref[idx]
        m_new = jnp.maximum(m_old, jnp.max(s, axis=0, keepdims=True))
        p = jnp.exp(s - m_new).astype(BF16)
        acc_ref[idx] = jnp.exp(m_old - m_new) * acc_ref[idx] + _dot(vts, p)
        m_ref[idx] = m_new

    def slc_tile(t, diagonal):
        k0 = pl.multiple_of(t * tq, tq)
        k = ks_ref[0, pl.ds(k0, tq), :]
        vts = v_sides(vs_ref[0, pl.ds(k0, tq), :])
        kpos = k0 + lax.broadcasted_iota(jnp.int32, (tq, 1), 0)
        expand_t = (lax.broadcasted_iota(jnp.int32, (tq, 128), 1) == kpos // SLC_BLK).astype(BF16)
        for side in range(2):
            keep = _dot(expand_t, sel_ref[side])
            if diagonal:
                keep = jnp.where(kpos <= pos, keep, 0.0)
            s = jnp.where(rep(keep) > 0.5, _dot_nt(k, qz_ref[2 + side]), NEG)
            update(side, s, vts[side])

    def slc_body(t, carry):
        slc_tile(t, False)
        return carry

    lax.fori_loop(0, qi, slc_body, 0)
    slc_tile(qi, True)

    def win_tile(t, kind):
        k0 = pl.multiple_of(t * tq, tq)
        k = kw_ref[0, pl.ds(k0, tq), :]
        vts = v_sides(vw_ref[0, pl.ds(k0, tq), :])
        kpos = k0 + lax.broadcasted_iota(jnp.int32, (tq, 1), 0)
        mask = (kpos > pos4 - WINDOW) if kind == 0 else ((kpos <= pos4) if kind == 2 else None)
        for side in range(2):
            s = _dot_nt(k, qz_ref[2 + side])
            if mask is not None:
                s = jnp.where(mask, s, NEG)
            update(2 + side, s, vts[side])

    @pl.when(qi >= 2)
    def _():
        win_tile(qi - 2, 0)

    @pl.when(qi >= 1)
    def _():
        win_tile(qi - 1, 1)

    win_tile(qi, 2)

    gates_t = _sigmoid(gd_ref[0]).T
    for j in range(HPG):
        sl = slice(j * tq, (j + 1) * tq)
        cols = []
        for side in range(2):
            lrow = HEAD_DIM if side == 0 else 0
            gi = (2 * gp + side) * (3 * HPG) + 3 * j
            g = [jnp.sum(jnp.where(row == gi + k, gates_t, 0.0), axis=0, keepdims=True)
                 for k in range(3)]
            a_s = acc_ref[side, :, sl]
            a_w = acc_ref[2 + side, :, sl]
            cols.append(g[0] * ocmp_ref[side, :, sl] + (g[1] / a_s[lrow:lrow + 1, :]) * a_s
                        + (g[2] / a_w[lrow:lrow + 1, :]) * a_w)
        o_ref[0, :, j * 128:(j + 1) * 128] = jnp.where(row_side[0], cols[0], cols[1]).T.astype(BF16)


def _attn_prompt(qn, qr, kc, vc, kvb, gd):
    b, t, _ = qn.shape
    tq = ATT_TQ
    assert WINDOW == 2 * tq and t % tq == 0 and t // SLC_BLK <= 128 and kc.shape[1] == 128
    tw = HPG * tq
    qspec = pl.BlockSpec((1, tq, 512), lambda i, g, q: (i, q, g))
    cspec = pl.BlockSpec((1, kc.shape[1], 128), lambda i, g, q: (i, 0, g))
    kvspec = lambda base: pl.BlockSpec((1, t, 128), lambda i, g, q: (i, 0, base + g))
    return pl.pallas_call(
        _attn_prompt_kernel,
        out_shape=jax.ShapeDtypeStruct((b, t, D_ATT), BF16),
        grid=(b, 2, t // tq),
        in_specs=[qspec, qspec, cspec, cspec, kvspec(0), kvspec(2), kvspec(4), kvspec(6),
                  pl.BlockSpec((1, tq, 128), lambda i, g, q: (i, q, 0))],
        out_specs=qspec,
        scratch_shapes=[pltpu.VMEM((4, tw, 128), BF16), pltpu.VMEM((2, 128, tq), BF16),
                        pltpu.VMEM((2, 128, tw), F32), pltpu.VMEM((4, 1, tw), F32),
                        pltpu.VMEM((4, 128, tw), F32)],
        compiler_params=_cparams(("parallel", "parallel", "arbitrary")),
        name="attn_prompt",
    )(qn, qr, kc, vc, kvb, kvb, kvb, kvb, gd)


ATS_PAGES = 16
N_ROWS_S = 128


def _attn_sample_kernel(pt_ref, *refs, n_steps, past, t_dec):
    npg = ATS_PAGES
    kpages = refs[0:2 * npg:2]
    vpages = refs[1:2 * npg:2]
    (qn_ref, qr_ref, kc_ref, vc_ref, kvn_ref, cwin_ref, wnew_ref, gt_ref,
     o_ref, sel_ref, ocmp_ref, m_ref, l_ref, acc_ref) = refs[2 * npg:]
    s_id = pl.program_id(1)
    nr = N_ROWS_S
    rows = lax.broadcasted_iota(jnp.int32, (nr, 1), 0)
    t_row = (rows % (t_dec * HPG)) // HPG
    pos = past + t_row
    qr = qr_ref[0]

    @pl.when(s_id == 0)
    def _():
        n_c = kc_ref.shape[1]
        n_blk = (past + t_dec + SLC_BLK - 1) // SLC_BLK
        bps = ATS_PAGES * PAGE // SLC_BLK
        nbl = -(-(n_steps * bps + 128) // 128) * 128
        cl = lax.broadcasted_iota(jnp.int32, (1, n_c), 1)
        vis = (CMP_STRIDE * cl + (CMP_LEN - 1) <= pos) & (cl < n_c - 1)
        s = jnp.where(vis, _dot_nt(qn_ref[0], kc_ref[0]) * SCALE, NEG)
        e = jnp.exp(s - jnp.max(s, axis=1, keepdims=True))
        p = jnp.where(vis, e / jnp.sum(e, axis=1, keepdims=True), 0.0)
        ocmp_ref[...] = _dot(p.astype(BF16), vc_ref[0])
        ng = nr // HPG
        gsum = (lax.broadcasted_iota(jnp.int32, (ng, nr), 1) // HPG
                == lax.broadcasted_iota(jnp.int32, (ng, nr), 0)).astype(F32)
        cr = lax.broadcasted_iota(jnp.int32, (n_c, nbl), 0) * CMP_STRIDE
        j0 = lax.broadcasted_iota(jnp.int32, (n_c, nbl), 1) * SLC_BLK
        overlap = ((cr < j0 + SLC_BLK) & (cr + CMP_LEN > j0)).astype(F32)
        ps = _dot(_dot(gsum, p, precision=HI), overlap, precision=HI)
        bl = lax.broadcasted_iota(jnp.int32, (1, nbl), 1)
        g_rows = lax.broadcasted_iota(jnp.int32, (ng, 1), 0)
        cur = (past + g_rows % t_dec) // SLC_BLK
        forced = (bl == 0) | (bl == cur) | (bl == cur - 1)
        ps = jnp.where(bl <= cur, jnp.where(forced, FORCE, ps), NEG)
        rank = jnp.zeros(ps.shape, F32)
        for i in range(n_blk):
            vi = ps[:, i:i + 1]
            rank = rank + ((vi > ps) | ((vi == ps) & (bl > i))).astype(F32)
        sel = jnp.where((rank < N_SELECT) & (bl < n_blk), 1.0, 0.0)
        gexp = (lax.broadcasted_iota(jnp.int32, (nr, ng), 0) // HPG
                == lax.broadcasted_iota(jnp.int32, (nr, ng), 1)).astype(F32)
        sel_rows = _dot(gexp, sel)
        for w in range(sel_ref.shape[0]):
            sel_ref[w] = sel_rows[:, bps * w:bps * w + 128].astype(BF16)
        m_ref[...] = jnp.full(m_ref.shape, NEG, F32)
        l_ref[...] = jnp.zeros(l_ref.shape, F32)
        acc_ref[...] = jnp.zeros(acc_ref.shape, F32)

    def update(s, mask, v, v_transposed=False):
        s = jnp.where(mask, s, NEG)
        m_old = m_ref[...]
        m_new = jnp.maximum(m_old, jnp.max(s, axis=1, keepdims=True))
        p = jnp.where(mask, jnp.exp(s - m_new), 0.0)
        alpha = jnp.exp(m_old - m_new)
        l_ref[...] = alpha * l_ref[...] + jnp.sum(p, axis=1, keepdims=True)
        pv = _dot_nt(p.astype(BF16), v) if v_transposed else _dot(p.astype(BF16), v)
        acc_ref[...] = alpha * acc_ref[...] + pv
        m_ref[...] = m_new

    nk = npg * PAGE
    k_t = jnp.concatenate([r[0] for r in kpages], axis=1).astype(BF16)
    v_t = jnp.concatenate([r[0] for r in vpages], axis=1).astype(BF16)
    expand = (lax.broadcasted_iota(jnp.int32, (128, nk), 0)
              == lax.broadcasted_iota(jnp.int32, (128, nk), 1) // SLC_BLK).astype(BF16)
    kpos = s_id * nk + lax.broadcasted_iota(jnp.int32, (1, nk), 1)
    mask = (_dot(sel_ref[s_id], expand) > 0.5) & (kpos <= pos)
    update(_dot(qr, k_t) * SCALE, mask, v_t, v_transposed=True)

    @pl.when(s_id == n_steps - 1)
    def _():
        kn = kvn_ref[0, :, 512:768].astype(BF16)
        vn = kvn_ref[0, :, 768:1024].astype(BF16)
        tk = lax.broadcasted_iota(jnp.int32, (1, t_dec), 1)
        cur_sel = sel_ref[n_steps][:, 0:1].astype(F32) > 0.5
        update(_dot_nt(qr, kn) * SCALE, cur_sel & (past + tk <= pos), vn)
        o_slc = acc_ref[...] / l_ref[...]
        wb = cwin_ref.shape[1]
        kw = cwin_ref[0, :, 0:256].astype(BF16)
        vw = cwin_ref[0, :, 256:512].astype(BF16)
        kwn = wnew_ref[0, :, 0:256].astype(BF16)
        vwn = wnew_ref[0, :, 256:512].astype(BF16)
        d1 = pos - (past - wb + lax.broadcasted_iota(jnp.int32, (1, wb), 1))
        d2 = pos - (past + tk)
        m1 = (d1 >= 0) & (d1 < WINDOW)
        m2 = (d2 >= 0) & (d2 < WINDOW)
        s1 = jnp.where(m1, _dot_nt(qr, kw) * SCALE, NEG)
        s2 = jnp.where(m2, _dot_nt(qr, kwn) * SCALE, NEG)
        mx = jnp.maximum(jnp.max(s1, axis=1, keepdims=True), jnp.max(s2, axis=1, keepdims=True))
        p1 = jnp.where(m1, jnp.exp(s1 - mx), 0.0)
        p2 = jnp.where(m2, jnp.exp(s2 - mx), 0.0)
        den = jnp.sum(p1, axis=1, keepdims=True) + jnp.sum(p2, axis=1, keepdims=True)
        o_win = (_dot(p1.astype(BF16), vw) + _dot(p2.astype(BF16), vwn)) / den
        g = _sigmoid(gt_ref[0])
        o_ref[0] = g[:, 0:1] * ocmp_ref[...] + g[:, 1:2] * o_slc + g[:, 2:3] * o_win


def _attn_sample(cache_t, page_table, qn_x, qr_x, kc, vc, kv_new, cache_win2, win_new, gt_rows, t_dec):
    b, n_pages = page_table.shape
    past = n_pages * PAGE
    n_steps = n_pages // ATS_PAGES
    nr = N_ROWS_S

    def pspec(k, blk):
        return pl.BlockSpec((1, 256, PAGE), lambda i, s, pt: (pt[i, s * ATS_PAGES + k], blk, 0))

    per_b = lambda r, w: pl.BlockSpec((1, r, w), lambda i, s, pt: (i, 0, 0))
    page_specs = []
    for k in range(ATS_PAGES):
        page_specs += [pspec(k, 2), pspec(k, 3)]
    gs = pltpu.PrefetchScalarGridSpec(
        num_scalar_prefetch=1,
        grid=(b, n_steps),
        in_specs=page_specs + [per_b(nr, 256), per_b(nr, 256), per_b(kc.shape[1], 256),
                               per_b(kc.shape[1], 256), per_b(t_dec, 1024),
                               per_b(cache_win2.shape[1], 512), per_b(t_dec, 512), per_b(nr, 128)],
        out_specs=per_b(nr, 256),
        scratch_shapes=[pltpu.VMEM((n_steps + 1, nr, 128), BF16), pltpu.VMEM((nr, 256), F32),
                        pltpu.VMEM((nr, 1), F32), pltpu.VMEM((nr, 1), F32),
                        pltpu.VMEM((nr, 256), F32)],
    )
    return pl.pallas_call(
        functools.partial(_attn_sample_kernel, n_steps=n_steps, past=past, t_dec=t_dec),
        out_shape=jax.ShapeDtypeStruct((b, nr, 256), F32),
        grid_spec=gs,
        compiler_params=_cparams(("parallel", "arbitrary")),
        name="attn_sample",
    )(page_table, *([cache_t] * (2 * ATS_PAGES)), qn_x, qr_x, kc, vc, kv_new, cache_win2, win_new,
      gt_rows)


def _expand_rows(q):
    b, t, _ = q.shape
    q5 = q.reshape(b, t, N_KV, HPG, HEAD_DIM)
    eye = jnp.eye(N_KV, dtype=q.dtype)
    return jnp.einsum('btghd,gk->bgthkd', q5, eye).reshape(b, N_KV * t * HPG, N_KV * HEAD_DIM)


def _collapse_rows(o, t):
    b = o.shape[0]
    o6 = o.reshape(b, N_KV, t, HPG, N_KV, HEAD_DIM)
    return jnp.einsum('bgthkd,gk->btghd', o6, jnp.eye(N_KV, dtype=o.dtype)).reshape(b, t, D_ATT)


def _layernorm(x, g, b):
    mu = jnp.mean(x, axis=-1, keepdims=True)
    xc = x - mu
    var = jnp.mean(xc * xc, axis=-1, keepdims=True)
    return xc * lax.rsqrt(var + LN_EPS) * g + b


OUT_SUB = 128


def _outproj_kernel(oa_ref, ys_ref, x_ref, g1_ref, sc2_ref, sh2_ref, w_ref, lg_ref, lb_ref,
                    rw_ref, rb_ref, x1_ref, h2_ref, ti_ref, tw_ref):
    tm = x_ref.shape[1]
    sub = min(OUT_SUB, tm)
    for r0 in range(0, tm, sub):
        rs = slice(r0, r0 + sub)
        mix = _dot(oa_ref[0, rs, :], w_ref[0:D_ATT, :]) + _dot(ys_ref[0, rs, :], w_ref[D_ATT:, :])
        x1 = _layernorm(DN_ALPHA * x_ref[0, rs, :] + g1_ref[0] * mix, lg_ref[...], lb_ref[...])
        x1_ref[0, rs, :] = x1
        h2 = x1 * (1.0 + sc2_ref[0]) + sh2_ref[0]
        h_hi = h2.astype(BF16)
        h2_ref[0, rs, :] = h_hi
        h_lo = (h2 - h_hi.astype(F32)).astype(BF16)
        logits = (_dot(h_hi, rw_ref[0]) + (_dot(h_lo, rw_ref[0]) + _dot(h_hi, rw_ref[1]))
                  + rb_ref[...])
        lane = lax.broadcasted_iota(jnp.int32, logits.shape, 1)
        vals, ids = [], []
        for _ in range(TOP_K):
            mx = jnp.max(logits, axis=1, keepdims=True)
            ix = jnp.min(jnp.where(logits == mx, lane, 128), axis=1, keepdims=True)
            vals.append(mx)
            ids.append(ix)
            logits = jnp.where(lane == ix, -jnp.inf, logits)
        es = [jnp.exp(v - vals[0]) for v in vals]
        inv = 1.0 / (es[0] + es[1] + es[2] + es[3])
        ti = jnp.zeros(lane.shape, jnp.int32)
        tw = jnp.zeros(lane.shape, F32)
        for k in range(TOP_K):
            ti = jnp.where(lane == k, ids[k], ti)
            tw = jnp.where(lane == k, es[k] * inv, tw)
        ti_ref[0, rs, :] = ti
        tw_ref[0, rs, :] = tw


def _out_proj(o_att, y_ssm, x, g1, sc2, sh2, w_out_bf, ln_g, ln_b, rw, rb, tm):
    b, t, _ = x.shape
    row = lambda w: pl.BlockSpec((1, tm, w), lambda i, j: (i, j, 0))
    per_b = pl.BlockSpec((1, 1, D_MODEL), lambda i, j: (i, 0, 0))
    par = lambda r, w, **kw: pl.BlockSpec((r, w), lambda i, j: (0, 0), **kw)
    return pl.pallas_call(
        _outproj_kernel,
        out_shape=[jax.ShapeDtypeStruct((b, t, D_MODEL), F32),
                   jax.ShapeDtypeStruct((b, t, D_MODEL), BF16),
                   jax.ShapeDtypeStruct((b, t, 128), jnp.int32),
                   jax.ShapeDtypeStruct((b, t, 128), F32)],
        grid=(b, t // tm),
        in_specs=[row(D_ATT), row(D_SSM), row(D_MODEL), per_b, per_b, per_b,
                  par(D_MODEL, D_MODEL, pipeline_mode=pl.Buffered(1)),
                  par(1, D_MODEL), par(1, D_MODEL),
                  pl.BlockSpec((2, D_MODEL, 128), lambda i, j: (0, 0, 0)), par(1, 128)],
        out_specs=[row(D_MODEL), row(D_MODEL), row(128), row(128)],
        compiler_params=_cparams(("parallel", "parallel")),
        name="out_proj",
    )(o_att, y_ssm, x, g1, sc2, sh2, w_out_bf, ln_g, ln_b, rw, rb)


MOE_TM = 1024
MOE_TN = 512
ROUTE_TM = 256


def _route_kernel(ti_ref, rank_ref, cnt_ref):
    @pl.when(pl.program_id(0) == 0)
    def _():
        cnt_ref[...] = jnp.zeros_like(cnt_ref)

    ti = ti_ref[...]
    tm = ti.shape[0]
    lane = lax.broadcasted_iota(jnp.int32, (tm, 128), 1)
    hits = [lane == ti[:, k:k + 1] for k in range(TOP_K)]
    oh = jnp.zeros((tm, 128), F32)
    for h in hits:
        oh = oh + h.astype(F32)
    ri = lax.broadcasted_iota(jnp.int32, (tm, tm), 0)
    ci = lax.broadcasted_iota(jnp.int32, (tm, tm), 1)
    before = _dot((ri > ci).astype(BF16), oh.astype(BF16)) + cnt_ref[0:1, :]
    rank = jnp.zeros((tm, 128), jnp.int32)
    for k, h in enumerate(hits):
        rk = jnp.sum(jnp.where(h, before, 0.0), axis=1, keepdims=True).astype(jnp.int32)
        rank = jnp.where(lane == k, rk, rank)
    rank_ref[...] = rank
    cnt_ref[...] = cnt_ref[...] + jnp.sum(oh, axis=0, keepdims=True)


def _route(ti_all):
    n = ti_all.shape[0]
    return pl.pallas_call(
        _route_kernel,
        out_shape=[jax.ShapeDtypeStruct((n, 128), jnp.int32),
                   jax.ShapeDtypeStruct((8, 128), F32)],
        grid=(n // ROUTE_TM,),
        in_specs=[pl.BlockSpec((ROUTE_TM, 128), lambda i: (i, 0))],
        out_specs=[pl.BlockSpec((ROUTE_TM, 128), lambda i: (i, 0)),
                   pl.BlockSpec((8, 128), lambda i: (0, 0))],
        compiler_params=_cparams(("arbitrary",)),
        name="moe_route",
    )(ti_all)


DMA_UNROLL = 8


def _dispatch_kernel(pos_ref, h_ref, xs_in_ref, xs_ref, sem):
    del xs_in_ref
    def copy(t, k):
        return pltpu.make_async_copy(h_ref.at[t], xs_ref.at[pos_ref[0, 0, t * TOP_K + k]], sem)

    def start(t, c):
        for k in range(TOP_K):
            copy(t, k).start()
        return c

    def wait(t, c):
        for k in range(TOP_K):
            copy(t, k).wait()
        return c

    lax.fori_loop(0, ROUTE_TM, start, 0, unroll=DMA_UNROLL // TOP_K)
    lax.fori_loop(0, ROUTE_TM, wait, 0, unroll=DMA_UNROLL // TOP_K)


def _dispatch(pos, h_all, n_rows):
    n = h_all.shape[0]
    steps = n // ROUTE_TM
    s = D_MODEL // 128
    xs0 = jnp.zeros((n_rows, s, 128), BF16)
    out = pl.pallas_call(
        _dispatch_kernel,
        out_shape=jax.ShapeDtypeStruct((n_rows, s, 128), BF16),
        grid=(steps,),
        in_specs=[pl.BlockSpec((1, 1, ROUTE_TM * TOP_K), lambda i: (i, 0, 0), memory_space=pltpu.SMEM),
                  pl.BlockSpec((ROUTE_TM, s, 128), lambda i: (i, 0, 0)),
                  pl.BlockSpec(memory_space=pl.ANY)],
        out_specs=pl.BlockSpec(memory_space=pl.ANY),
        scratch_shapes=[pltpu.SemaphoreType.DMA(())],
        input_output_aliases={2: 0},
        compiler_params=_cparams(("arbitrary",), no_bounds_checks=True),
        name="moe_dispatch",
    )(pos.reshape(steps, 1, ROUTE_TM * TOP_K), h_all.reshape(n, s, 128), xs0)
    return out.reshape(n_rows, D_MODEL)


def _moe_up_kernel(te_ref, nt_ref, xs_ref, wg_ref, wu_ref, bg_ref, bu_ref, act_ref):
    @pl.when(pl.program_id(0) < nt_ref[0])
    def _():
        x = xs_ref[...]
        g = _dot(x, wg_ref[0].astype(BF16)) + bg_ref[0]
        u = _dot(x, wu_ref[0].astype(BF16)) + bu_ref[0]
        g = jnp.minimum(g, SWIGLU_LIMIT)
        u = jnp.clip(u, -SWIGLU_LIMIT, SWIGLU_LIMIT)
        act_ref[...] = ((u + 1.0) * g * _sigmoid(SWIGLU_ALPHA * g)).astype(BF16)

    @pl.when(pl.program_id(0) >= nt_ref[0])
    def _():
        act_ref[...] = jnp.zeros_like(act_ref)


def _moe_up(tile_e, n_used, xs, w_gu, b_gu3):
    n_rows = xs.shape[0]
    nj = D_FF // MOE_TN
    row_i = lambda i, j, te, nt: (jnp.minimum(i, nt[0] - 1), 0)
    jc = lambda i, j, nt: jnp.where(i < nt[0], j, nj - 1)
    gs = pltpu.PrefetchScalarGridSpec(
        num_scalar_prefetch=2,
        grid=(n_rows // MOE_TM, nj),
        in_specs=[pl.BlockSpec((MOE_TM, D_MODEL), row_i),
                  pl.BlockSpec((1, D_MODEL, MOE_TN), lambda i, j, te, nt: (te[i], 0, jc(i, j, nt))),
                  pl.BlockSpec((1, D_MODEL, MOE_TN), lambda i, j, te, nt: (te[i], 0, jc(i, j, nt) + nj)),
                  pl.BlockSpec((1, 1, MOE_TN), lambda i, j, te, nt: (te[i], 0, jc(i, j, nt))),
                  pl.BlockSpec((1, 1, MOE_TN), lambda i, j, te, nt: (te[i], 0, jc(i, j, nt) + nj))],
        out_specs=pl.BlockSpec((MOE_TM, MOE_TN), lambda i, j, te, nt: (i, j)),
    )
    return pl.pallas_call(
        _moe_up_kernel,
        out_shape=jax.ShapeDtypeStruct((n_rows, D_FF), BF16),
        grid_spec=gs,
        compiler_params=_cparams(("parallel", "arbitrary")),
        name="moe_up",
    )(tile_e, n_used, xs, w_gu, w_gu, b_gu3, b_gu3)


def _moe_down_kernel(te_ref, nt_ref, a_ref, w_ref, b_ref, y_ref):
    @pl.when(pl.program_id(0) < nt_ref[0])
    def _():
        y_ref[...] = _dot(a_ref[...], w_ref[0].astype(BF16)) + b_ref[0]

    @pl.when(pl.program_id(0) >= nt_ref[0])
    def _():
        y_ref[...] = jnp.zeros_like(y_ref)


def _moe_down(tile_e, n_used, act, w_down, b_down3):
    n_rows = act.shape[0]
    nj = D_MODEL // MOE_TN
    row_i = lambda i, j, te, nt: (jnp.minimum(i, nt[0] - 1), 0)
    jc = lambda i, j, nt: jnp.where(i < nt[0], j, nj - 1)
    gs = pltpu.PrefetchScalarGridSpec(
        num_scalar_prefetch=2,
        grid=(n_rows // MOE_TM, nj),
        in_specs=[pl.BlockSpec((MOE_TM, D_FF), row_i),
                  pl.BlockSpec((1, D_FF, MOE_TN), lambda i, j, te, nt: (te[i], 0, jc(i, j, nt))),
                  pl.BlockSpec((1, 1, MOE_TN), lambda i, j, te, nt: (te[i], 0, jc(i, j, nt)))],
        out_specs=pl.BlockSpec((MOE_TM, MOE_TN), lambda i, j, te, nt: (i, j)),
    )
    return pl.pallas_call(
        _moe_down_kernel,
        out_shape=jax.ShapeDtypeStruct((n_rows, D_MODEL), F32),
        grid_spec=gs,
        compiler_params=_cparams(("parallel", "arbitrary")),
        name="moe_down",
    )(tile_e, n_used, act, w_down, b_down3)


def _combine_kernel(pos_ref, yd_ref, x1_ref, tw_ref, g2_ref, lg_ref, lb_ref, o_ref, buf, sem):
    tm = buf.shape[1]

    def copy(t, k):
        return pltpu.make_async_copy(yd_ref.at[pl.ds(pos_ref[0, 0, t * TOP_K + k], 1), :],
                                     buf.at[k, pl.ds(t, 1), :], sem)

    def start(t, c):
        for k in range(TOP_K):
            copy(t, k).start()
        return c

    def wait(t, c):
        for k in range(TOP_K):
            copy(t, k).wait()
        return c

    lax.fori_loop(0, tm, start, 0, unroll=DMA_UNROLL // TOP_K)
    lax.fori_loop(0, tm, wait, 0, unroll=DMA_UNROLL // TOP_K)
    tw = tw_ref[0]
    moe = buf[0] * tw[:, 0:1]
    for k in range(1, TOP_K):
        moe = moe + buf[k] * tw[:, k:k + 1]
    o_ref[0] = _layernorm(DN_ALPHA * x1_ref[0] + g2_ref[0] * moe, lg_ref[...], lb_ref[...])


def _combine(pos, yd, x1, tw, g2, ln_g, ln_b, tok0):
    b, t, _ = x1.shape
    tm = min(ROUTE_TM, t)
    per = t // tm
    step0 = tok0 // tm
    row = lambda w: pl.BlockSpec((1, tm, w), lambda i, j: (i, j, 0))
    par = pl.BlockSpec((1, D_MODEL), lambda i, j: (0, 0))
    return pl.pallas_call(
        functools.partial(_combine_kernel),
        out_shape=jax.ShapeDtypeStruct((b, t, D_MODEL), F32),
        grid=(b, per),
        in_specs=[pl.BlockSpec((1, 1, tm * TOP_K), lambda i, j: (step0 + i * per + j, 0, 0),
                               memory_space=pltpu.SMEM),
                  pl.BlockSpec(memory_space=pl.ANY),
                  row(D_MODEL), row(128),
                  pl.BlockSpec((1, 1, D_MODEL), lambda i, j: (i, 0, 0)), par, par],
        out_specs=row(D_MODEL),
        scratch_shapes=[pltpu.VMEM((TOP_K, tm, D_MODEL), F32), pltpu.SemaphoreType.DMA(())],
        compiler_params=_cparams(("arbitrary", "arbitrary"), no_bounds_checks=True),
        name="moe_combine",
    )(pos.reshape(-1, 1, tm * TOP_K), yd, x1, tw, g2, ln_g, ln_b)


IN_TM = 256
SSD_Q = 256
OUT_TM = 512


def kernel(x_prompt, x_sample, cache_kv, cache_win, state_conv, state_ssm, page_table,
           c_prompt, c_sample, w_ada, b_ada, w_in, cmp_pe, cmp_w1, cmp_b1, cmp_w2,
           conv_w, conv_b, dt_bias, a_log, d_skip, ssm_norm_g, w_out, ln1_g, ln1_b,
           router_w, router_b, w_gu, b_gu, w_down, b_down, ln2_g, ln2_b):
    bp, tp, _ = x_prompt.shape
    bs, ts, _ = x_sample.shape
    past = page_table.shape[1] * PAGE

    m = _ada(jnp.concatenate([c_prompt, c_sample], axis=0), w_ada[0], b_ada[0])
    mod = [m[:, None, k * D_MODEL:(k + 1) * D_MODEL] for k in range(6)]
    mod_p = [a[:bp] for a in mod]
    mod_s = [a[bp:] for a in mod]

    perm = _q_perm()
    inv_perm = np.argsort(perm)
    w_bf = _prep_w_in(w_in[0])
    cw = _prep_cmp_weights(cmp_pe[0], cmp_w1[0], cmp_b1[0], cmp_w2[0])
    sp = _prep_ssm_params(conv_w[0], conv_b[0], dt_bias[0], a_log[0], d_skip[0], ssm_norm_g[0])
    w_out_bf = jnp.concatenate([w_out[0][:D_ATT][perm], w_out[0][D_ATT:]], axis=0).astype(BF16)
    rw = jnp.concatenate([router_w[0], jnp.zeros((D_MODEL, 128 - N_EXPERTS), F32)], axis=1)
    rw_hi = rw.astype(BF16)
    rw = jnp.stack([rw_hi, (rw - rw_hi.astype(F32)).astype(BF16)])
    rb = jnp.concatenate([router_b[0], jnp.full((128 - N_EXPERTS,), NEG, F32)]).reshape(1, 128)
    ln1 = (ln1_g[0].reshape(1, D_MODEL), ln1_b[0].reshape(1, D_MODEL))
    ln2 = (ln2_g[0].reshape(1, D_MODEL), ln2_b[0].reshape(1, D_MODEL))

    tab_p = _rope_tables(jnp.arange(tp))
    qn, qr, kv_p, win_p, kvb, z, xbc, gd = _in_proj(x_prompt, mod_p[1], mod_p[0], w_bf, tab_p, IN_TM)
    kc, vc = _compress_prompt(kv_p, cw)
    o_att = _attn_prompt(qn, qr, kc, vc, kvb, gd)
    y_ssm, ssm_p = _ssd(xbc, z, gd, jnp.zeros((bp, 8, CONV_CH), F32),
                        jnp.zeros((bp, D_SSM, D_STATE), F32), sp, SSD_Q)
    x1_p, h2_p, ti_p, tw_p = _out_proj(o_att, y_ssm, x_prompt, mod_p[2], mod_p[4], mod_p[3],
                                       w_out_bf, *ln1, rw, rb, OUT_TM)

    tab_s = _rope_tables(past + jnp.arange(ts))
    qn_s, qr_s, kv_s, win_s, _, z_s, xbc_s, gd_s = _in_proj(
        x_sample, mod_s[1], mod_s[0], w_bf, tab_s, ts)
    cache_t = jnp.transpose(cache_kv[0], (0, 2, 3, 4, 1)).reshape(
        cache_kv.shape[1], 4 * N_KV * HEAD_DIM, PAGE)
    kc_s, vc_s = _compress_sample(cache_t, page_table, cw)
    wb = cache_win.shape[2]
    cwin2 = cache_win[0].reshape(bs, wb, 2 * N_KV * HEAD_DIM)
    gt_rows = gd_s[:, :, :D_GT].reshape(bs, ts, N_KV, HPG, 3).transpose(0, 2, 1, 3, 4)
    gt_rows = jnp.concatenate([gt_rows.reshape(bs, N_ROWS_S, 3),
                               jnp.zeros((bs, N_ROWS_S, 125), F32)], axis=-1)
    o_rows = _attn_sample(cache_t, page_table, _expand_rows(qn_s[:, :, inv_perm]),
                          _expand_rows(qr_s[:, :, inv_perm]), kc_s, vc_s, kv_s, cwin2, win_s,
                          gt_rows, ts)
    o_att_s = _collapse_rows(o_rows, ts)[:, :, perm].astype(BF16)
    cprev = jnp.concatenate([jnp.zeros((bs, 5, CONV_CH), F32), state_conv[0]], axis=1)
    y_ssm_s, ssm_s = _ssd(xbc_s, z_s, gd_s, cprev, state_ssm[0].reshape(bs, D_SSM, D_STATE), sp, ts)
    x1_s, h2_s, ti_s, tw_s = _out_proj(o_att_s, y_ssm_s, x_sample, mod_s[2], mod_s[4], mod_s[3],
                                       w_out_bf, *ln1, rw, rb, ts)

    n_tok = bp * tp + bs * ts
    n_tiles = -(-(n_tok * TOP_K + N_EXPERTS * (MOE_TM - 1)) // MOE_TM)
    ti_all = jnp.concatenate([ti_p.reshape(-1, 128), ti_s.reshape(-1, 128)], axis=0)
    rank, cnt = _route(ti_all)
    counts = cnt[0, :N_EXPERTS].astype(jnp.int32)
    padded = (counts + MOE_TM - 1) // MOE_TM * MOE_TM
    ends = jnp.cumsum(padded)
    offs = ends - padded
    n_used = (ends[-1] // MOE_TM).astype(jnp.int32).reshape(1)
    tiles = jnp.minimum(jnp.arange(n_tiles, dtype=jnp.int32), n_used[0] - 1)
    tile_e = jnp.sum((tiles[:, None] * MOE_TM >= ends[None, :]).astype(jnp.int32), axis=1)
    tile_e = jnp.minimum(tile_e, N_EXPERTS - 1).astype(jnp.int32)
    pos = (offs[ti_all[:, :TOP_K]] + rank[:, :TOP_K]).astype(jnp.int32)
    h2_all = jnp.concatenate([h2_p.reshape(-1, D_MODEL), h2_s.reshape(-1, D_MODEL)], axis=0)
    xs = _dispatch(pos, h2_all, n_tiles * MOE_TM)
    act = _moe_up(tile_e, n_used, xs, w_gu[0], b_gu[0].reshape(N_EXPERTS, 1, 2 * D_FF))
    yd = _moe_down(tile_e, n_used, act, w_down[0], b_down[0].reshape(N_EXPERTS, 1, D_MODEL))
    y_p = _combine(pos, yd, x1_p, tw_p, mod_p[5], *ln2, 0)
    y_s = _combine(pos, yd, x1_s, tw_s, mod_s[5], *ln2, bp * tp)

    kv_shape = (4, N_KV, HEAD_DIM)
    win_shape = (2, N_KV, HEAD_DIM)
    win_prompt = win_p[:, tp - min(WINDOW, tp):].reshape((1, bp, min(WINDOW, tp)) + win_shape)
    win_sample = jnp.concatenate([cwin2, win_s], axis=1)[:, -wb:].reshape((1, bs, wb) + win_shape)
    return (y_p, y_s,
            kv_p.reshape((1, bp, tp) + kv_shape), kv_s.reshape((1, bs, ts) + kv_shape),
            win_prompt, win_sample,
            xbc[None, :, tp - (CONV_W - 1):], xbc_s[None, :, ts - (CONV_W - 1):],
            ssm_p.reshape(1, bp, N_HEADS_SSM, HEAD_DIM, D_STATE),
            ssm_s.reshape(1, bs, N_HEADS_SSM, HEAD_DIM, D_STATE))
```

```python
import functools
import math

import jax
import jax.numpy as jnp
import numpy as np
from jax import lax
from jax.experimental import pallas as pl
from jax.experimental.pallas import tpu as pltpu

D_MODEL = 2048
D_ATT = 1024
D_SSM = 1024
HEAD_DIM = 64
N_HEADS_ATT = 16
N_KV = 4
HPG = 4
ROT_DIM = 16
ROPE_THETA = 500000.0
CMP_LEN = 32
CMP_STRIDE = 16
CMP_HID = 128
SLC_BLK = 64
N_SELECT = 16
WINDOW = 512
N_HEADS_SSM = 16
SSM_GROUPS = 4
D_STATE = 128
CONV_W = 4
CONV_CH = 2048
N_EXPERTS = 32
TOP_K = 4
D_FF = 2048
SWIGLU_LIMIT = 7.0
SWIGLU_ALPHA = 1.702
DN_ALPHA = 2.0 ** 0.25
LN_EPS = 1e-5
RMS_EPS = 1e-5
NEG = -1e30
FORCE = 1e6
PAGE = 128

D_KV = 6 * N_KV * HEAD_DIM
D_GT = 3 * N_HEADS_ATT
D_IN = D_ATT + D_KV + D_GT + D_SSM + CONV_CH + N_HEADS_SSM
D_INP = D_ATT + D_KV + D_SSM + CONV_CH + 128

LANES = 128
VMEM_LIMIT = 56 * 1024 * 1024

F32 = jnp.float32
BF16 = jnp.bfloat16
HI = lax.Precision.HIGHEST


def _cparams(sem, vmem=VMEM_LIMIT, no_bounds_checks=False):
    return pltpu.CompilerParams(dimension_semantics=sem, vmem_limit_bytes=vmem,
                                disable_bounds_checks=no_bounds_checks)


def _dot(a, b, precision=None):
    return jnp.dot(a, b, preferred_element_type=F32, precision=precision)


def _dot_nt(a, b, precision=None):
    return lax.dot_general(a, b, (((1,), (1,)), ((), ())), preferred_element_type=F32,
                           precision=precision)


def _dot_tn(a, b, precision=None):
    return lax.dot_general(a, b, (((0,), (0,)), ((), ())), preferred_element_type=F32,
                           precision=precision)


def _split3(x):
    x1 = x.astype(BF16)
    r = x - x1.astype(F32)
    x2 = r.astype(BF16)
    return x1, x2, (r - x2.astype(F32)).astype(BF16)


def _dot_sel_l(sel, x):
    s = sel.astype(BF16)
    x1, x2, x3 = _split3(x)
    return _dot(s, x1) + (_dot(s, x2) + _dot(s, x3))


def _dot_sel_r(x, sel):
    s = sel.astype(BF16)
    x1, x2, x3 = _split3(x)
    return _dot(x1, s) + (_dot(x2, s) + _dot(x3, s))


def _sigmoid(x):
    return 1.0 / (1.0 + jnp.exp(-x))


def _silu(x):
    return x * _sigmoid(x)


def _ada_kernel(c_ref, w_ref, b_ref, o_ref):
    c = c_ref[...]
    a = _silu(c).astype(BF16)
    o_ref[...] = _dot(a, w_ref[...].astype(BF16)) + b_ref[...]


def _ada(c_all, w_ada, b_ada):
    nb = c_all.shape[0]
    tn = 1024
    n = w_ada.shape[1]
    return pl.pallas_call(
        _ada_kernel,
        out_shape=jax.ShapeDtypeStruct((nb, n), F32),
        grid=(n // tn,),
        in_specs=[pl.BlockSpec((nb, D_MODEL), lambda j: (0, 0)),
                  pl.BlockSpec((D_MODEL, tn), lambda j: (0, j)),
                  pl.BlockSpec((1, tn), lambda j: (0, j))],
        out_specs=pl.BlockSpec((nb, tn), lambda j: (0, j)),
        compiler_params=_cparams(("parallel",)),
        name="ada",
    )(c_all, w_ada, b_ada.reshape(1, n))


def _q_perm():
    idx = np.zeros(D_ATT, np.int32)
    for gp in range(2):
        for j in range(HPG):
            for side in range(2):
                g = 2 * gp + side
                for d in range(HEAD_DIM):
                    idx[gp * 512 + j * 128 + side * 64 + d] = g * 256 + j * 64 + d
    return idx


def _prep_w_in(w_in):
    o_kv = D_ATT
    o_gt = o_kv + D_KV
    o_z = o_gt + D_GT
    o_xbc = o_z + D_SSM
    o_dt = o_xbc + CONV_CH
    wq = w_in[:, :D_ATT][:, _q_perm()]
    pad = jnp.zeros((D_MODEL, 128 - D_GT - N_HEADS_SSM), w_in.dtype)
    w = jnp.concatenate([wq, w_in[:, o_kv:o_gt], w_in[:, o_z:o_xbc], w_in[:, o_xbc:o_dt],
                         w_in[:, o_gt:o_z], w_in[:, o_dt:], pad], axis=1)
    return w.astype(BF16)


def _rope_tables(pos):
    half = ROT_DIM // 2
    inv = ROPE_THETA ** (-jnp.arange(half, dtype=F32) / half)
    ang = pos.astype(F32)[:, None] * inv
    cos, sin = jnp.cos(ang), jnp.sin(ang)
    t = pos.shape[0]
    one = jnp.ones((t, HEAD_DIM - ROT_DIM), F32)
    zero = jnp.zeros((t, HEAD_DIM - ROT_DIM), F32)
    z8 = jnp.zeros((t, half), F32)
    c64 = jnp.concatenate([cos, cos, one], 1)
    s1 = jnp.concatenate([z8, sin, zero], 1)
    s2 = jnp.concatenate([-sin, z8, zero], 1)
    rep = lambda a: jnp.tile(a, (1, 4))
    return jnp.concatenate([rep(c64), rep(s1), rep(s2)], axis=1)


def _rope256(x, tab):
    n = x.shape[1]
    return (x * tab[:, 0:256] + pltpu.roll(x, 8, 1) * tab[:, 256:512]
            + pltpu.roll(x, n - 8, 1) * tab[:, 512:768])


def _inproj_kernel(x_ref, sc_ref, sh_ref, w_ref, tab_ref,
                   qn_ref, qr_ref, kv_ref, win_ref, kvb_ref, z_ref, xbc_ref, gd_ref):
    h = (x_ref[0] * (1.0 + sc_ref[0]) + sh_ref[0]).astype(BF16)
    tab = tab_ref[...]
    for c in range(4):
        q = _dot(h, w_ref[:, c * 256:(c + 1) * 256])
        qn_ref[0, :, c * 256:(c + 1) * 256] = q.astype(BF16)
        qr_ref[0, :, c * 256:(c + 1) * 256] = _rope256(q, tab).astype(BF16)
    o = D_ATT
    for p in range(6):
        y = _dot(h, w_ref[:, o + p * 256:o + (p + 1) * 256])
        if p in (2, 4):
            y = _rope256(y, tab)
        if p < 4:
            kv_ref[0, :, p * 256:(p + 1) * 256] = y
        else:
            win_ref[0, :, (p - 4) * 256:(p - 3) * 256] = y
        if p >= 2:
            kvb_ref[0, :, (p - 2) * 256:(p - 1) * 256] = y.astype(BF16)
    o += D_KV
    for c in range(2):
        z_ref[0, :, c * 512:(c + 1) * 512] = _dot(h, w_ref[:, o + c * 512:o + (c + 1) * 512]).astype(BF16)
    o += D_SSM
    for c in range(4):
        xbc_ref[0, :, c * 512:(c + 1) * 512] = _dot(h, w_ref[:, o + c * 512:o + (c + 1) * 512])
    o += CONV_CH
    gd_ref[0] = _dot(h, w_ref[:, o:o + 128])


def _in_proj(x, sc, sh, w_bf, tab, tm):
    b, t, _ = x.shape
    nt = t // tm
    row = lambda w: pl.BlockSpec((1, tm, w), lambda i, j: (i, j, 0))
    mod = row(D_MODEL) if sc.shape[1] == t else pl.BlockSpec((1, 1, D_MODEL), lambda i, j: (i, 0, 0))
    outs = [(D_ATT, BF16), (D_ATT, BF16), (1024, F32), (512, F32), (1024, BF16),
            (D_SSM, BF16), (CONV_CH, F32), (128, F32)]
    return pl.pallas_call(
        _inproj_kernel,
        out_shape=[jax.ShapeDtypeStruct((b, t, w), dt) for w, dt in outs],
        grid=(b, nt),
        in_specs=[row(D_MODEL), mod, mod,
                  pl.BlockSpec((D_MODEL, D_INP), lambda i, j: (0, 0), pipeline_mode=pl.Buffered(1)),
                  pl.BlockSpec((tm, 768), lambda i, j: (j, 0))],
        out_specs=[row(w) for w, _ in outs],
        compiler_params=_cparams(("parallel", "parallel")),
        name="in_proj",
    )(x, sc, sh, w_bf, tab)


def _prep_cmp_weights(cmp_pe, cmp_w1, cmp_b1, cmp_w2):
    w1 = cmp_w1.reshape(2, 2, CMP_STRIDE, HEAD_DIM, CMP_HID)
    eye = jnp.eye(2, dtype=cmp_w1.dtype)
    w1p = jnp.einsum('phsdj,ab->psadhbj', w1, eye).reshape(2, CMP_STRIDE * 128, 4 * CMP_HID)
    w2p = jnp.einsum('pjd,ab->pajbd', cmp_w2, eye).reshape(2, 2 * CMP_HID, 2 * HEAD_DIM)
    pe = cmp_pe.reshape(2, 1, CMP_LEN * HEAD_DIM)
    pe = jnp.concatenate([pe, jnp.zeros((2, 7, CMP_LEN * HEAD_DIM), pe.dtype)], axis=1)
    b1 = jnp.concatenate([cmp_b1, cmp_b1], axis=-1).reshape(2, 1, 2 * CMP_HID)
    return w1p.astype(BF16), w2p.astype(BF16), pe, cmp_w1, b1


def _gelu_tanh(x):
    return 0.5 * x * (1.0 + jnp.tanh(0.7978845608028654 * (x + 0.044715 * x * x * x)))


def _compress_core(load, nrows, w1p_ref, w2p_ref, pe_ref, w1_ref, b1_ref, store, prepare=None):
    for p in range(2):
        pe_bias = _dot(pe_ref[p], w1_ref[p], precision=HI)[0:1]
        pe_bias = jnp.concatenate([pe_bias, pe_bias], axis=1)
        for pr in range(2):
            if prepare is not None:
                prepare(2 * p + pr)
            xs = jnp.concatenate([load(s, 2 * p + pr).astype(BF16) for s in range(CMP_STRIDE)], axis=1)
            ab = _dot(xs, w1p_ref[p])
            a, bb = ab[:, 0:256], ab[:, 256:512]
            pre = a + pltpu.roll(bb, nrows - 1, 0) + pe_bias + b1_ref[p]
            hid = _gelu_tanh(pre).astype(BF16)
            store(p, pr, _dot(hid, w2p_ref[p]))


def _compress_prompt_kernel(b0, b1, b2, b3, w1p_ref, w2p_ref, pe_ref, w1_ref, b1_ref, kc_ref, vc_ref):
    blocks = (b0, b1, b2, b3)
    n = b0.shape[1] // CMP_STRIDE
    outs = (kc_ref, vc_ref)

    def load(s, lb):
        return blocks[lb][0, pl.ds(s, n, stride=CMP_STRIDE), :]

    def store(p, pr, val):
        outs[p][0, :, pr * 128:(pr + 1) * 128] = val.astype(BF16)

    _compress_core(load, n, w1p_ref, w2p_ref, pe_ref, w1_ref, b1_ref, store)


def _wspecs():
    z3 = lambda *a: (0, 0, 0)
    return [pl.BlockSpec((2, CMP_STRIDE * 128, 4 * CMP_HID), z3),
            pl.BlockSpec((2, 2 * CMP_HID, 2 * HEAD_DIM), z3),
            pl.BlockSpec((2, 8, CMP_LEN * HEAD_DIM), z3),
            pl.BlockSpec((2, CMP_LEN * HEAD_DIM, CMP_HID), z3),
            pl.BlockSpec((2, 1, 2 * CMP_HID), z3)]


def _compress_prompt(kv_f32, cw):
    b, t, _ = kv_f32.shape
    n = t // CMP_STRIDE
    lane_blk = lambda lb: pl.BlockSpec((1, t, 128), lambda i: (i, 0, lb))
    return pl.pallas_call(
        _compress_prompt_kernel,
        out_shape=[jax.ShapeDtypeStruct((b, n, 256), BF16)] * 2,
        grid=(b,),
        in_specs=[lane_blk(lb) for lb in range(4)] + _wspecs(),
        out_specs=[pl.BlockSpec((1, n, 256), lambda i: (i, 0, 0))] * 2,
        compiler_params=_cparams(("parallel",)),
        name="compress_prompt",
    )(kv_f32, kv_f32, kv_f32, kv_f32, *cw)


CMP_PAGES = 32


def _compress_sample_kernel(pt_ref, *refs):
    npg = CMP_PAGES + 1
    pages = refs[:npg]
    w1p_ref, w2p_ref, pe_ref, w1_ref, b1_ref, kc_ref, vc_ref, tok_ref = refs[npg:]
    cpp = PAGE // CMP_STRIDE
    nrows = CMP_PAGES * cpp
    outs = (kc_ref, vc_ref)
    def prepare(lb):
        for k in range(npg):
            tok_ref[lb, k * PAGE:(k + 1) * PAGE, :] = pages[k][0, lb * 128:(lb + 1) * 128, :].T

    def load(s, lb):
        return tok_ref[lb, pl.ds(s, nrows + cpp, stride=CMP_STRIDE), :]

    def store(p, pr, val):
        outs[p][0, :, pr * 128:(pr + 1) * 128] = val[:nrows].astype(BF16)

    _compress_core(load, nrows + cpp, w1p_ref, w2p_ref, pe_ref, w1_ref, b1_ref, store, prepare)


def _compress_sample(cache_t, page_table, cw):
    b, n_pages = page_table.shape
    steps = n_pages // CMP_PAGES
    nrows = CMP_PAGES * (PAGE // CMP_STRIDE)

    def pspec(k):
        return pl.BlockSpec(
            (1, 512, PAGE),
            lambda i, j, pt: (pt[i, jnp.minimum(j * CMP_PAGES + k, n_pages - 1)], 0, 0))

    gs = pltpu.PrefetchScalarGridSpec(
        num_scalar_prefetch=1,
        grid=(b, steps),
        in_specs=[pspec(k) for k in range(CMP_PAGES + 1)]
                 + [pl.BlockSpec(s.block_shape, lambda i, j, pt: (0, 0, 0)) for s in _wspecs()],
        out_specs=[pl.BlockSpec((1, nrows, 256), lambda i, j, pt: (i, j, 0))] * 2,
        scratch_shapes=[pltpu.VMEM((4, (CMP_PAGES + 1) * PAGE, 128), F32)],
    )
    return pl.pallas_call(
        _compress_sample_kernel,
        out_shape=[jax.ShapeDtypeStruct((b, n_pages * 8, 256), BF16)] * 2,
        grid_spec=gs,
        compiler_params=_cparams(("parallel", "parallel")),
        name="compress_sample",
    )(page_table, *([cache_t] * (CMP_PAGES + 1)), *cw)


DT_LANE = D_GT


def _prep_ssm_params(conv_w, conv_b, dt_bias, a_log, d_skip, ssm_norm_g):
    cw = jnp.concatenate([conv_w, jnp.zeros((8 - CONV_W, CONV_CH), conv_w.dtype)], axis=0)
    lane = lambda v: jnp.zeros((1, 128), F32).at[0, DT_LANE:DT_LANE + N_HEADS_SSM].set(v)
    dsk = jnp.repeat(d_skip, HEAD_DIM).reshape(1, D_SSM)
    return (cw, conv_b.reshape(1, CONV_CH), lane(dt_bias), lane(a_log), dsk,
            ssm_norm_g.reshape(1, D_SSM))


def _ssd_kernel(xbc_ref, z_ref, gd_ref, cprev_ref, h0_ref, cw_ref, cb_ref, dtb_ref, alog_ref,
                dsk_ref, ng_ref, y_ref, st_ref, xpad_ref):
    q = xbc_ref.shape[1]
    c = pl.program_id(1)

    @pl.when(c == 0)
    def _():
        st_ref[0] = h0_ref[0]
        xpad_ref[0:8, :] = cprev_ref[0]

    xpad_ref[8:8 + q, :] = xbc_ref[0]
    conv = cb_ref[...] + xpad_ref[pl.ds(5, q), :] * cw_ref[0:1, :]
    for k in range(1, CONV_W):
        conv = conv + xpad_ref[pl.ds(5 + k, q), :] * cw_ref[k:k + 1, :]
    xpad_ref[0:8, :] = xpad_ref[q:q + 8, :]
    xc = _silu(conv)
    xs = xc[:, 0:D_SSM]

    lane = lax.broadcasted_iota(jnp.int32, (1, 128), 1)
    in_dt = (lane >= DT_LANE) & (lane < DT_LANE + N_HEADS_SSM)
    v = gd_ref[0] + dtb_ref[...]
    dt = jnp.maximum(v, 0.0) + jnp.log1p(jnp.exp(-jnp.abs(v)))
    a = jnp.where(in_dt, -jnp.exp(alog_ref[...]), 0.0) * dt
    er = lax.broadcasted_iota(jnp.int32, (128, D_SSM), 0)
    ec = lax.broadcasted_iota(jnp.int32, (128, D_SSM), 1)
    expand = er - DT_LANE == ec // HEAD_DIM
    ri = lax.broadcasted_iota(jnp.int32, (q, q), 0)
    ci = lax.broadcasted_iota(jnp.int32, (q, q), 1)
    causal = ri >= ci
    acs = _dot_sel_l(causal, a)
    acs_t = acs.T
    dt_x = _dot_sel_r(jnp.where(in_dt, dt, 0.0), expand)
    acs_x = _dot_sel_r(acs, expand)
    last_x = acs_x[q - 1:q, :]
    grow_x = jnp.exp(acs_x)
    decay_x = jnp.exp(last_x - acs_x)
    er2 = lax.broadcasted_iota(jnp.int32, (D_SSM, 128), 0)
    ec2 = lax.broadcasted_iota(jnp.int32, (D_SSM, 128), 1)
    expand_t = ec2 - DT_LANE == er2 // HEAD_DIM
    tot_col = jnp.exp(_dot_sel_l(expand_t, acs_t)[:, q - 1:q])

    xd = xs * dt_x
    xdd = (xd * decay_x).astype(BF16)
    xd_b = xd.astype(BF16)
    half = lax.broadcasted_iota(jnp.int32, (1, 128), 1) < HEAD_DIM
    ys = []
    for g in range(SSM_GROUPS):
        bg = xc[:, D_SSM + g * D_STATE:D_SSM + (g + 1) * D_STATE].astype(BF16)
        cg = xc[:, D_SSM + (SSM_GROUPS + g) * D_STATE:D_SSM + (SSM_GROUPS + g + 1) * D_STATE].astype(BF16)
        cbm = _dot_nt(cg, bg)
        for m in (2 * g, 2 * g + 1):
            sl = slice(128 * m, 128 * (m + 1))
            yh = []
            for hh in (2 * m, 2 * m + 1):
                col = acs[:, DT_LANE + hh:DT_LANE + hh + 1]
                row = acs_t[DT_LANE + hh:DT_LANE + hh + 1, :]
                lm = jnp.where(causal, jnp.exp(jnp.where(causal, col - row, 0.0)), 0.0)
                yh.append(_dot((cbm * lm).astype(BF16), xd_b[:, sl]))
            y_diag = jnp.where(half, yh[0], yh[1])
            st = st_ref[0, sl, :]
            y_off = _dot_nt(cg, st.astype(BF16)) * grow_x[:, sl]
            st_ref[0, sl, :] = st * tot_col[sl, :] + _dot_tn(xdd[:, sl], bg)
            ys.append(y_diag + y_off)
    y = jnp.concatenate(ys, axis=1) + dsk_ref[...] * xs
    y = y * _silu(z_ref[0].astype(F32))
    gw = D_SSM // SSM_GROUPS
    outs = []
    for g in range(SSM_GROUPS):
        blk = y[:, g * gw:(g + 1) * gw]
        ms = jnp.mean(blk * blk, axis=1, keepdims=True)
        outs.append(blk * lax.rsqrt(ms + RMS_EPS))
    y_ref[0] = (jnp.concatenate(outs, axis=1) * ng_ref[...]).astype(BF16)


def _ssd(xbc, z, gd, conv_prev8, h0, sp, q):
    b, t, _ = xbc.shape
    nc = t // q
    row = lambda w: pl.BlockSpec((1, q, w), lambda i, j: (i, j, 0))
    per_b = lambda r, w: pl.BlockSpec((1, r, w), lambda i, j: (i, 0, 0))
    par = lambda r, w: pl.BlockSpec((r, w), lambda i, j: (0, 0))
    return pl.pallas_call(
        _ssd_kernel,
        out_shape=[jax.ShapeDtypeStruct((b, t, D_SSM), BF16),
                   jax.ShapeDtypeStruct((b, D_SSM, D_STATE), F32)],
        grid=(b, nc),
        in_specs=[row(CONV_CH), row(D_SSM), row(128), per_b(8, CONV_CH), per_b(D_SSM, D_STATE),
                  par(8, CONV_CH), par(1, CONV_CH), par(1, 128), par(1, 128), par(1, D_SSM),
                  par(1, D_SSM)],
        out_specs=[row(D_SSM), per_b(D_SSM, D_STATE)],
        scratch_shapes=[pltpu.VMEM((q + 8, CONV_CH), F32)],
        compiler_params=_cparams(("parallel", "arbitrary")),
        name="ssd",
    )(xbc, z, gd, conv_prev8, h0, *sp)


ATT_TQ = 256
SCALE = HEAD_DIM ** -0.5


def _select_blocks_t(ps, n_blk):
    blk = lax.broadcasted_iota(jnp.int32, ps.shape, 0)
    rank = jnp.zeros(ps.shape, F32)
    for i in range(n_blk):
        vi = ps[i:i + 1, :]
        rank = rank + jnp.where(vi > ps, 1.0, jnp.where((vi == ps) & (blk > i), 1.0, 0.0))
    return jnp.where(rank < N_SELECT, 1.0, 0.0)


def _attn_prompt_kernel(qn_ref, qr_ref, kc_ref, vc_ref, ks_ref, vs_ref, kw_ref, vw_ref, gd_ref,
                        o_ref, qz_ref, sel_ref, ocmp_ref, m_ref, acc_ref):
    tq = qn_ref.shape[1]
    t_all = ks_ref.shape[1]
    n_cmp = t_all // CMP_STRIDE - 1
    n_blk = t_all // SLC_BLK
    gp = pl.program_id(1)
    qi = pl.program_id(2)
    rep = lambda a: jnp.concatenate([a] * HPG, axis=1)
    pos = qi * tq + lax.broadcasted_iota(jnp.int32, (1, tq), 1)
    pos4 = rep(pos)
    lane = lax.broadcasted_iota(jnp.int32, (1, 128), 1)
    row = lax.broadcasted_iota(jnp.int32, (128, 1), 0)
    lane_side = (lane < HEAD_DIM, lane >= HEAD_DIM)
    row_side = (row < HEAD_DIM, row >= HEAD_DIM)

    for r, ref in enumerate((qn_ref, qr_ref)):
        for side in range(2):
            for j in range(HPG):
                col = ref[0, :, j * 128:(j + 1) * 128]
                col = jnp.where(lane_side[side], col, jnp.zeros_like(col))
                qz_ref[2 * r + side, j * tq:(j + 1) * tq, :] = col * SCALE

    def v_sides(v):
        vt = v.astype(F32).T
        return [jnp.where(row_side[s], vt, 1.0).astype(BF16) for s in range(2)]

    kc = kc_ref[0]
    vct = vc_ref[0].astype(F32).T.astype(BF16)
    vis = (CMP_STRIDE * row + (CMP_LEN - 1) <= pos4) & (row < n_cmp)
    br = lax.broadcasted_iota(jnp.int32, (128, 128), 0) * SLC_BLK
    cc = lax.broadcasted_iota(jnp.int32, (128, 128), 1) * CMP_STRIDE
    overlap_t = ((cc < br + SLC_BLK) & (cc + CMP_LEN > br)).astype(F32)
    cur = pos // SLC_BLK
    forced = (row == 0) | (row == cur) | (row == cur - 1)
    for side in range(2):
        s = jnp.where(vis, _dot_nt(kc, qz_ref[side]), NEG)
        e = jnp.exp(s - jnp.max(s, axis=0, keepdims=True))
        p = jnp.where(vis, e / jnp.sum(e, axis=0, keepdims=True), 0.0)
        ocmp_ref[side] = _dot(vct, p.astype(BF16))
        psum = p[:, 0:tq]
        for j in range(1, HPG):
            psum = psum + p[:, j * tq:(j + 1) * tq]
        ps = _dot(overlap_t, psum, precision=HI)
        ps = jnp.where(row <= cur, jnp.where(forced, FORCE, ps), NEG)
        sel = _select_blocks_t(ps[0:n_blk, :], n_blk)
        sel_ref[side] = jnp.concatenate([sel, jnp.zeros((128 - n_blk, tq), F32)], axis=0).astype(BF16)

    m_ref[...] = jnp.full(m_ref.shape, NEG, F32)
    acc_ref[...] = jnp.zeros(acc_ref.shape, F32)

    def update(idx, s, vts):
        m_old = m_ref[idx]
        m_new = jnp.maximum(m_old, jnp.max(s, axis=0, keepdims=True))
        p = jnp.exp(s - m_new).astype(BF16)
        acc_ref[idx] = jnp.exp(m_old - m_new) * acc_ref[idx] + _dot(vts, p)
        m_ref[idx] = m_new

    def slc_tile(t, diagonal):
        k0 = pl.multiple_of(t * tq, tq)
        k = ks_ref[0, pl.ds(k0, tq), :]
        vts = v_sides(vs_ref[0, pl.ds(k0, tq), :])
        kpos = k0 + lax.broadcasted_iota(jnp.int32, (tq, 1), 0)
        expand_t = (lax.broadcasted_iota(jnp.int32, (tq, 128), 1) == kpos // SLC_BLK).astype(BF16)
        for side in range(2):
            keep = _dot(expand_t, sel_ref[side])
            if diagonal:
                keep = jnp.where(kpos <= pos, keep, 0.0)
            bias = (keep - 1.0) * (-NEG)
            update(side, _dot_nt(k, qz_ref[2 + side]) + rep(bias), vts[side])

    def slc_body(t, carry):
        slc_tile(t, False)
        return carry

    lax.fori_loop(0, qi, slc_body, 0)
    slc_tile(qi, True)

    def win_tile(t, kind):
        k0 = pl.multiple_of(t * tq, tq)
        k = kw_ref[0, pl.ds(k0, tq), :]
        vts = v_sides(vw_ref[0, pl.ds(k0, tq), :])
        kpos = k0 + lax.broadcasted_iota(jnp.int32, (tq, 1), 0)
        mask = (kpos > pos - WINDOW) if kind == 0 else ((kpos <= pos) if kind == 2 else None)
        bias = None if mask is None else rep(jnp.where(mask, 0.0, NEG))
        for side in range(2):
            s = _dot_nt(k, qz_ref[2 + side])
            if bias is not None:
                s = s + bias
            update(2 + side, s, vts[side])

    @pl.when(qi >= 2)
    def _():
        win_tile(qi - 2, 0)

    @pl.when(qi >= 1)
    def _():
        win_tile(qi - 1, 1)

    win_tile(qi, 2)

    gates_t = _sigmoid(gd_ref[0]).T
    for j in range(HPG):
        sl = slice(j * tq, (j + 1) * tq)
        cols = []
        for side in range(2):
            lrow = HEAD_DIM if side == 0 else 0
            gi = (2 * gp + side) * (3 * HPG) + 3 * j
            g = [jnp.sum(jnp.where(row == gi + k, gates_t, 0.0), axis=0, keepdims=True)
                 for k in range(3)]
            a_s = acc_ref[side, :, sl]
            a_w = acc_ref[2 + side, :, sl]
            cols.append(g[0] * ocmp_ref[side, :, sl] + (g[1] / a_s[lrow:lrow + 1, :]) * a_s
                        + (g[2] / a_w[lrow:lrow + 1, :]) * a_w)
        o_ref[0, :, j * 128:(j + 1) * 128] = jnp.where(row_side[0], cols[0], cols[1]).T.astype(BF16)


def _attn_prompt(qn, qr, kc, vc, kvb, gd):
    b, t, _ = qn.shape
    tq = ATT_TQ
    assert WINDOW == 2 * tq and t % tq == 0 and t // SLC_BLK <= 128 and kc.shape[1] == 128
    tw = HPG * tq
    qspec = pl.BlockSpec((1, tq, 512), lambda i, g, q: (i, q, g))
    cspec = pl.BlockSpec((1, kc.shape[1], 128), lambda i, g, q: (i, 0, g))
    kvspec = lambda base: pl.BlockSpec((1, t, 128), lambda i, g, q: (i, 0, base + g))
    return pl.pallas_call(
        _attn_prompt_kernel,
        out_shape=jax.ShapeDtypeStruct((b, t, D_ATT), BF16),
        grid=(b, 2, t // tq),
        in_specs=[qspec, qspec, cspec, cspec, kvspec(0), kvspec(2), kvspec(4), kvspec(6),
                  pl.BlockSpec((1, tq, 128), lambda i, g, q: (i, q, 0))],
        out_specs=qspec,
        scratch_shapes=[pltpu.VMEM((4, tw, 128), BF16), pltpu.VMEM((2, 128, tq), BF16),
                        pltpu.VMEM((2, 128, tw), F32), pltpu.VMEM((4, 1, tw), F32),
                        pltpu.VMEM((4, 128, tw), F32)],
        compiler_params=_cparams(("parallel", "parallel", "arbitrary")),
        name="attn_prompt",
    )(qn, qr, kc, vc, kvb, kvb, kvb, kvb, gd)


ATS_PAGES = 16
N_ROWS_S = 128


def _attn_sample_kernel(pt_ref, *refs, n_steps, past, t_dec):
    npg = ATS_PAGES
    kpages = refs[0:2 * npg:2]
    vpages = refs[1:2 * npg:2]
    (qn_ref, qr_ref, kc_ref, vc_ref, kvn_ref, cwin_ref, wnew_ref, gt_ref,
     o_ref, sel_ref, ocmp_ref, m_ref, l_ref, acc_ref) = refs[2 * npg:]
    s_id = pl.program_id(1)
    nr = N_ROWS_S
    rows = lax.broadcasted_iota(jnp.int32, (nr, 1), 0)
    t_row = (rows % (t_dec * HPG)) // HPG
    pos = past + t_row
    qr = qr_ref[0]

    @pl.when(s_id == 0)
    def _():
        n_c = kc_ref.shape[1]
        n_blk = (past + t_dec + SLC_BLK - 1) // SLC_BLK
        bps = ATS_PAGES * PAGE // SLC_BLK
        nbl = -(-(n_steps * bps + 128) // 128) * 128
        cl = lax.broadcasted_iota(jnp.int32, (1, n_c), 1)
        vis = (CMP_STRIDE * cl + (CMP_LEN - 1) <= pos) & (cl < n_c - 1)
        s = jnp.where(vis, _dot_nt(qn_ref[0], kc_ref[0]) * SCALE, NEG)
        e = jnp.exp(s - jnp.max(s, axis=1, keepdims=True))
        p = jnp.where(vis, e / jnp.sum(e, axis=1, keepdims=True), 0.0)
        ocmp_ref[...] = _dot(p.astype(BF16), vc_ref[0])
        ng = nr // HPG
        gsum = (lax.broadcasted_iota(jnp.int32, (ng, nr), 1) // HPG
                == lax.broadcasted_iota(jnp.int32, (ng, nr), 0)).astype(F32)
        cr = lax.broadcasted_iota(jnp.int32, (n_c, nbl), 0) * CMP_STRIDE
        j0 = lax.broadcasted_iota(jnp.int32, (n_c, nbl), 1) * SLC_BLK
        overlap = ((cr < j0 + SLC_BLK) & (cr + CMP_LEN > j0)).astype(F32)
        ps = _dot(_dot(gsum, p, precision=HI), overlap, precision=HI)
        bl = lax.broadcasted_iota(jnp.int32, (1, nbl), 1)
        g_rows = lax.broadcasted_iota(jnp.int32, (ng, 1), 0)
        cur = (past + g_rows % t_dec) // SLC_BLK
        forced = (bl == 0) | (bl == cur) | (bl == cur - 1)
        ps = jnp.where(bl <= cur, jnp.where(forced, FORCE, ps), NEG)
        rank = jnp.zeros(ps.shape, F32)
        for i in range(n_blk):
            vi = ps[:, i:i + 1]
            rank = rank + ((vi > ps) | ((vi == ps) & (bl > i))).astype(F32)
        sel = jnp.where((rank < N_SELECT) & (bl < n_blk), 1.0, 0.0)
        gexp = (lax.broadcasted_iota(jnp.int32, (nr, ng), 0) // HPG
                == lax.broadcasted_iota(jnp.int32, (nr, ng), 1)).astype(F32)
        sel_rows = _dot(gexp, sel)
        for w in range(sel_ref.shape[0]):
            sel_ref[w] = sel_rows[:, bps * w:bps * w + 128].astype(BF16)
        m_ref[...] = jnp.full(m_ref.shape, NEG, F32)
        l_ref[...] = jnp.zeros(l_ref.shape, F32)
        acc_ref[...] = jnp.zeros(acc_ref.shape, F32)

    def update(s, mask, v, v_transposed=False):
        s = jnp.where(mask, s, NEG)
        m_old = m_ref[...]
        m_new = jnp.maximum(m_old, jnp.max(s, axis=1, keepdims=True))
        p = jnp.where(mask, jnp.exp(s - m_new), 0.0)
        alpha = jnp.exp(m_old - m_new)
        l_ref[...] = alpha * l_ref[...] + jnp.sum(p, axis=1, keepdims=True)
        pv = _dot_nt(p.astype(BF16), v) if v_transposed else _dot(p.astype(BF16), v)
        acc_ref[...] = alpha * acc_ref[...] + pv
        m_ref[...] = m_new

    nk = npg * PAGE
    k_t = jnp.concatenate([r[0] for r in kpages], axis=1).astype(BF16)
    v_t = jnp.concatenate([r[0] for r in vpages], axis=1).astype(BF16)
    expand = (lax.broadcasted_iota(jnp.int32, (128, nk), 0)
              == lax.broadcasted_iota(jnp.int32, (128, nk), 1) // SLC_BLK).astype(BF16)
    kpos = s_id * nk + lax.broadcasted_iota(jnp.int32, (1, nk), 1)
    mask = (_dot(sel_ref[s_id], expand) > 0.5) & (kpos <= pos)
    update(_dot(qr, k_t) * SCALE, mask, v_t, v_transposed=True)

    @pl.when(s_id == n_steps - 1)
    def _():
        kn = kvn_ref[0, :, 512:768].astype(BF16)
        vn = kvn_ref[0, :, 768:1024].astype(BF16)
        tk = lax.broadcasted_iota(jnp.int32, (1, t_dec), 1)
        cur_sel = sel_ref[n_steps][:, 0:1].astype(F32) > 0.5
        update(_dot_nt(qr, kn) * SCALE, cur_sel & (past + tk <= pos), vn)
        o_slc = acc_ref[...] / l_ref[...]
        wb = cwin_ref.shape[1]
        kw = cwin_ref[0, :, 0:256].astype(BF16)
        vw = cwin_ref[0, :, 256:512].astype(BF16)
        kwn = wnew_ref[0, :, 0:256].astype(BF16)
        vwn = wnew_ref[0, :, 256:512].astype(BF16)
        d1 = pos - (past - wb + lax.broadcasted_iota(jnp.int32, (1, wb), 1))
        d2 = pos - (past + tk)
        m1 = (d1 >= 0) & (d1 < WINDOW)
        m2 = (d2 >= 0) & (d2 < WINDOW)
        s1 = jnp.where(m1, _dot_nt(qr, kw) * SCALE, NEG)
        s2 = jnp.where(m2, _dot_nt(qr, kwn) * SCALE, NEG)
        mx = jnp.maximum(jnp.max(s1, axis=1, keepdims=True), jnp.max(s2, axis=1, keepdims=True))
        p1 = jnp.where(m1, jnp.exp(s1 - mx), 0.0)
        p2 = jnp.where(m2, jnp.exp(s2 - mx), 0.0)
        den = jnp.sum(p1, axis=1, keepdims=True) + jnp.sum(p2, axis=1, keepdims=True)
        o_win = (_dot(p1.astype(BF16), vw) + _dot(p2.astype(BF16), vwn)) / den
        g = _sigmoid(gt_ref[0])
        o_ref[0] = g[:, 0:1] * ocmp_ref[...] + g[:, 1:2] * o_slc + g[:, 2:3] * o_win


def _attn_sample(cache_t, page_table, qn_x, qr_x, kc, vc, kv_new, cache_win2, win_new, gt_rows, t_dec):
    b, n_pages = page_table.shape
    past = n_pages * PAGE
    n_steps = n_pages // ATS_PAGES
    nr = N_ROWS_S

    def pspec(k, blk):
        return pl.BlockSpec((1, 256, PAGE), lambda i, s, pt: (pt[i, s * ATS_PAGES + k], blk, 0))

    per_b = lambda r, w: pl.BlockSpec((1, r, w), lambda i, s, pt: (i, 0, 0))
    page_specs = []
    for k in range(ATS_PAGES):
        page_specs += [pspec(k, 2), pspec(k, 3)]
    gs = pltpu.PrefetchScalarGridSpec(
        num_scalar_prefetch=1,
        grid=(b, n_steps),
        in_specs=page_specs + [per_b(nr, 256), per_b(nr, 256), per_b(kc.shape[1], 256),
                               per_b(kc.shape[1], 256), per_b(t_dec, 1024),
                               per_b(cache_win2.shape[1], 512), per_b(t_dec, 512), per_b(nr, 128)],
        out_specs=per_b(nr, 256),
        scratch_shapes=[pltpu.VMEM((n_steps + 1, nr, 128), BF16), pltpu.VMEM((nr, 256), F32),
                        pltpu.VMEM((nr, 1), F32), pltpu.VMEM((nr, 1), F32),
                        pltpu.VMEM((nr, 256), F32)],
    )
    return pl.pallas_call(
        functools.partial(_attn_sample_kernel, n_steps=n_steps, past=past, t_dec=t_dec),
        out_shape=jax.ShapeDtypeStruct((b, nr, 256), F32),
        grid_spec=gs,
        compiler_params=_cparams(("parallel", "arbitrary")),
        name="attn_sample",
    )(page_table, *([cache_t] * (2 * ATS_PAGES)), qn_x, qr_x, kc, vc, kv_new, cache_win2, win_new,
      gt_rows)


def _expand_rows(q):
    b, t, _ = q.shape
    q5 = q.reshape(b, t, N_KV, HPG, HEAD_DIM)
    eye = jnp.eye(N_KV, dtype=q.dtype)
    return jnp.einsum('btghd,gk->bgthkd', q5, eye).reshape(b, N_KV * t * HPG, N_KV * HEAD_DIM)


def _collapse_rows(o, t):
    b = o.shape[0]
    o6 = o.reshape(b, N_KV, t, HPG, N_KV, HEAD_DIM)
    return jnp.einsum('bgthkd,gk->btghd', o6, jnp.eye(N_KV, dtype=o.dtype)).reshape(b, t, D_ATT)


def _layernorm(x, g, b):
    mu = jnp.mean(x, axis=-1, keepdims=True)
    xc = x - mu
    var = jnp.mean(xc * xc, axis=-1, keepdims=True)
    return xc * lax.rsqrt(var + LN_EPS) * g + b


OUT_SUB = 128


def _outproj_kernel(oa_ref, ys_ref, x_ref, g1_ref, sc2_ref, sh2_ref, w_ref, lg_ref, lb_ref,
                    rw_ref, rb_ref, x1_ref, h2_ref, ti_ref, tw_ref):
    tm = x_ref.shape[1]
    sub = min(OUT_SUB, tm)
    for r0 in range(0, tm, sub):
        rs = slice(r0, r0 + sub)
        mix = _dot(oa_ref[0, rs, :], w_ref[0:D_ATT, :]) + _dot(ys_ref[0, rs, :], w_ref[D_ATT:, :])
        x1 = _layernorm(DN_ALPHA * x_ref[0, rs, :] + g1_ref[0] * mix, lg_ref[...], lb_ref[...])
        x1_ref[0, rs, :] = x1
        h2 = x1 * (1.0 + sc2_ref[0]) + sh2_ref[0]
        h_hi = h2.astype(BF16)
        bits = lax.bitcast_convert_type(h_hi.astype(F32), jnp.uint32)
        h2_ref[0, rs, :] = ((bits[:, :D_MODEL // 2] >> 16)
                            | (bits[:, D_MODEL // 2:] & jnp.uint32(0xFFFF0000)))
        h_lo = (h2 - h_hi.astype(F32)).astype(BF16)
        logits = (_dot(h_hi, rw_ref[0]) + (_dot(h_lo, rw_ref[0]) + _dot(h_hi, rw_ref[1]))
                  + rb_ref[...])
        lane = lax.broadcasted_iota(jnp.int32, logits.shape, 1)
        vals, ids = [], []
        for _ in range(TOP_K):
            mx = jnp.max(logits, axis=1, keepdims=True)
            ix = jnp.min(jnp.where(logits == mx, lane, 128), axis=1, keepdims=True)
            vals.append(mx)
            ids.append(ix)
            logits = jnp.where(lane == ix, -jnp.inf, logits)
        es = [jnp.exp(v - vals[0]) for v in vals]
        inv = 1.0 / (es[0] + es[1] + es[2] + es[3])
        ti = jnp.zeros(lane.shape, jnp.int32)
        tw = jnp.zeros(lane.shape, F32)
        for k in range(TOP_K):
            ti = jnp.where(lane == k, ids[k], ti)
            tw = jnp.where(lane == k, es[k] * inv, tw)
        ti_ref[0, rs, :] = ti
        tw_ref[0, rs, :] = tw


def _out_proj(o_att, y_ssm, x, g1, sc2, sh2, w_out_bf, ln_g, ln_b, rw, rb, tm):
    b, t, _ = x.shape
    row = lambda w: pl.BlockSpec((1, tm, w), lambda i, j: (i, j, 0))
    per_b = pl.BlockSpec((1, 1, D_MODEL), lambda i, j: (i, 0, 0))
    par = lambda r, w, **kw: pl.BlockSpec((r, w), lambda i, j: (0, 0), **kw)
    return pl.pallas_call(
        _outproj_kernel,
        out_shape=[jax.ShapeDtypeStruct((b, t, D_MODEL), F32),
                   jax.ShapeDtypeStruct((b, t, D_MODEL // 2), jnp.uint32),
                   jax.ShapeDtypeStruct((b, t, 128), jnp.int32),
                   jax.ShapeDtypeStruct((b, t, 128), F32)],
        grid=(b, t // tm),
        in_specs=[row(D_ATT), row(D_SSM), row(D_MODEL), per_b, per_b, per_b,
                  par(D_MODEL, D_MODEL, pipeline_mode=pl.Buffered(1)),
                  par(1, D_MODEL), par(1, D_MODEL),
                  pl.BlockSpec((2, D_MODEL, 128), lambda i, j: (0, 0, 0)), par(1, 128)],
        out_specs=[row(D_MODEL), row(D_MODEL // 2), row(128), row(128)],
        compiler_params=_cparams(("parallel", "parallel")),
        name="out_proj",
    )(o_att, y_ssm, x, g1, sc2, sh2, w_out_bf, ln_g, ln_b, rw, rb)


MOE_TM = 1024
MOE_TN = 512
ROUTE_TM = 256


def _route_kernel(ti_ref, rank_ref, cnt_ref):
    @pl.when(pl.program_id(0) == 0)
    def _():
        cnt_ref[...] = jnp.zeros_like(cnt_ref)

    ti = ti_ref[...]
    tm = ti.shape[0]
    lane = lax.broadcasted_iota(jnp.int32, (tm, 128), 1)
    hits = [lane == ti[:, k:k + 1] for k in range(TOP_K)]
    oh = jnp.zeros((tm, 128), F32)
    for h in hits:
        oh = oh + h.astype(F32)
    ri = lax.broadcasted_iota(jnp.int32, (tm, tm), 0)
    ci = lax.broadcasted_iota(jnp.int32, (tm, tm), 1)
    before = _dot((ri > ci).astype(BF16), oh.astype(BF16)) + cnt_ref[0:1, :]
    rank = jnp.zeros((tm, 128), jnp.int32)
    for k, h in enumerate(hits):
        rk = jnp.sum(jnp.where(h, before, 0.0), axis=1, keepdims=True).astype(jnp.int32)
        rank = jnp.where(lane == k, rk, rank)
    rank_ref[...] = rank
    cnt_ref[...] = cnt_ref[...] + jnp.sum(oh, axis=0, keepdims=True)


def _route(ti_all):
    n = ti_all.shape[0]
    return pl.pallas_call(
        _route_kernel,
        out_shape=[jax.ShapeDtypeStruct((n, 128), jnp.int32),
                   jax.ShapeDtypeStruct((8, 128), F32)],
        grid=(n // ROUTE_TM,),
        in_specs=[pl.BlockSpec((ROUTE_TM, 128), lambda i: (i, 0))],
        out_specs=[pl.BlockSpec((ROUTE_TM, 128), lambda i: (i, 0)),
                   pl.BlockSpec((8, 128), lambda i: (0, 0))],
        compiler_params=_cparams(("arbitrary",)),
        name="moe_route",
    )(ti_all)


DMA_UNROLL = 8


def _dispatch_kernel(pos_ref, h_ref, xs_in_ref, xs_ref, sem):
    del xs_in_ref
    def copy(t, k):
        return pltpu.make_async_copy(h_ref.at[pl.ds(t, 1), :],
                                     xs_ref.at[pl.ds(pos_ref[0, 0, t * TOP_K + k], 1), :], sem)

    def start(t, c):
        for k in range(TOP_K):
            copy(t, k).start()
        return c

    def wait(t, c):
        for k in range(TOP_K):
            copy(t, k).wait()
        return c

    lax.fori_loop(0, ROUTE_TM, start, 0, unroll=DMA_UNROLL // TOP_K)
    lax.fori_loop(0, ROUTE_TM, wait, 0, unroll=DMA_UNROLL // TOP_K)


def _dispatch(pos, h, xs):
    n, w = h.shape
    assert n % ROUTE_TM == 0
    steps = n // ROUTE_TM
    return pl.pallas_call(
        _dispatch_kernel,
        out_shape=jax.ShapeDtypeStruct(xs.shape, xs.dtype),
        grid=(steps,),
        in_specs=[pl.BlockSpec((1, 1, ROUTE_TM * TOP_K), lambda i: (i, 0, 0), memory_space=pltpu.SMEM),
                  pl.BlockSpec((ROUTE_TM, w), lambda i: (i, 0)),
                  pl.BlockSpec(memory_space=pl.ANY)],
        out_specs=pl.BlockSpec(memory_space=pl.ANY),
        scratch_shapes=[pltpu.SemaphoreType.DMA(())],
        input_output_aliases={2: 0},
        compiler_params=_cparams(("arbitrary",), no_bounds_checks=True),
        name="moe_dispatch",
    )(pos.reshape(steps, 1, ROUTE_TM * TOP_K), h, xs)


def _moe_up_kernel(te_ref, nt_ref, tv_ref, xs_ref, wg_ref, wu_ref, bg_ref, bu_ref, act_ref):
    hk = D_MODEL // 2

    @pl.when(tv_ref[pl.program_id(0)] > 0)
    def _():
        w = xs_ref[...]
        lo = lax.bitcast_convert_type(w << 16, F32).astype(BF16)
        hi = lax.bitcast_convert_type(w & jnp.uint32(0xFFFF0000), F32).astype(BF16)
        wg = wg_ref[0].astype(BF16)
        wu = wu_ref[0].astype(BF16)
        g = _dot(lo, wg[:hk]) + _dot(hi, wg[hk:]) + bg_ref[0]
        u = _dot(lo, wu[:hk]) + _dot(hi, wu[hk:]) + bu_ref[0]
        g = jnp.minimum(g, SWIGLU_LIMIT)
        u = jnp.clip(u, -SWIGLU_LIMIT, SWIGLU_LIMIT)
        act_ref[...] = ((u + 1.0) * g * _sigmoid(SWIGLU_ALPHA * g)).astype(BF16)

    @pl.when(tv_ref[pl.program_id(0)] <= 0)
    def _():
        act_ref[...] = jnp.zeros_like(act_ref)


def _moe_up(tile_e, n_used, tile_valid, xs, w_gu, b_gu3):
    n_rows = xs.shape[0]
    nj = D_FF // MOE_TN
    row_i = lambda i, j, te, nt, tv: (jnp.minimum(i, nt[0] - 1), 0)
    jc = lambda i, j, nt: jnp.where(i < nt[0], j, nj - 1)
    wspec = lambda off: pl.BlockSpec((1, D_MODEL, MOE_TN),
                                     lambda i, j, te, nt, tv: (te[i], 0, jc(i, j, nt) + off))
    bspec = lambda off: pl.BlockSpec((1, 1, MOE_TN),
                                     lambda i, j, te, nt, tv: (te[i], 0, jc(i, j, nt) + off))
    gs = pltpu.PrefetchScalarGridSpec(
        num_scalar_prefetch=3,
        grid=(n_rows // MOE_TM, nj),
        in_specs=[pl.BlockSpec((MOE_TM, D_MODEL // 2), row_i), wspec(0), wspec(nj), bspec(0), bspec(nj)],
        out_specs=pl.BlockSpec((MOE_TM, MOE_TN), lambda i, j, te, nt, tv: (i, j)),
    )
    return pl.pallas_call(
        _moe_up_kernel,
        out_shape=jax.ShapeDtypeStruct((n_rows, D_FF), BF16),
        grid_spec=gs,
        compiler_params=_cparams(("parallel", "arbitrary")),
        name="moe_up",
    )(tile_e, n_used, tile_valid, xs, w_gu, w_gu, b_gu3, b_gu3)


def _moe_down_kernel(te_ref, nt_ref, tv_ref, a_ref, w_ref, b_ref, y_ref):
    @pl.when(tv_ref[pl.program_id(0)] > 0)
    def _():
        y_ref[...] = _dot(a_ref[...], w_ref[0].astype(BF16)) + b_ref[0]

    @pl.when(tv_ref[pl.program_id(0)] <= 0)
    def _():
        y_ref[...] = jnp.zeros_like(y_ref)


def _moe_down(tile_e, n_used, tile_valid, act, w_down, b_down3):
    n_rows = act.shape[0]
    nj = D_MODEL // MOE_TN
    row_i = lambda i, j, te, nt, tv: (jnp.minimum(i, nt[0] - 1), 0)
    jc = lambda i, j, nt: jnp.where(i < nt[0], j, nj - 1)
    gs = pltpu.PrefetchScalarGridSpec(
        num_scalar_prefetch=3,
        grid=(n_rows // MOE_TM, nj),
        in_specs=[pl.BlockSpec((MOE_TM, D_FF), row_i),
                  pl.BlockSpec((1, D_FF, MOE_TN), lambda i, j, te, nt, tv: (te[i], 0, jc(i, j, nt))),
                  pl.BlockSpec((1, 1, MOE_TN), lambda i, j, te, nt, tv: (te[i], 0, jc(i, j, nt)))],
        out_specs=pl.BlockSpec((MOE_TM, MOE_TN), lambda i, j, te, nt, tv: (i, j)),
    )
    return pl.pallas_call(
        _moe_down_kernel,
        out_shape=jax.ShapeDtypeStruct((n_rows, D_MODEL), F32),
        grid_spec=gs,
        compiler_params=_cparams(("parallel", "arbitrary")),
        name="moe_down",
    )(tile_e, n_used, tile_valid, act, w_down, b_down3)


def _combine_kernel(pos_ref, yd_ref, x1_ref, tw_ref, g2_ref, lg_ref, lb_ref, o_ref, buf, sem):
    tm = buf.shape[1]

    def copy(t, k):
        return pltpu.make_async_copy(yd_ref.at[pl.ds(pos_ref[0, 0, t * TOP_K + k], 1), :],
                                     buf.at[k, pl.ds(t, 1), :], sem)

    def start(t, c):
        for k in range(TOP_K):
            copy(t, k).start()
        return c

    def wait(t, c):
        for k in range(TOP_K):
            copy(t, k).wait()
        return c

    lax.fori_loop(0, tm, start, 0, unroll=DMA_UNROLL // TOP_K)
    lax.fori_loop(0, tm, wait, 0, unroll=DMA_UNROLL // TOP_K)
    tw = tw_ref[0]
    moe = buf[0] * tw[:, 0:1]
    for k in range(1, TOP_K):
        moe = moe + buf[k] * tw[:, k:k + 1]
    o_ref[0] = _layernorm(DN_ALPHA * x1_ref[0] + g2_ref[0] * moe, lg_ref[...], lb_ref[...])


def _combine(pos, yd, x1, tw, g2, ln_g, ln_b, tok0):
    b, t, _ = x1.shape
    tm = min(ROUTE_TM, t)
    per = t // tm
    step0 = tok0 // tm
    row = lambda w: pl.BlockSpec((1, tm, w), lambda i, j: (i, j, 0))
    par = pl.BlockSpec((1, D_MODEL), lambda i, j: (0, 0))
    return pl.pallas_call(
        functools.partial(_combine_kernel),
        out_shape=jax.ShapeDtypeStruct((b, t, D_MODEL), F32),
        grid=(b, per),
        in_specs=[pl.BlockSpec((1, 1, tm * TOP_K), lambda i, j: (step0 + i * per + j, 0, 0),
                               memory_space=pltpu.SMEM),
                  pl.BlockSpec(memory_space=pl.ANY),
                  row(D_MODEL), row(128),
                  pl.BlockSpec((1, 1, D_MODEL), lambda i, j: (i, 0, 0)), par, par],
        out_specs=row(D_MODEL),
        scratch_shapes=[pltpu.VMEM((TOP_K, tm, D_MODEL), F32), pltpu.SemaphoreType.DMA(())],
        compiler_params=_cparams(("arbitrary", "arbitrary"), no_bounds_checks=True),
        name="moe_combine",
    )(pos.reshape(-1, 1, tm * TOP_K), yd, x1, tw, g2, ln_g, ln_b)


IN_TM = 256
SSD_Q = 256
OUT_TM = 512


def kernel(x_prompt, x_sample, cache_kv, cache_win, state_conv, state_ssm, page_table,
           c_prompt, c_sample, w_ada, b_ada, w_in, cmp_pe, cmp_w1, cmp_b1, cmp_w2,
           conv_w, conv_b, dt_bias, a_log, d_skip, ssm_norm_g, w_out, ln1_g, ln1_b,
           router_w, router_b, w_gu, b_gu, w_down, b_down, ln2_g, ln2_b):
    bp, tp, _ = x_prompt.shape
    bs, ts, _ = x_sample.shape
    past = page_table.shape[1] * PAGE

    m = _ada(jnp.concatenate([c_prompt, c_sample], axis=0), w_ada[0], b_ada[0])
    mod = [m[:, None, k * D_MODEL:(k + 1) * D_MODEL] for k in range(6)]
    mod_p = [a[:bp] for a in mod]
    mod_s = [a[bp:] for a in mod]

    perm = _q_perm()
    inv_perm = np.argsort(perm)
    w_bf = _prep_w_in(w_in[0])
    cw = _prep_cmp_weights(cmp_pe[0], cmp_w1[0], cmp_b1[0], cmp_w2[0])
    sp = _prep_ssm_params(conv_w[0], conv_b[0], dt_bias[0], a_log[0], d_skip[0], ssm_norm_g[0])
    w_out_bf = jnp.concatenate([w_out[0][:D_ATT][perm], w_out[0][D_ATT:]], axis=0).astype(BF16)
    rw = jnp.concatenate([router_w[0], jnp.zeros((D_MODEL, 128 - N_EXPERTS), F32)], axis=1)
    rw_hi = rw.astype(BF16)
    rw = jnp.stack([rw_hi, (rw - rw_hi.astype(F32)).astype(BF16)])
    rb = jnp.concatenate([router_b[0], jnp.full((128 - N_EXPERTS,), NEG, F32)]).reshape(1, 128)
    ln1 = (ln1_g[0].reshape(1, D_MODEL), ln1_b[0].reshape(1, D_MODEL))
    ln2 = (ln2_g[0].reshape(1, D_MODEL), ln2_b[0].reshape(1, D_MODEL))

    tab_p = _rope_tables(jnp.arange(tp))
    qn, qr, kv_p, win_p, kvb, z, xbc, gd = _in_proj(x_prompt, mod_p[1], mod_p[0], w_bf, tab_p, IN_TM)
    kc, vc = _compress_prompt(kv_p, cw)
    o_att = _attn_prompt(qn, qr, kc, vc, kvb, gd)
    y_ssm, ssm_p = _ssd(xbc, z, gd, jnp.zeros((bp, 8, CONV_CH), F32),
                        jnp.zeros((bp, D_SSM, D_STATE), F32), sp, SSD_Q)
    x1_p, h2_p, ti_p, tw_p = _out_proj(o_att, y_ssm, x_prompt, mod_p[2], mod_p[4], mod_p[3],
                                       w_out_bf, *ln1, rw, rb, OUT_TM)

    tab_s = _rope_tables(past + jnp.arange(ts))
    n_s = bs * ts
    per_tok = lambda a: jnp.broadcast_to(a, (bs, ts, D_MODEL)).reshape(1, n_s, D_MODEL)
    s_outs = _in_proj(x_sample.reshape(1, n_s, D_MODEL), per_tok(mod_s[1]), per_tok(mod_s[0]),
                      w_bf, jnp.tile(tab_s, (bs, 1)), n_s)
    qn_s, qr_s, kv_s, win_s, _, z_s, xbc_s, gd_s = [a.reshape(bs, ts, a.shape[-1]) for a in s_outs]
    cache_t = jnp.transpose(cache_kv[0], (0, 2, 3, 4, 1)).reshape(
        cache_kv.shape[1], 4 * N_KV * HEAD_DIM, PAGE)
    kc_s, vc_s = _compress_sample(cache_t, page_table, cw)
    wb = cache_win.shape[2]
    cwin2 = cache_win[0].reshape(bs, wb, 2 * N_KV * HEAD_DIM)
    gt_rows = gd_s[:, :, :D_GT].reshape(bs, ts, N_KV, HPG, 3).transpose(0, 2, 1, 3, 4)
    gt_rows = jnp.concatenate([gt_rows.reshape(bs, N_ROWS_S, 3),
                               jnp.zeros((bs, N_ROWS_S, 125), F32)], axis=-1)
    o_rows = _attn_sample(cache_t, page_table, _expand_rows(qn_s[:, :, inv_perm]),
                          _expand_rows(qr_s[:, :, inv_perm]), kc_s, vc_s, kv_s, cwin2, win_s,
                          gt_rows, ts)
    o_att_s = _collapse_rows(o_rows, ts)[:, :, perm].astype(BF16)
    cprev = jnp.concatenate([jnp.zeros((bs, 5, CONV_CH), F32), state_conv[0]], axis=1)
    y_ssm_s, ssm_s = _ssd(xbc_s, z_s, gd_s, cprev, state_ssm[0].reshape(bs, D_SSM, D_STATE), sp, ts)
    x1_s, h2_s, ti_s, tw_s = _out_proj(o_att_s, y_ssm_s, x_sample, mod_s[2], mod_s[4], mod_s[3],
                                       w_out_bf, *ln1, rw, rb, ts)

    n_tok = bp * tp + bs * ts
    n_tiles = -(-(n_tok * TOP_K + N_EXPERTS * (MOE_TM - 1)) // MOE_TM)
    ti_all = jnp.concatenate([ti_p.reshape(-1, 128), ti_s.reshape(-1, 128)], axis=0)
    rank, cnt = _route(ti_all)
    counts = cnt[0, :N_EXPERTS].astype(jnp.int32)
    padded = (counts + MOE_TM - 1) // MOE_TM * MOE_TM
    ends = jnp.cumsum(padded)
    offs = ends - padded
    n_used = (ends[-1] // MOE_TM).astype(jnp.int32).reshape(1)
    tiles = jnp.minimum(jnp.arange(n_tiles, dtype=jnp.int32), n_used[0] - 1)
    tile_e = jnp.sum((tiles[:, None] * MOE_TM >= ends[None, :]).astype(jnp.int32), axis=1)
    tile_e = jnp.minimum(tile_e, N_EXPERTS - 1).astype(jnp.int32)
    first_tile = offs // MOE_TM
    tile_valid = jnp.clip(counts[tile_e] - (tiles - first_tile[tile_e]) * MOE_TM, 0, MOE_TM)
    tile_valid = jnp.where(jnp.arange(n_tiles) < n_used[0], tile_valid, 0).astype(jnp.int32)
    pos = (offs[ti_all[:, :TOP_K]] + rank[:, :TOP_K]).astype(jnp.int32)
    n_p = bp * tp
    xs = jnp.zeros((n_tiles * MOE_TM, D_MODEL // 2), jnp.uint32)
    xs = _dispatch(pos[:n_p], h2_p.reshape(n_p, D_MODEL // 2), xs)
    xs = _dispatch(pos[n_p:], h2_s.reshape(bs * ts, D_MODEL // 2), xs)
    act = _moe_up(tile_e, n_used, tile_valid, xs, w_gu[0], b_gu[0].reshape(N_EXPERTS, 1, 2 * D_FF))
    yd = _moe_down(tile_e, n_used, tile_valid, act, w_down[0],
                   b_down[0].reshape(N_EXPERTS, 1, D_MODEL))
    y_p = _combine(pos, yd, x1_p, tw_p, mod_p[5], *ln2, 0)
    y_s = _combine(pos, yd, x1_s, tw_s, mod_s[5], *ln2, bp * tp)

    kv_shape = (4, N_KV, HEAD_DIM)
    win_shape = (2, N_KV, HEAD_DIM)
    win_prompt = win_p[:, tp - min(WINDOW, tp):].reshape((1, bp, min(WINDOW, tp)) + win_shape)
    win_sample = jnp.concatenate([cwin2, win_s], axis=1)[:, -wb:].reshape((1, bs, wb) + win_shape)
    return (y_p, y_s,
            kv_p.reshape((1, bp, tp) + kv_shape), kv_s.reshape((1, bs, ts) + kv_shape),
            win_prompt, win_sample,
            xbc[None, :, tp - (CONV_W - 1):], xbc_s[None, :, ts - (CONV_W - 1):],
            ssm_p.reshape(1, bp, N_HEADS_SSM, HEAD_DIM, D_STATE),
            ssm_s.reshape(1, bs, N_HEADS_SSM, HEAD_DIM, D_STATE))
```

```python
import functools
import math

import jax
import jax.numpy as jnp
import numpy as np
from jax import lax
from jax.experimental import pallas as pl
from jax.experimental.pallas import tpu as pltpu

D_MODEL = 2048
D_ATT = 1024
D_SSM = 1024
HEAD_DIM = 64
N_HEADS_ATT = 16
N_KV = 4
HPG = 4
ROT_DIM = 16
ROPE_THETA = 500000.0
CMP_LEN = 32
CMP_STRIDE = 16
CMP_HID = 128
SLC_BLK = 64
N_SELECT = 16
WINDOW = 512
N_HEADS_SSM = 16
SSM_GROUPS = 4
D_STATE = 128
CONV_W = 4
CONV_CH = 2048
N_EXPERTS = 32
TOP_K = 4
D_FF = 2048
SWIGLU_LIMIT = 7.0
SWIGLU_ALPHA = 1.702
DN_ALPHA = 2.0 ** 0.25
LN_EPS = 1e-5
RMS_EPS = 1e-5
NEG = -1e30
FORCE = 1e6
PAGE = 128

D_KV = 6 * N_KV * HEAD_DIM
D_GT = 3 * N_HEADS_ATT
D_IN = D_ATT + D_KV + D_GT + D_SSM + CONV_CH + N_HEADS_SSM
D_INP = D_ATT + D_KV + D_SSM + CONV_CH + 128

LANES = 128
VMEM_LIMIT = 56 * 1024 * 1024

F32 = jnp.float32
BF16 = jnp.bfloat16
HI = lax.Precision.HIGHEST


def _cparams(sem, vmem=VMEM_LIMIT, no_bounds_checks=False):
    return pltpu.CompilerParams(dimension_semantics=sem, vmem_limit_bytes=vmem,
                                disable_bounds_checks=no_bounds_checks)


def _dot(a, b, precision=None):
    return jnp.dot(a, b, preferred_element_type=F32, precision=precision)


def _dot_nt(a, b, precision=None):
    return lax.dot_general(a, b, (((1,), (1,)), ((), ())), preferred_element_type=F32,
                           precision=precision)


def _dot_tn(a, b, precision=None):
    return lax.dot_general(a, b, (((0,), (0,)), ((), ())), preferred_element_type=F32,
                           precision=precision)


def _split3(x):
    x1 = x.astype(BF16)
    r = x - x1.astype(F32)
    x2 = r.astype(BF16)
    return x1, x2, (r - x2.astype(F32)).astype(BF16)


def _dot_sel_l(sel, x):
    s = sel.astype(BF16)
    x1, x2, x3 = _split3(x)
    return _dot(s, x1) + (_dot(s, x2) + _dot(s, x3))


def _dot_sel_r(x, sel):
    s = sel.astype(BF16)
    x1, x2, x3 = _split3(x)
    return _dot(x1, s) + (_dot(x2, s) + _dot(x3, s))


def _pack_pairs(x):
    h = x.shape[1] // 2
    bits = lax.bitcast_convert_type(x.astype(BF16).astype(F32), jnp.uint32)
    return (bits[:, :h] >> 16) | (bits[:, h:] & jnp.uint32(0xFFFF0000))


def _unpack_pairs(w):
    lo = lax.bitcast_convert_type(w << 16, F32)
    hi = lax.bitcast_convert_type(w & jnp.uint32(0xFFFF0000), F32)
    return lo, hi


def _sigmoid(x):
    return 1.0 / (1.0 + jnp.exp(-x))


def _silu(x):
    return x * _sigmoid(x)


def _ada_kernel(c_ref, w_ref, b_ref, o_ref):
    c = c_ref[...]
    a = _silu(c).astype(BF16)
    o_ref[...] = _dot(a, w_ref[...].astype(BF16)) + b_ref[...]


def _ada(c_all, w_ada, b_ada):
    nb = c_all.shape[0]
    tn = 1024
    n = w_ada.shape[1]
    return pl.pallas_call(
        _ada_kernel,
        out_shape=jax.ShapeDtypeStruct((nb, n), F32),
        grid=(n // tn,),
        in_specs=[pl.BlockSpec((nb, D_MODEL), lambda j: (0, 0)),
                  pl.BlockSpec((D_MODEL, tn), lambda j: (0, j)),
                  pl.BlockSpec((1, tn), lambda j: (0, j))],
        out_specs=pl.BlockSpec((nb, tn), lambda j: (0, j)),
        compiler_params=_cparams(("parallel",)),
        name="ada",
    )(c_all, w_ada, b_ada.reshape(1, n))


def _q_perm():
    idx = np.zeros(D_ATT, np.int32)
    for gp in range(2):
        for j in range(HPG):
            for side in range(2):
                g = 2 * gp + side
                for d in range(HEAD_DIM):
                    idx[gp * 512 + j * 128 + side * 64 + d] = g * 256 + j * 64 + d
    return idx


def _prep_w_in(w_in):
    o_kv = D_ATT
    o_gt = o_kv + D_KV
    o_z = o_gt + D_GT
    o_xbc = o_z + D_SSM
    o_dt = o_xbc + CONV_CH
    wq = w_in[:, :D_ATT][:, _q_perm()]
    pad = jnp.zeros((D_MODEL, 128 - D_GT - N_HEADS_SSM), w_in.dtype)
    w = jnp.concatenate([wq, w_in[:, o_kv:o_gt], w_in[:, o_z:o_xbc], w_in[:, o_xbc:o_dt],
                         w_in[:, o_gt:o_z], w_in[:, o_dt:], pad], axis=1)
    return w.astype(BF16)


def _rope_tables(pos):
    half = ROT_DIM // 2
    inv = ROPE_THETA ** (-jnp.arange(half, dtype=F32) / half)
    ang = pos.astype(F32)[:, None] * inv
    cos, sin = jnp.cos(ang), jnp.sin(ang)
    t = pos.shape[0]
    one = jnp.ones((t, HEAD_DIM - ROT_DIM), F32)
    zero = jnp.zeros((t, HEAD_DIM - ROT_DIM), F32)
    z8 = jnp.zeros((t, half), F32)
    c64 = jnp.concatenate([cos, cos, one], 1)
    s1 = jnp.concatenate([z8, sin, zero], 1)
    s2 = jnp.concatenate([-sin, z8, zero], 1)
    rep = lambda a: jnp.tile(a, (1, 4))
    return jnp.concatenate([rep(c64), rep(s1), rep(s2)], axis=1)


def _rope256(x, tab):
    n = x.shape[1]
    return (x * tab[:, 0:256] + pltpu.roll(x, 8, 1) * tab[:, 256:512]
            + pltpu.roll(x, n - 8, 1) * tab[:, 512:768])


def _inproj_kernel(x_ref, sc_ref, sh_ref, w_ref, tab_ref,
                   qn_ref, qr_ref, kv_ref, win_ref, kvb_ref, z_ref, xbc_ref, gd_ref):
    h = (x_ref[0] * (1.0 + sc_ref[0]) + sh_ref[0]).astype(BF16)
    tab = tab_ref[...]
    for c in range(4):
        q = _dot(h, w_ref[:, c * 256:(c + 1) * 256])
        qn_ref[0, :, c * 256:(c + 1) * 256] = q.astype(BF16)
        qr_ref[0, :, c * 256:(c + 1) * 256] = _rope256(q, tab).astype(BF16)
    o = D_ATT
    for p in range(6):
        y = _dot(h, w_ref[:, o + p * 256:o + (p + 1) * 256])
        if p in (2, 4):
            y = _rope256(y, tab)
        if p < 4:
            kv_ref[0, :, p * 256:(p + 1) * 256] = y
        else:
            win_ref[0, :, (p - 4) * 256:(p - 3) * 256] = y
        if p >= 2:
            kvb_ref[0, :, (p - 2) * 256:(p - 1) * 256] = y.astype(BF16)
    o += D_KV
    for c in range(2):
        z_ref[0, :, c * 512:(c + 1) * 512] = _dot(h, w_ref[:, o + c * 512:o + (c + 1) * 512]).astype(BF16)
    o += D_SSM
    for c in range(4):
        xbc_ref[0, :, c * 512:(c + 1) * 512] = _dot(h, w_ref[:, o + c * 512:o + (c + 1) * 512])
    o += CONV_CH
    gd_ref[0] = _dot(h, w_ref[:, o:o + 128])


def _in_proj(x, sc, sh, w_bf, tab, tm):
    b, t, _ = x.shape
    nt = t // tm
    row = lambda w: pl.BlockSpec((1, tm, w), lambda i, j: (i, j, 0))
    mod = row(D_MODEL) if sc.shape[1] == t else pl.BlockSpec((1, 1, D_MODEL), lambda i, j: (i, 0, 0))
    outs = [(D_ATT, BF16), (D_ATT, BF16), (1024, F32), (512, F32), (1024, BF16),
            (D_SSM, BF16), (CONV_CH, F32), (128, F32)]
    return pl.pallas_call(
        _inproj_kernel,
        out_shape=[jax.ShapeDtypeStruct((b, t, w), dt) for w, dt in outs],
        grid=(b, nt),
        in_specs=[row(D_MODEL), mod, mod,
                  pl.BlockSpec((D_MODEL, D_INP), lambda i, j: (0, 0), pipeline_mode=pl.Buffered(1)),
                  pl.BlockSpec((tm, 768), lambda i, j: (j, 0))],
        out_specs=[row(w) for w, _ in outs],
        compiler_params=_cparams(("parallel", "parallel")),
        name="in_proj",
    )(x, sc, sh, w_bf, tab)


def _prep_cmp_weights(cmp_pe, cmp_w1, cmp_b1, cmp_w2):
    w1 = cmp_w1.reshape(2, 2, CMP_STRIDE, HEAD_DIM, CMP_HID)
    eye = jnp.eye(2, dtype=cmp_w1.dtype)
    w1p = jnp.einsum('phsdj,ab->psadhbj', w1, eye).reshape(2, CMP_STRIDE * 128, 4 * CMP_HID)
    w2p = jnp.einsum('pjd,ab->pajbd', cmp_w2, eye).reshape(2, 2 * CMP_HID, 2 * HEAD_DIM)
    pe = cmp_pe.reshape(2, 1, CMP_LEN * HEAD_DIM)
    pe = jnp.concatenate([pe, jnp.zeros((2, 7, CMP_LEN * HEAD_DIM), pe.dtype)], axis=1)
    b1 = jnp.concatenate([cmp_b1, cmp_b1], axis=-1).reshape(2, 1, 2 * CMP_HID)
    return w1p.astype(BF16), w2p.astype(BF16), pe, cmp_w1, b1


def _gelu_tanh(x):
    return 0.5 * x * (1.0 + jnp.tanh(0.7978845608028654 * (x + 0.044715 * x * x * x)))


def _compress_core(load, nrows, w1p_ref, w2p_ref, pe_ref, w1_ref, b1_ref, store, prepare=None):
    for p in range(2):
        pe_bias = _dot(pe_ref[p], w1_ref[p], precision=HI)[0:1]
        pe_bias = jnp.concatenate([pe_bias, pe_bias], axis=1)
        for pr in range(2):
            if prepare is not None:
                prepare(2 * p + pr)
            xs = jnp.concatenate([load(s, 2 * p + pr).astype(BF16) for s in range(CMP_STRIDE)], axis=1)
            ab = _dot(xs, w1p_ref[p])
            a, bb = ab[:, 0:256], ab[:, 256:512]
            pre = a + pltpu.roll(bb, nrows - 1, 0) + pe_bias + b1_ref[p]
            hid = _gelu_tanh(pre).astype(BF16)
            store(p, pr, _dot(hid, w2p_ref[p]))


def _compress_prompt_kernel(b0, b1, b2, b3, w1p_ref, w2p_ref, pe_ref, w1_ref, b1_ref, kc_ref, vc_ref):
    blocks = (b0, b1, b2, b3)
    n = b0.shape[1] // CMP_STRIDE
    outs = (kc_ref, vc_ref)

    def load(s, lb):
        return blocks[lb][0, pl.ds(s, n, stride=CMP_STRIDE), :]

    def store(p, pr, val):
        outs[p][0, :, pr * 128:(pr + 1) * 128] = val.astype(BF16)

    _compress_core(load, n, w1p_ref, w2p_ref, pe_ref, w1_ref, b1_ref, store)


def _wspecs():
    z3 = lambda *a: (0, 0, 0)
    return [pl.BlockSpec((2, CMP_STRIDE * 128, 4 * CMP_HID), z3),
            pl.BlockSpec((2, 2 * CMP_HID, 2 * HEAD_DIM), z3),
            pl.BlockSpec((2, 8, CMP_LEN * HEAD_DIM), z3),
            pl.BlockSpec((2, CMP_LEN * HEAD_DIM, CMP_HID), z3),
            pl.BlockSpec((2, 1, 2 * CMP_HID), z3)]


def _compress_prompt(kv_f32, cw):
    b, t, _ = kv_f32.shape
    n = t // CMP_STRIDE
    lane_blk = lambda lb: pl.BlockSpec((1, t, 128), lambda i: (i, 0, lb))
    return pl.pallas_call(
        _compress_prompt_kernel,
        out_shape=[jax.ShapeDtypeStruct((b, n, 256), BF16)] * 2,
        grid=(b,),
        in_specs=[lane_blk(lb) for lb in range(4)] + _wspecs(),
        out_specs=[pl.BlockSpec((1, n, 256), lambda i: (i, 0, 0))] * 2,
        compiler_params=_cparams(("parallel",)),
        name="compress_prompt",
    )(kv_f32, kv_f32, kv_f32, kv_f32, *cw)


CMP_PAGES = 32


def _compress_sample_kernel(pt_ref, *refs):
    npg = CMP_PAGES + 1
    pages = refs[:npg]
    w1p_ref, w2p_ref, pe_ref, w1_ref, b1_ref, kc_ref, vc_ref, tok_ref = refs[npg:]
    cpp = PAGE // CMP_STRIDE
    nrows = CMP_PAGES * cpp
    outs = (kc_ref, vc_ref)
    def prepare(lb):
        for k in range(npg):
            tok_ref[lb, k * PAGE:(k + 1) * PAGE, :] = pages[k][0, lb * 128:(lb + 1) * 128, :].T

    def load(s, lb):
        return tok_ref[lb, pl.ds(s, nrows + cpp, stride=CMP_STRIDE), :]

    def store(p, pr, val):
        outs[p][0, :, pr * 128:(pr + 1) * 128] = val[:nrows].astype(BF16)

    _compress_core(load, nrows + cpp, w1p_ref, w2p_ref, pe_ref, w1_ref, b1_ref, store, prepare)


def _compress_sample(cache_t, page_table, cw):
    b, n_pages = page_table.shape
    steps = n_pages // CMP_PAGES
    nrows = CMP_PAGES * (PAGE // CMP_STRIDE)

    def pspec(k):
        return pl.BlockSpec(
            (1, 512, PAGE),
            lambda i, j, pt: (pt[i, jnp.minimum(j * CMP_PAGES + k, n_pages - 1)], 0, 0))

    gs = pltpu.PrefetchScalarGridSpec(
        num_scalar_prefetch=1,
        grid=(b, steps),
        in_specs=[pspec(k) for k in range(CMP_PAGES + 1)]
                 + [pl.BlockSpec(s.block_shape, lambda i, j, pt: (0, 0, 0)) for s in _wspecs()],
        out_specs=[pl.BlockSpec((1, nrows, 256), lambda i, j, pt: (i, j, 0))] * 2,
        scratch_shapes=[pltpu.VMEM((4, (CMP_PAGES + 1) * PAGE, 128), F32)],
    )
    return pl.pallas_call(
        _compress_sample_kernel,
        out_shape=[jax.ShapeDtypeStruct((b, n_pages * 8, 256), BF16)] * 2,
        grid_spec=gs,
        compiler_params=_cparams(("parallel", "parallel")),
        name="compress_sample",
    )(page_table, *([cache_t] * (CMP_PAGES + 1)), *cw)


DT_LANE = D_GT


def _prep_ssm_params(conv_w, conv_b, dt_bias, a_log, d_skip, ssm_norm_g):
    cw = jnp.concatenate([conv_w, jnp.zeros((8 - CONV_W, CONV_CH), conv_w.dtype)], axis=0)
    lane = lambda v: jnp.zeros((1, 128), F32).at[0, DT_LANE:DT_LANE + N_HEADS_SSM].set(v)
    dsk = jnp.repeat(d_skip, HEAD_DIM).reshape(1, D_SSM)
    return (cw, conv_b.reshape(1, CONV_CH), lane(dt_bias), lane(a_log), dsk,
            ssm_norm_g.reshape(1, D_SSM))


def _ssd_kernel(xbc_ref, z_ref, gd_ref, cprev_ref, h0_ref, cw_ref, cb_ref, dtb_ref, alog_ref,
                dsk_ref, ng_ref, y_ref, st_ref, xpad_ref):
    q = xbc_ref.shape[1]
    c = pl.program_id(1)

    @pl.when(c == 0)
    def _():
        st_ref[0] = h0_ref[0]
        xpad_ref[0:8, :] = cprev_ref[0]

    xpad_ref[8:8 + q, :] = xbc_ref[0]
    conv = cb_ref[...] + xpad_ref[pl.ds(5, q), :] * cw_ref[0:1, :]
    for k in range(1, CONV_W):
        conv = conv + xpad_ref[pl.ds(5 + k, q), :] * cw_ref[k:k + 1, :]
    xpad_ref[0:8, :] = xpad_ref[q:q + 8, :]
    xc = _silu(conv)
    xs = xc[:, 0:D_SSM]

    lane = lax.broadcasted_iota(jnp.int32, (1, 128), 1)
    in_dt = (lane >= DT_LANE) & (lane < DT_LANE + N_HEADS_SSM)
    v = gd_ref[0] + dtb_ref[...]
    dt = jnp.maximum(v, 0.0) + jnp.log1p(jnp.exp(-jnp.abs(v)))
    a = jnp.where(in_dt, -jnp.exp(alog_ref[...]), 0.0) * dt
    er = lax.broadcasted_iota(jnp.int32, (128, D_SSM), 0)
    ec = lax.broadcasted_iota(jnp.int32, (128, D_SSM), 1)
    expand = er - DT_LANE == ec // HEAD_DIM
    ri = lax.broadcasted_iota(jnp.int32, (q, q), 0)
    ci = lax.broadcasted_iota(jnp.int32, (q, q), 1)
    causal = ri >= ci
    acs = _dot_sel_l(causal, a)
    acs_t = acs.T
    dt_x = _dot_sel_r(jnp.where(in_dt, dt, 0.0), expand)
    acs_x = _dot_sel_r(acs, expand)
    last_x = acs_x[q - 1:q, :]
    grow_x = jnp.exp(acs_x)
    decay_x = jnp.exp(last_x - acs_x)
    er2 = lax.broadcasted_iota(jnp.int32, (D_SSM, 128), 0)
    ec2 = lax.broadcasted_iota(jnp.int32, (D_SSM, 128), 1)
    expand_t = ec2 - DT_LANE == er2 // HEAD_DIM
    tot_col = jnp.exp(_dot_sel_l(expand_t, acs_t)[:, q - 1:q])

    xd = xs * dt_x
    xdd = (xd * decay_x).astype(BF16)
    xd_b = xd.astype(BF16)
    half = lax.broadcasted_iota(jnp.int32, (1, 128), 1) < HEAD_DIM
    ys = []
    for g in range(SSM_GROUPS):
        bg = xc[:, D_SSM + g * D_STATE:D_SSM + (g + 1) * D_STATE].astype(BF16)
        cg = xc[:, D_SSM + (SSM_GROUPS + g) * D_STATE:D_SSM + (SSM_GROUPS + g + 1) * D_STATE].astype(BF16)
        cbm = _dot_nt(cg, bg)
        for m in (2 * g, 2 * g + 1):
            sl = slice(128 * m, 128 * (m + 1))
            yh = []
            for hh in (2 * m, 2 * m + 1):
                col = acs[:, DT_LANE + hh:DT_LANE + hh + 1]
                row = acs_t[DT_LANE + hh:DT_LANE + hh + 1, :]
                lm = jnp.where(causal, jnp.exp(jnp.where(causal, col - row, 0.0)), 0.0)
                yh.append(_dot((cbm * lm).astype(BF16), xd_b[:, sl]))
            y_diag = jnp.where(half, yh[0], yh[1])
            st = st_ref[0, sl, :]
            y_off = _dot_nt(cg, st.astype(BF16)) * grow_x[:, sl]
            st_ref[0, sl, :] = st * tot_col[sl, :] + _dot_tn(xdd[:, sl], bg)
            ys.append(y_diag + y_off)
    y = jnp.concatenate(ys, axis=1) + dsk_ref[...] * xs
    y = y * _silu(z_ref[0].astype(F32))
    gw = D_SSM // SSM_GROUPS
    outs = []
    for g in range(SSM_GROUPS):
        blk = y[:, g * gw:(g + 1) * gw]
        ms = jnp.mean(blk * blk, axis=1, keepdims=True)
        outs.append(blk * lax.rsqrt(ms + RMS_EPS))
    y_ref[0] = (jnp.concatenate(outs, axis=1) * ng_ref[...]).astype(BF16)


def _ssd(xbc, z, gd, conv_prev8, h0, sp, q):
    b, t, _ = xbc.shape
    nc = t // q
    row = lambda w: pl.BlockSpec((1, q, w), lambda i, j: (i, j, 0))
    per_b = lambda r, w: pl.BlockSpec((1, r, w), lambda i, j: (i, 0, 0))
    par = lambda r, w: pl.BlockSpec((r, w), lambda i, j: (0, 0))
    return pl.pallas_call(
        _ssd_kernel,
        out_shape=[jax.ShapeDtypeStruct((b, t, D_SSM), BF16),
                   jax.ShapeDtypeStruct((b, D_SSM, D_STATE), F32)],
        grid=(b, nc),
        in_specs=[row(CONV_CH), row(D_SSM), row(128), per_b(8, CONV_CH), per_b(D_SSM, D_STATE),
                  par(8, CONV_CH), par(1, CONV_CH), par(1, 128), par(1, 128), par(1, D_SSM),
                  par(1, D_SSM)],
        out_specs=[row(D_SSM), per_b(D_SSM, D_STATE)],
        scratch_shapes=[pltpu.VMEM((q + 8, CONV_CH), F32)],
        compiler_params=_cparams(("parallel", "arbitrary")),
        name="ssd",
    )(xbc, z, gd, conv_prev8, h0, *sp)


ATT_TQ = 512
SCALE = HEAD_DIM ** -0.5


def _select_blocks_t(ps, n_blk):
    blk = lax.broadcasted_iota(jnp.int32, ps.shape, 0)
    rank = jnp.zeros(ps.shape, F32)
    for i in range(n_blk):
        vi = ps[i:i + 1, :]
        rank = rank + jnp.where(vi > ps, 1.0, jnp.where((vi == ps) & (blk > i), 1.0, 0.0))
    return jnp.where(rank < N_SELECT, 1.0, 0.0)


def _attn_prompt_kernel(qn_ref, qr_ref, kc_ref, vc_ref, ks_ref, vs_ref, kw_ref, vw_ref, gd_ref,
                        o_ref, qz_ref, sel_ref, ocmp_ref, m_ref, acc_ref):
    tq = qn_ref.shape[1]
    t_all = ks_ref.shape[1]
    n_cmp = t_all // CMP_STRIDE - 1
    n_blk = t_all // SLC_BLK
    gp = pl.program_id(1)
    qi = pl.program_id(2)
    rep = lambda a: jnp.concatenate([a] * HPG, axis=1)
    pos = qi * tq + lax.broadcasted_iota(jnp.int32, (1, tq), 1)
    pos4 = rep(pos)
    lane = lax.broadcasted_iota(jnp.int32, (1, 128), 1)
    row = lax.broadcasted_iota(jnp.int32, (128, 1), 0)
    lane_side = (lane < HEAD_DIM, lane >= HEAD_DIM)
    row_side = (row < HEAD_DIM, row >= HEAD_DIM)

    for r, ref in enumerate((qn_ref, qr_ref)):
        for side in range(2):
            for j in range(HPG):
                col = ref[0, :, j * 128:(j + 1) * 128]
                col = jnp.where(lane_side[side], col, jnp.zeros_like(col))
                qz_ref[2 * r + side, j * tq:(j + 1) * tq, :] = col * SCALE

    def v_sides(v):
        vt = v.astype(F32).T
        return [jnp.where(row_side[s], vt, 1.0).astype(BF16) for s in range(2)]

    kc = kc_ref[0]
    vct = vc_ref[0].astype(F32).T.astype(BF16)
    vis = (CMP_STRIDE * row + (CMP_LEN - 1) <= pos4) & (row < n_cmp)
    br = lax.broadcasted_iota(jnp.int32, (128, 128), 0) * SLC_BLK
    cc = lax.broadcasted_iota(jnp.int32, (128, 128), 1) * CMP_STRIDE
    overlap_t = ((cc < br + SLC_BLK) & (cc + CMP_LEN > br)).astype(F32)
    cur = pos // SLC_BLK
    forced = (row == 0) | (row == cur) | (row == cur - 1)
    for side in range(2):
        s = jnp.where(vis, _dot_nt(kc, qz_ref[side]), NEG)
        e = jnp.exp(s - jnp.max(s, axis=0, keepdims=True))
        p = jnp.where(vis, e / jnp.sum(e, axis=0, keepdims=True), 0.0)
        ocmp_ref[side] = _dot(vct, p.astype(BF16))
        psum = p[:, 0:tq]
        for j in range(1, HPG):
            psum = psum + p[:, j * tq:(j + 1) * tq]
        ps = _dot(overlap_t, psum, precision=HI)
        ps = jnp.where(row <= cur, jnp.where(forced, FORCE, ps), NEG)
        sel = _select_blocks_t(ps[0:n_blk, :], n_blk)
        sel_ref[side] = jnp.concatenate([sel, jnp.zeros((128 - n_blk, tq), F32)], axis=0).astype(BF16)

    m_ref[...] = jnp.full(m_ref.shape, NEG, F32)
    acc_ref[...] = jnp.zeros(acc_ref.shape, F32)

    def update(idx, s, vts):
        m_old = m_ref[idx]
        m_new = jnp.maximum(m_old, jnp.max(s, axis=0, keepdims=True))
        p = jnp.exp(s - m_new).astype(BF16)
        acc_ref[idx] = jnp.exp(m_old - m_new) * acc_ref[idx] + _dot(vts, p)
        m_ref[idx] = m_new

    def slc_tile(t, diagonal):
        k0 = pl.multiple_of(t * tq, tq)
        k = ks_ref[0, pl.ds(k0, tq), :]
        vts = v_sides(vs_ref[0, pl.ds(k0, tq), :])
        kpos = k0 + lax.broadcasted_iota(jnp.int32, (tq, 1), 0)
        expand_t = (lax.broadcasted_iota(jnp.int32, (tq, 128), 1) == kpos // SLC_BLK).astype(BF16)
        for side in range(2):
            keep = _dot(expand_t, sel_ref[side])
            if diagonal:
                keep = jnp.where(kpos <= pos, keep, 0.0)
            bias = (keep - 1.0) * (-NEG)
            update(side, _dot_nt(k, qz_ref[2 + side]) + rep(bias), vts[side])

    def slc_body(t, carry):
        slc_tile(t, False)
        return carry

    lax.fori_loop(0, qi, slc_body, 0)
    slc_tile(qi, True)

    def win_tile(t, kind):
        k0 = pl.multiple_of(t * tq, tq)
        k = kw_ref[0, pl.ds(k0, tq), :]
        vts = v_sides(vw_ref[0, pl.ds(k0, tq), :])
        kpos = k0 + lax.broadcasted_iota(jnp.int32, (tq, 1), 0)
        mask = (kpos > pos - WINDOW) if kind == 0 else ((kpos <= pos) if kind == 2 else None)
        bias = None if mask is None else rep(jnp.where(mask, 0.0, NEG))
        for side in range(2):
            s = _dot_nt(k, qz_ref[2 + side])
            if bias is not None:
                s = s + bias
            update(2 + side, s, vts[side])

    n_back = WINDOW // tq
    for d in range(n_back, 0, -1):
        pl.when(qi >= d)(functools.partial(win_tile, qi - d, 0 if d == n_back else 1))
    win_tile(qi, 2)

    gates_t = _sigmoid(gd_ref[0]).T
    for j in range(HPG):
        sl = slice(j * tq, (j + 1) * tq)
        cols = []
        for side in range(2):
            lrow = HEAD_DIM if side == 0 else 0
            gi = (2 * gp + side) * (3 * HPG) + 3 * j
            g = [jnp.sum(jnp.where(row == gi + k, gates_t, 0.0), axis=0, keepdims=True)
                 for k in range(3)]
            a_s = acc_ref[side, :, sl]
            a_w = acc_ref[2 + side, :, sl]
            cols.append(g[0] * ocmp_ref[side, :, sl] + (g[1] / a_s[lrow:lrow + 1, :]) * a_s
                        + (g[2] / a_w[lrow:lrow + 1, :]) * a_w)
        o_ref[0, :, j * 128:(j + 1) * 128] = jnp.where(row_side[0], cols[0], cols[1]).T.astype(BF16)


def _attn_prompt(qn, qr, kc, vc, kvb, gd):
    b, t, _ = qn.shape
    tq = ATT_TQ
    assert WINDOW % tq == 0 and t % tq == 0 and t // SLC_BLK <= 128 and kc.shape[1] == 128
    tw = HPG * tq
    qspec = pl.BlockSpec((1, tq, 512), lambda i, g, q: (i, q, g))
    cspec = pl.BlockSpec((1, kc.shape[1], 128), lambda i, g, q: (i, 0, g))
    kvspec = lambda base: pl.BlockSpec((1, t, 128), lambda i, g, q: (i, 0, base + g))
    return pl.pallas_call(
        _attn_prompt_kernel,
        out_shape=jax.ShapeDtypeStruct((b, t, D_ATT), BF16),
        grid=(b, 2, t // tq),
        in_specs=[qspec, qspec, cspec, cspec, kvspec(0), kvspec(2), kvspec(4), kvspec(6),
                  pl.BlockSpec((1, tq, 128), lambda i, g, q: (i, q, 0))],
        out_specs=qspec,
        scratch_shapes=[pltpu.VMEM((4, tw, 128), BF16), pltpu.VMEM((2, 128, tq), BF16),
                        pltpu.VMEM((2, 128, tw), F32), pltpu.VMEM((4, 1, tw), F32),
                        pltpu.VMEM((4, 128, tw), F32)],
        compiler_params=_cparams(("parallel", "parallel", "arbitrary")),
        name="attn_prompt",
    )(qn, qr, kc, vc, kvb, kvb, kvb, kvb, gd)


ATS_PAGES = 16
N_ROWS_S = 128


def _attn_sample_kernel(pt_ref, *refs, n_steps, past, t_dec):
    npg = ATS_PAGES
    kpages = refs[0:2 * npg:2]
    vpages = refs[1:2 * npg:2]
    (qn_ref, qr_ref, kc_ref, vc_ref, kvn_ref, cwin_ref, wnew_ref, gt_ref,
     o_ref, sel_ref, ocmp_ref, m_ref, l_ref, acc_ref) = refs[2 * npg:]
    s_id = pl.program_id(1)
    nr = N_ROWS_S
    rows = lax.broadcasted_iota(jnp.int32, (nr, 1), 0)
    t_row = (rows % (t_dec * HPG)) // HPG
    pos = past + t_row
    qr = qr_ref[0]

    @pl.when(s_id == 0)
    def _():
        n_c = kc_ref.shape[1]
        n_blk = (past + t_dec + SLC_BLK - 1) // SLC_BLK
        bps = ATS_PAGES * PAGE // SLC_BLK
        nbl = -(-(n_steps * bps + 128) // 128) * 128
        cl = lax.broadcasted_iota(jnp.int32, (1, n_c), 1)
        vis = (CMP_STRIDE * cl + (CMP_LEN - 1) <= pos) & (cl < n_c - 1)
        s = jnp.where(vis, _dot_nt(qn_ref[0], kc_ref[0]) * SCALE, NEG)
        e = jnp.exp(s - jnp.max(s, axis=1, keepdims=True))
        p = jnp.where(vis, e / jnp.sum(e, axis=1, keepdims=True), 0.0)
        ocmp_ref[...] = _dot(p.astype(BF16), vc_ref[0])
        ng = nr // HPG
        gsum = (lax.broadcasted_iota(jnp.int32, (ng, nr), 1) // HPG
                == lax.broadcasted_iota(jnp.int32, (ng, nr), 0)).astype(F32)
        cr = lax.broadcasted_iota(jnp.int32, (n_c, nbl), 0) * CMP_STRIDE
        j0 = lax.broadcasted_iota(jnp.int32, (n_c, nbl), 1) * SLC_BLK
        overlap = ((cr < j0 + SLC_BLK) & (cr + CMP_LEN > j0)).astype(F32)
        ps = _dot(_dot(gsum, p, precision=HI), overlap, precision=HI)
        bl = lax.broadcasted_iota(jnp.int32, (1, nbl), 1)
        g_rows = lax.broadcasted_iota(jnp.int32, (ng, 1), 0)
        cur = (past + g_rows % t_dec) // SLC_BLK
        forced = (bl == 0) | (bl == cur) | (bl == cur - 1)
        ps = jnp.where(bl <= cur, jnp.where(forced, FORCE, ps), NEG)
        rank = jnp.zeros(ps.shape, F32)
        for i in range(n_blk):
            vi = ps[:, i:i + 1]
            rank = rank + ((vi > ps) | ((vi == ps) & (bl > i))).astype(F32)
        sel = jnp.where((rank < N_SELECT) & (bl < n_blk), 1.0, 0.0)
        gexp = (lax.broadcasted_iota(jnp.int32, (nr, ng), 0) // HPG
                == lax.broadcasted_iota(jnp.int32, (nr, ng), 1)).astype(F32)
        sel_rows = _dot(gexp, sel)
        for w in range(sel_ref.shape[0]):
            sel_ref[w] = sel_rows[:, bps * w:bps * w + 128].astype(BF16)
        m_ref[...] = jnp.full(m_ref.shape, NEG, F32)
        l_ref[...] = jnp.zeros(l_ref.shape, F32)
        acc_ref[...] = jnp.zeros(acc_ref.shape, F32)

    def update(s, mask, v, v_transposed=False):
        s = jnp.where(mask, s, NEG)
        m_old = m_ref[...]
        m_new = jnp.maximum(m_old, jnp.max(s, axis=1, keepdims=True))
        p = jnp.where(mask, jnp.exp(s - m_new), 0.0)
        alpha = jnp.exp(m_old - m_new)
        l_ref[...] = alpha * l_ref[...] + jnp.sum(p, axis=1, keepdims=True)
        pv = _dot_nt(p.astype(BF16), v) if v_transposed else _dot(p.astype(BF16), v)
        acc_ref[...] = alpha * acc_ref[...] + pv
        m_ref[...] = m_new

    nk = npg * PAGE
    k_t = jnp.concatenate([r[0] for r in kpages], axis=1).astype(BF16)
    v_t = jnp.concatenate([r[0] for r in vpages], axis=1).astype(BF16)
    expand = (lax.broadcasted_iota(jnp.int32, (128, nk), 0)
              == lax.broadcasted_iota(jnp.int32, (128, nk), 1) // SLC_BLK).astype(BF16)
    kpos = s_id * nk + lax.broadcasted_iota(jnp.int32, (1, nk), 1)
    mask = (_dot(sel_ref[s_id], expand) > 0.5) & (kpos <= pos)
    update(_dot(qr, k_t) * SCALE, mask, v_t, v_transposed=True)

    @pl.when(s_id == n_steps - 1)
    def _():
        kn = kvn_ref[0, :, 512:768].astype(BF16)
        vn = kvn_ref[0, :, 768:1024].astype(BF16)
        tk = lax.broadcasted_iota(jnp.int32, (1, t_dec), 1)
        cur_sel = sel_ref[n_steps][:, 0:1].astype(F32) > 0.5
        update(_dot_nt(qr, kn) * SCALE, cur_sel & (past + tk <= pos), vn)
        o_slc = acc_ref[...] / l_ref[...]
        wb = cwin_ref.shape[1]
        kw = cwin_ref[0, :, 0:256].astype(BF16)
        vw = cwin_ref[0, :, 256:512].astype(BF16)
        kwn = wnew_ref[0, :, 0:256].astype(BF16)
        vwn = wnew_ref[0, :, 256:512].astype(BF16)
        d1 = pos - (past - wb + lax.broadcasted_iota(jnp.int32, (1, wb), 1))
        d2 = pos - (past + tk)
        m1 = (d1 >= 0) & (d1 < WINDOW)
        m2 = (d2 >= 0) & (d2 < WINDOW)
        s1 = jnp.where(m1, _dot_nt(qr, kw) * SCALE, NEG)
        s2 = jnp.where(m2, _dot_nt(qr, kwn) * SCALE, NEG)
        mx = jnp.maximum(jnp.max(s1, axis=1, keepdims=True), jnp.max(s2, axis=1, keepdims=True))
        p1 = jnp.where(m1, jnp.exp(s1 - mx), 0.0)
        p2 = jnp.where(m2, jnp.exp(s2 - mx), 0.0)
        den = jnp.sum(p1, axis=1, keepdims=True) + jnp.sum(p2, axis=1, keepdims=True)
        o_win = (_dot(p1.astype(BF16), vw) + _dot(p2.astype(BF16), vwn)) / den
        g = _sigmoid(gt_ref[0])
        o_ref[0] = g[:, 0:1] * ocmp_ref[...] + g[:, 1:2] * o_slc + g[:, 2:3] * o_win


def _attn_sample(cache_t, page_table, qn_x, qr_x, kc, vc, kv_new, cache_win2, win_new, gt_rows, t_dec):
    b, n_pages = page_table.shape
    past = n_pages * PAGE
    n_steps = n_pages // ATS_PAGES
    nr = N_ROWS_S

    def pspec(k, blk):
        return pl.BlockSpec((1, 256, PAGE), lambda i, s, pt: (pt[i, s * ATS_PAGES + k], blk, 0))

    per_b = lambda r, w: pl.BlockSpec((1, r, w), lambda i, s, pt: (i, 0, 0))
    page_specs = []
    for k in range(ATS_PAGES):
        page_specs += [pspec(k, 2), pspec(k, 3)]
    gs = pltpu.PrefetchScalarGridSpec(
        num_scalar_prefetch=1,
        grid=(b, n_steps),
        in_specs=page_specs + [per_b(nr, 256), per_b(nr, 256), per_b(kc.shape[1], 256),
                               per_b(kc.shape[1], 256), per_b(t_dec, 1024),
                               per_b(cache_win2.shape[1], 512), per_b(t_dec, 512), per_b(nr, 128)],
        out_specs=per_b(nr, 256),
        scratch_shapes=[pltpu.VMEM((n_steps + 1, nr, 128), BF16), pltpu.VMEM((nr, 256), F32),
                        pltpu.VMEM((nr, 1), F32), pltpu.VMEM((nr, 1), F32),
                        pltpu.VMEM((nr, 256), F32)],
    )
    return pl.pallas_call(
        functools.partial(_attn_sample_kernel, n_steps=n_steps, past=past, t_dec=t_dec),
        out_shape=jax.ShapeDtypeStruct((b, nr, 256), F32),
        grid_spec=gs,
        compiler_params=_cparams(("parallel", "arbitrary")),
        name="attn_sample",
    )(page_table, *([cache_t] * (2 * ATS_PAGES)), qn_x, qr_x, kc, vc, kv_new, cache_win2, win_new,
      gt_rows)


def _expand_rows(q):
    b, t, _ = q.shape
    q5 = q.reshape(b, t, N_KV, HPG, HEAD_DIM)
    eye = jnp.eye(N_KV, dtype=q.dtype)
    return jnp.einsum('btghd,gk->bgthkd', q5, eye).reshape(b, N_KV * t * HPG, N_KV * HEAD_DIM)


def _collapse_rows(o, t):
    b = o.shape[0]
    o6 = o.reshape(b, N_KV, t, HPG, N_KV, HEAD_DIM)
    return jnp.einsum('bgthkd,gk->btghd', o6, jnp.eye(N_KV, dtype=o.dtype)).reshape(b, t, D_ATT)


def _layernorm(x, g, b):
    mu = jnp.mean(x, axis=-1, keepdims=True)
    xc = x - mu
    var = jnp.mean(xc * xc, axis=-1, keepdims=True)
    return xc * lax.rsqrt(var + LN_EPS) * g + b


OUT_SUB = 256


def _outproj_kernel(oa_ref, ys_ref, x_ref, g1_ref, sc2_ref, sh2_ref, w_ref, lg_ref, lb_ref,
                    rw_ref, rb_ref, x1_ref, h2_ref, ti_ref, tw_ref):
    tm = x_ref.shape[1]
    sub = min(OUT_SUB, tm)
    for r0 in range(0, tm, sub):
        rs = slice(r0, r0 + sub)
        mix = _dot(oa_ref[0, rs, :], w_ref[0:D_ATT, :]) + _dot(ys_ref[0, rs, :], w_ref[D_ATT:, :])
        x1 = _layernorm(DN_ALPHA * x_ref[0, rs, :] + g1_ref[0] * mix, lg_ref[...], lb_ref[...])
        x1_ref[0, rs, :] = x1
        h2 = x1 * (1.0 + sc2_ref[0]) + sh2_ref[0]
        h_hi = h2.astype(BF16)
        h2_ref[0, rs, :] = _pack_pairs(h2)
        h_lo = (h2 - h_hi.astype(F32)).astype(BF16)
        logits = (_dot(h_hi, rw_ref[0]) + (_dot(h_lo, rw_ref[0]) + _dot(h_hi, rw_ref[1]))
                  + rb_ref[...])
        lane = lax.broadcasted_iota(jnp.int32, logits.shape, 1)
        vals, ids = [], []
        for _ in range(TOP_K):
            mx = jnp.max(logits, axis=1, keepdims=True)
            ix = jnp.min(jnp.where(logits == mx, lane, 128), axis=1, keepdims=True)
            vals.append(mx)
            ids.append(ix)
            logits = jnp.where(lane == ix, -jnp.inf, logits)
        es = [jnp.exp(v - vals[0]) for v in vals]
        inv = 1.0 / (es[0] + es[1] + es[2] + es[3])
        ti = jnp.zeros(lane.shape, jnp.int32)
        tw = jnp.zeros(lane.shape, F32)
        for k in range(TOP_K):
            ti = jnp.where(lane == k, ids[k], ti)
            tw = jnp.where(lane == k, es[k] * inv, tw)
        ti_ref[0, rs, :] = ti
        tw_ref[0, rs, :] = tw


def _out_proj(o_att, y_ssm, x, g1, sc2, sh2, w_out_bf, ln_g, ln_b, rw, rb, tm):
    b, t, _ = x.shape
    row = lambda w: pl.BlockSpec((1, tm, w), lambda i, j: (i, j, 0))
    per_b = pl.BlockSpec((1, 1, D_MODEL), lambda i, j: (i, 0, 0))
    par = lambda r, w, **kw: pl.BlockSpec((r, w), lambda i, j: (0, 0), **kw)
    return pl.pallas_call(
        _outproj_kernel,
        out_shape=[jax.ShapeDtypeStruct((b, t, D_MODEL), F32),
                   jax.ShapeDtypeStruct((b, t, D_MODEL // 2), jnp.uint32),
                   jax.ShapeDtypeStruct((b, t, 128), jnp.int32),
                   jax.ShapeDtypeStruct((b, t, 128), F32)],
        grid=(b, t // tm),
        in_specs=[row(D_ATT), row(D_SSM), row(D_MODEL), per_b, per_b, per_b,
                  par(D_MODEL, D_MODEL, pipeline_mode=pl.Buffered(1)),
                  par(1, D_MODEL), par(1, D_MODEL),
                  pl.BlockSpec((2, D_MODEL, 128), lambda i, j: (0, 0, 0)), par(1, 128)],
        out_specs=[row(D_MODEL), row(D_MODEL // 2), row(128), row(128)],
        compiler_params=_cparams(("parallel", "parallel")),
        name="out_proj",
    )(o_att, y_ssm, x, g1, sc2, sh2, w_out_bf, ln_g, ln_b, rw, rb)


MOE_TM = 1024
MOE_TN = 512
MOE_TN_DOWN = 1024
ROUTE_TM = 256


def _route_kernel(ti_ref, rank_ref, cnt_ref):
    @pl.when(pl.program_id(0) == 0)
    def _():
        cnt_ref[...] = jnp.zeros_like(cnt_ref)

    ti = ti_ref[...]
    tm = ti.shape[0]
    lane = lax.broadcasted_iota(jnp.int32, (tm, 128), 1)
    hits = [lane == ti[:, k:k + 1] for k in range(TOP_K)]
    oh = jnp.zeros((tm, 128), F32)
    for h in hits:
        oh = oh + h.astype(F32)
    ri = lax.broadcasted_iota(jnp.int32, (tm, tm), 0)
    ci = lax.broadcasted_iota(jnp.int32, (tm, tm), 1)
    before = _dot((ri > ci).astype(BF16), oh.astype(BF16)) + cnt_ref[0:1, :]
    rank = jnp.zeros((tm, 128), jnp.int32)
    for k, h in enumerate(hits):
        rk = jnp.sum(jnp.where(h, before, 0.0), axis=1, keepdims=True).astype(jnp.int32)
        rank = jnp.where(lane == k, rk, rank)
    rank_ref[...] = rank
    cnt_ref[...] = cnt_ref[...] + jnp.sum(oh, axis=0, keepdims=True)


def _route(ti_all):
    n = ti_all.shape[0]
    return pl.pallas_call(
        _route_kernel,
        out_shape=[jax.ShapeDtypeStruct((n, 128), jnp.int32),
                   jax.ShapeDtypeStruct((8, 128), F32)],
        grid=(n // ROUTE_TM,),
        in_specs=[pl.BlockSpec((ROUTE_TM, 128), lambda i: (i, 0))],
        out_specs=[pl.BlockSpec((ROUTE_TM, 128), lambda i: (i, 0)),
                   pl.BlockSpec((8, 128), lambda i: (0, 0))],
        compiler_params=_cparams(("arbitrary",)),
        name="moe_route",
    )(ti_all)


DMA_UNROLL = 8


def _dispatch_kernel(pos_ref, h_ref, xs_in_ref, xs_ref, sem):
    del xs_in_ref
    def copy(t, k):
        return pltpu.make_async_copy(h_ref.at[pl.ds(t, 1), :],
                                     xs_ref.at[pl.ds(pos_ref[0, 0, t * TOP_K + k], 1), :], sem)

    def start(t, c):
        for k in range(TOP_K):
            copy(t, k).start()
        return c

    def wait(t, c):
        for k in range(TOP_K):
            copy(t, k).wait()
        return c

    lax.fori_loop(0, ROUTE_TM, start, 0, unroll=DMA_UNROLL // TOP_K)
    lax.fori_loop(0, ROUTE_TM, wait, 0, unroll=DMA_UNROLL // TOP_K)


def _dispatch(pos, h, xs):
    n, w = h.shape
    assert n % ROUTE_TM == 0
    steps = n // ROUTE_TM
    return pl.pallas_call(
        _dispatch_kernel,
        out_shape=jax.ShapeDtypeStruct(xs.shape, xs.dtype),
        grid=(steps,),
        in_specs=[pl.BlockSpec((1, 1, ROUTE_TM * TOP_K), lambda i: (i, 0, 0), memory_space=pltpu.SMEM),
                  pl.BlockSpec((ROUTE_TM, w), lambda i: (i, 0)),
                  pl.BlockSpec(memory_space=pl.ANY)],
        out_specs=pl.BlockSpec(memory_space=pl.ANY),
        scratch_shapes=[pltpu.SemaphoreType.DMA(())],
        input_output_aliases={2: 0},
        compiler_params=_cparams(("arbitrary",), no_bounds_checks=True),
        name="moe_dispatch",
    )(pos.reshape(steps, 1, ROUTE_TM * TOP_K), h, xs)


def _moe_up_kernel(te_ref, nt_ref, tv_ref, xs_ref, wg_ref, wu_ref, bg_ref, bu_ref, act_ref):
    hk = D_MODEL // 2

    @pl.when(tv_ref[pl.program_id(0)] > 0)
    def _():
        lo, hi = [a.astype(BF16) for a in _unpack_pairs(xs_ref[...])]
        wg = wg_ref[0].astype(BF16)
        wu = wu_ref[0].astype(BF16)
        g = _dot(lo, wg[:hk]) + _dot(hi, wg[hk:]) + bg_ref[0]
        u = _dot(lo, wu[:hk]) + _dot(hi, wu[hk:]) + bu_ref[0]
        g = jnp.minimum(g, SWIGLU_LIMIT)
        u = jnp.clip(u, -SWIGLU_LIMIT, SWIGLU_LIMIT)
        act_ref[...] = ((u + 1.0) * g * _sigmoid(SWIGLU_ALPHA * g)).astype(BF16)

    @pl.when(tv_ref[pl.program_id(0)] <= 0)
    def _():
        act_ref[...] = jnp.zeros_like(act_ref)


def _moe_up(tile_e, n_used, tile_valid, xs, w_gu, b_gu3):
    n_rows = xs.shape[0]
    nj = D_FF // MOE_TN
    row_i = lambda i, j, te, nt, tv: (jnp.minimum(i, nt[0] - 1), 0)
    jc = lambda i, j, nt: jnp.where(i < nt[0], j, nj - 1)
    wspec = lambda off: pl.BlockSpec((1, D_MODEL, MOE_TN),
                                     lambda i, j, te, nt, tv: (te[i], 0, jc(i, j, nt) + off))
    bspec = lambda off: pl.BlockSpec((1, 1, MOE_TN),
                                     lambda i, j, te, nt, tv: (te[i], 0, jc(i, j, nt) + off))
    gs = pltpu.PrefetchScalarGridSpec(
        num_scalar_prefetch=3,
        grid=(n_rows // MOE_TM, nj),
        in_specs=[pl.BlockSpec((MOE_TM, D_MODEL // 2), row_i), wspec(0), wspec(nj), bspec(0), bspec(nj)],
        out_specs=pl.BlockSpec((MOE_TM, MOE_TN), lambda i, j, te, nt, tv: (i, j)),
    )
    return pl.pallas_call(
        _moe_up_kernel,
        out_shape=jax.ShapeDtypeStruct((n_rows, D_FF), BF16),
        grid_spec=gs,
        compiler_params=_cparams(("parallel", "arbitrary")),
        name="moe_up",
    )(tile_e, n_used, tile_valid, xs, w_gu, w_gu, b_gu3, b_gu3)


def _moe_down_kernel(te_ref, nt_ref, tv_ref, a_ref, w_ref, b_ref, y_ref):
    @pl.when(tv_ref[pl.program_id(0)] > 0)
    def _():
        y_ref[...] = _pack_pairs(_dot(a_ref[...], w_ref[0].astype(BF16)) + b_ref[0])

    @pl.when(tv_ref[pl.program_id(0)] <= 0)
    def _():
        y_ref[...] = jnp.zeros_like(y_ref)


def _moe_down(tile_e, n_used, tile_valid, act, w_down, b_down3):
    n_rows = act.shape[0]
    tn = MOE_TN_DOWN
    nj = D_MODEL // tn
    row_i = lambda i, j, te, nt, tv: (jnp.minimum(i, nt[0] - 1), 0)
    jc = lambda i, j, nt: jnp.where(i < nt[0], j, nj - 1)
    gs = pltpu.PrefetchScalarGridSpec(
        num_scalar_prefetch=3,
        grid=(n_rows // MOE_TM, nj),
        in_specs=[pl.BlockSpec((MOE_TM, D_FF), row_i),
                  pl.BlockSpec((1, D_FF, tn), lambda i, j, te, nt, tv: (te[i], 0, jc(i, j, nt))),
                  pl.BlockSpec((1, 1, tn), lambda i, j, te, nt, tv: (te[i], 0, jc(i, j, nt)))],
        out_specs=pl.BlockSpec((MOE_TM, tn // 2), lambda i, j, te, nt, tv: (i, j)),
    )
    return pl.pallas_call(
        _moe_down_kernel,
        out_shape=jax.ShapeDtypeStruct((n_rows, D_MODEL // 2), jnp.uint32),
        grid_spec=gs,
        compiler_params=_cparams(("parallel", "arbitrary")),
        name="moe_down",
    )(tile_e, n_used, tile_valid, act, w_down, b_down3)


def _combine_kernel(pos_ref, yd_ref, x1_ref, tw_ref, g2_ref, lg_ref, lb_ref, o_ref, buf, sem):
    tm = buf.shape[1]

    def copy(t, k):
        return pltpu.make_async_copy(yd_ref.at[pl.ds(pos_ref[0, 0, t * TOP_K + k], 1), :],
                                     buf.at[k, pl.ds(t, 1), :], sem)

    def start(t, c):
        for k in range(TOP_K):
            copy(t, k).start()
        return c

    def wait(t, c):
        for k in range(TOP_K):
            copy(t, k).wait()
        return c

    lax.fori_loop(0, tm, start, 0, unroll=DMA_UNROLL // TOP_K)
    lax.fori_loop(0, tm, wait, 0, unroll=DMA_UNROLL // TOP_K)
    tw = tw_ref[0]
    hw = MOE_TN_DOWN // 2
    moe = None
    for k in range(TOP_K):
        parts = []
        for j in range(buf.shape[2] // hw):
            parts += list(_unpack_pairs(buf[k, :, j * hw:(j + 1) * hw]))
        term = jnp.concatenate(parts, axis=1) * tw[:, k:k + 1]
        moe = term if moe is None else moe + term
    o_ref[0] = _layernorm(DN_ALPHA * x1_ref[0] + g2_ref[0] * moe, lg_ref[...], lb_ref[...])


def _combine(pos, yd, x1, tw, g2, ln_g, ln_b, tok0):
    b, t, _ = x1.shape
    tm = min(ROUTE_TM, t)
    per = t // tm
    step0 = tok0 // tm
    row = lambda w: pl.BlockSpec((1, tm, w), lambda i, j: (i, j, 0))
    par = pl.BlockSpec((1, D_MODEL), lambda i, j: (0, 0))
    return pl.pallas_call(
        functools.partial(_combine_kernel),
        out_shape=jax.ShapeDtypeStruct((b, t, D_MODEL), F32),
        grid=(b, per),
        in_specs=[pl.BlockSpec((1, 1, tm * TOP_K), lambda i, j: (step0 + i * per + j, 0, 0),
                               memory_space=pltpu.SMEM),
                  pl.BlockSpec(memory_space=pl.ANY),
                  row(D_MODEL), row(128),
                  pl.BlockSpec((1, 1, D_MODEL), lambda i, j: (i, 0, 0)), par, par],
        out_specs=row(D_MODEL),
        scratch_shapes=[pltpu.VMEM((TOP_K, tm, D_MODEL // 2), jnp.uint32),
                        pltpu.SemaphoreType.DMA(())],
        compiler_params=_cparams(("arbitrary", "arbitrary"), no_bounds_checks=True),
        name="moe_combine",
    )(pos.reshape(-1, 1, tm * TOP_K), yd, x1, tw, g2, ln_g, ln_b)


IN_TM = 256
SSD_Q = 256
OUT_TM = 512


def kernel(x_prompt, x_sample, cache_kv, cache_win, state_conv, state_ssm, page_table,
           c_prompt, c_sample, w_ada, b_ada, w_in, cmp_pe, cmp_w1, cmp_b1, cmp_w2,
           conv_w, conv_b, dt_bias, a_log, d_skip, ssm_norm_g, w_out, ln1_g, ln1_b,
           router_w, router_b, w_gu, b_gu, w_down, b_down, ln2_g, ln2_b):
    bp, tp, _ = x_prompt.shape
    bs, ts, _ = x_sample.shape
    past = page_table.shape[1] * PAGE

    m = _ada(jnp.concatenate([c_prompt, c_sample], axis=0), w_ada[0], b_ada[0])
    mod = [m[:, None, k * D_MODEL:(k + 1) * D_MODEL] for k in range(6)]
    mod_p = [a[:bp] for a in mod]
    mod_s = [a[bp:] for a in mod]

    perm = _q_perm()
    inv_perm = np.argsort(perm)
    w_bf = _prep_w_in(w_in[0])
    cw = _prep_cmp_weights(cmp_pe[0], cmp_w1[0], cmp_b1[0], cmp_w2[0])
    sp = _prep_ssm_params(conv_w[0], conv_b[0], dt_bias[0], a_log[0], d_skip[0], ssm_norm_g[0])
    w_out_bf = jnp.concatenate([w_out[0][:D_ATT][perm], w_out[0][D_ATT:]], axis=0).astype(BF16)
    rw = jnp.concatenate([router_w[0], jnp.zeros((D_MODEL, 128 - N_EXPERTS), F32)], axis=1)
    rw_hi = rw.astype(BF16)
    rw = jnp.stack([rw_hi, (rw - rw_hi.astype(F32)).astype(BF16)])
    rb = jnp.concatenate([router_b[0], jnp.full((128 - N_EXPERTS,), NEG, F32)]).reshape(1, 128)
    ln1 = (ln1_g[0].reshape(1, D_MODEL), ln1_b[0].reshape(1, D_MODEL))
    ln2 = (ln2_g[0].reshape(1, D_MODEL), ln2_b[0].reshape(1, D_MODEL))

    tab_p = _rope_tables(jnp.arange(tp))
    qn, qr, kv_p, win_p, kvb, z, xbc, gd = _in_proj(x_prompt, mod_p[1], mod_p[0], w_bf, tab_p, IN_TM)
    kc, vc = _compress_prompt(kv_p, cw)
    o_att = _attn_prompt(qn, qr, kc, vc, kvb, gd)
    y_ssm, ssm_p = _ssd(xbc, z, gd, jnp.zeros((bp, 8, CONV_CH), F32),
                        jnp.zeros((bp, D_SSM, D_STATE), F32), sp, SSD_Q)
    x1_p, h2_p, ti_p, tw_p = _out_proj(o_att, y_ssm, x_prompt, mod_p[2], mod_p[4], mod_p[3],
                                       w_out_bf, *ln1, rw, rb, OUT_TM)

    tab_s = _rope_tables(past + jnp.arange(ts))
    n_s = bs * ts
    per_tok = lambda a: jnp.broadcast_to(a, (bs, ts, D_MODEL)).reshape(1, n_s, D_MODEL)
    s_outs = _in_proj(x_sample.reshape(1, n_s, D_MODEL), per_tok(mod_s[1]), per_tok(mod_s[0]),
                      w_bf, jnp.tile(tab_s, (bs, 1)), n_s)
    qn_s, qr_s, kv_s, win_s, _, z_s, xbc_s, gd_s = [a.reshape(bs, ts, a.shape[-1]) for a in s_outs]
    cache_t = jnp.transpose(cache_kv[0], (0, 2, 3, 4, 1)).reshape(
        cache_kv.shape[1], 4 * N_KV * HEAD_DIM, PAGE)
    kc_s, vc_s = _compress_sample(cache_t, page_table, cw)
    wb = cache_win.shape[2]
    cwin2 = cache_win[0].reshape(bs, wb, 2 * N_KV * HEAD_DIM)
    gt_rows = gd_s[:, :, :D_GT].reshape(bs, ts, N_KV, HPG, 3).transpose(0, 2, 1, 3, 4)
    gt_rows = jnp.concatenate([gt_rows.reshape(bs, N_ROWS_S, 3),
                               jnp.zeros((bs, N_ROWS_S, 125), F32)], axis=-1)
    o_rows = _attn_sample(cache_t, page_table, _expand_rows(qn_s[:, :, inv_perm]),
                          _expand_rows(qr_s[:, :, inv_perm]), kc_s, vc_s, kv_s, cwin2, win_s,
                          gt_rows, ts)
    o_att_s = _collapse_rows(o_rows, ts)[:, :, perm].astype(BF16)
    cprev = jnp.concatenate([jnp.zeros((bs, 5, CONV_CH), F32), state_conv[0]], axis=1)
    y_ssm_s, ssm_s = _ssd(xbc_s, z_s, gd_s, cprev, state_ssm[0].reshape(bs, D_SSM, D_STATE), sp, ts)
    x1_s, h2_s, ti_s, tw_s = _out_proj(o_att_s, y_ssm_s, x_sample, mod_s[2], mod_s[4], mod_s[3],
                                       w_out_bf, *ln1, rw, rb, ts)

    n_tok = bp * tp + bs * ts
    n_tiles = -(-(n_tok * TOP_K + N_EXPERTS * (MOE_TM - 1)) // MOE_TM)
    ti_all = jnp.concatenate([ti_p.reshape(-1, 128), ti_s.reshape(-1, 128)], axis=0)
    rank, cnt = _route(ti_all)
    counts = cnt[0, :N_EXPERTS].astype(jnp.int32)
    padded = (counts + MOE_TM - 1) // MOE_TM * MOE_TM
    ends = jnp.cumsum(padded)
    offs = ends - padded
    n_used = (ends[-1] // MOE_TM).astype(jnp.int32).reshape(1)
    tiles = jnp.minimum(jnp.arange(n_tiles, dtype=jnp.int32), n_used[0] - 1)
    tile_e = jnp.sum((tiles[:, None] * MOE_TM >= ends[None, :]).astype(jnp.int32), axis=1)
    tile_e = jnp.minimum(tile_e, N_EXPERTS - 1).astype(jnp.int32)
    first_tile = offs // MOE_TM
    tile_valid = jnp.clip(counts[tile_e] - (tiles - first_tile[tile_e]) * MOE_TM, 0, MOE_TM)
    tile_valid = jnp.where(jnp.arange(n_tiles) < n_used[0], tile_valid, 0).astype(jnp.int32)
    pos = (offs[ti_all[:, :TOP_K]] + rank[:, :TOP_K]).astype(jnp.int32)
    n_p = bp * tp
    xs = jnp.zeros((n_tiles * MOE_TM, D_MODEL // 2), jnp.uint32)
    xs = _dispatch(pos[:n_p], h2_p.reshape(n_p, D_MODEL // 2), xs)
    xs = _dispatch(pos[n_p:], h2_s.reshape(bs * ts, D_MODEL // 2), xs)
    act = _moe_up(tile_e, n_used, tile_valid, xs, w_gu[0], b_gu[0].reshape(N_EXPERTS, 1, 2 * D_FF))
    yd = _moe_down(tile_e, n_used, tile_valid, act, w_down[0],
                   b_down[0].reshape(N_EXPERTS, 1, D_MODEL))
    y_p = _combine(pos, yd, x1_p, tw_p, mod_p[5], *ln2, 0)
    y_s = _combine(pos, yd, x1_s, tw_s, mod_s[5], *ln2, bp * tp)

    kv_shape = (4, N_KV, HEAD_DIM)
    win_shape = (2, N_KV, HEAD_DIM)
    win_prompt = win_p[:, tp - min(WINDOW, tp):].reshape((1, bp, min(WINDOW, tp)) + win_shape)
    win_sample = jnp.concatenate([cwin2, win_s], axis=1)[:, -wb:].reshape((1, bs, wb) + win_shape)
    return (y_p, y_s,
            kv_p.reshape((1, bp, tp) + kv_shape), kv_s.reshape((1, bs, ts) + kv_shape),
            win_prompt, win_sample,
            xbc[None, :, tp - (CONV_W - 1):], xbc_s[None, :, ts - (CONV_W - 1):],
            ssm_p.reshape(1, bp, N_HEADS_SSM, HEAD_DIM, D_STATE),
            ssm_s.reshape(1, bs, N_HEADS_SSM, HEAD_DIM, D_STATE))
```

```python
import functools
import math

import jax
import jax.numpy as jnp
import numpy as np
from jax import lax
from jax.experimental import pallas as pl
from jax.experimental.pallas import tpu as pltpu

D_MODEL = 2048
D_ATT = 1024
D_SSM = 1024
HEAD_DIM = 64
N_HEADS_ATT = 16
N_KV = 4
HPG = 4
ROT_DIM = 16
ROPE_THETA = 500000.0
CMP_LEN = 32
CMP_STRIDE = 16
CMP_HID = 128
SLC_BLK = 64
N_SELECT = 16
WINDOW = 512
N_HEADS_SSM = 16
SSM_GROUPS = 4
D_STATE = 128
CONV_W = 4
CONV_CH = 2048
N_EXPERTS = 32
TOP_K = 4
D_FF = 2048
SWIGLU_LIMIT = 7.0
SWIGLU_ALPHA = 1.702
DN_ALPHA = 2.0 ** 0.25
LN_EPS = 1e-5
RMS_EPS = 1e-5
NEG = -1e30
FORCE = 1e6
PAGE = 128

D_KV = 6 * N_KV * HEAD_DIM
D_GT = 3 * N_HEADS_ATT
D_IN = D_ATT + D_KV + D_GT + D_SSM + CONV_CH + N_HEADS_SSM
D_INP = D_ATT + D_KV + D_SSM + CONV_CH + 128

LANES = 128
VMEM_LIMIT = 56 * 1024 * 1024

F32 = jnp.float32
BF16 = jnp.bfloat16
HI = lax.Precision.HIGHEST


def _cparams(sem, vmem=VMEM_LIMIT, no_bounds_checks=False):
    return pltpu.CompilerParams(dimension_semantics=sem, vmem_limit_bytes=vmem,
                                disable_bounds_checks=no_bounds_checks)


def _dot(a, b, precision=None):
    return jnp.dot(a, b, preferred_element_type=F32, precision=precision)


def _dot_nt(a, b, precision=None):
    return lax.dot_general(a, b, (((1,), (1,)), ((), ())), preferred_element_type=F32,
                           precision=precision)


def _dot_tn(a, b, precision=None):
    return lax.dot_general(a, b, (((0,), (0,)), ((), ())), preferred_element_type=F32,
                           precision=precision)


def _split3(x):
    x1 = x.astype(BF16)
    r = x - x1.astype(F32)
    x2 = r.astype(BF16)
    return x1, x2, (r - x2.astype(F32)).astype(BF16)


def _dot_sel_l(sel, x):
    s = sel.astype(BF16)
    x1, x2, x3 = _split3(x)
    return _dot(s, x1) + (_dot(s, x2) + _dot(s, x3))


def _dot_sel_r(x, sel):
    s = sel.astype(BF16)
    x1, x2, x3 = _split3(x)
    return _dot(x1, s) + (_dot(x2, s) + _dot(x3, s))


def _pack_pairs(x):
    h = x.shape[1] // 2
    bits = lax.bitcast_convert_type(x.astype(BF16).astype(F32), jnp.uint32)
    return (bits[:, :h] >> 16) | (bits[:, h:] & jnp.uint32(0xFFFF0000))


def _unpack_pairs(w):
    lo = lax.bitcast_convert_type(w << 16, F32)
    hi = lax.bitcast_convert_type(w & jnp.uint32(0xFFFF0000), F32)
    return lo, hi


def _sigmoid(x):
    return 1.0 / (1.0 + jnp.exp(-x))


def _silu(x):
    return x * _sigmoid(x)


def _ada_kernel(c_ref, w_ref, b_ref, o_ref):
    c = c_ref[...]
    a = _silu(c).astype(BF16)
    o_ref[...] = _dot(a, w_ref[...].astype(BF16)) + b_ref[...]


def _ada(c_all, w_ada, b_ada):
    nb = c_all.shape[0]
    tn = 1024
    n = w_ada.shape[1]
    return pl.pallas_call(
        _ada_kernel,
        out_shape=jax.ShapeDtypeStruct((nb, n), F32),
        grid=(n // tn,),
        in_specs=[pl.BlockSpec((nb, D_MODEL), lambda j: (0, 0)),
                  pl.BlockSpec((D_MODEL, tn), lambda j: (0, j)),
                  pl.BlockSpec((1, tn), lambda j: (0, j))],
        out_specs=pl.BlockSpec((nb, tn), lambda j: (0, j)),
        compiler_params=_cparams(("parallel",)),
        name="ada",
    )(c_all, w_ada, b_ada.reshape(1, n))


def _q_perm():
    idx = np.zeros(D_ATT, np.int32)
    for gp in range(2):
        for j in range(HPG):
            for side in range(2):
                g = 2 * gp + side
                for d in range(HEAD_DIM):
                    idx[gp * 512 + j * 128 + side * 64 + d] = g * 256 + j * 64 + d
    return idx


def _prep_w_in(w_in):
    o_kv = D_ATT
    o_gt = o_kv + D_KV
    o_z = o_gt + D_GT
    o_xbc = o_z + D_SSM
    o_dt = o_xbc + CONV_CH
    wq = w_in[:, :D_ATT][:, _q_perm()]
    pad = jnp.zeros((D_MODEL, 128 - D_GT - N_HEADS_SSM), w_in.dtype)
    w = jnp.concatenate([wq, w_in[:, o_kv:o_gt], w_in[:, o_z:o_xbc], w_in[:, o_xbc:o_dt],
                         w_in[:, o_gt:o_z], w_in[:, o_dt:], pad], axis=1)
    return w.astype(BF16)


def _rope_tables(pos):
    half = ROT_DIM // 2
    inv = ROPE_THETA ** (-jnp.arange(half, dtype=F32) / half)
    ang = pos.astype(F32)[:, None] * inv
    cos, sin = jnp.cos(ang), jnp.sin(ang)
    t = pos.shape[0]
    one = jnp.ones((t, HEAD_DIM - ROT_DIM), F32)
    zero = jnp.zeros((t, HEAD_DIM - ROT_DIM), F32)
    z8 = jnp.zeros((t, half), F32)
    c64 = jnp.concatenate([cos, cos, one], 1)
    s1 = jnp.concatenate([z8, sin, zero], 1)
    s2 = jnp.concatenate([-sin, z8, zero], 1)
    rep = lambda a: jnp.tile(a, (1, 4))
    return jnp.concatenate([rep(c64), rep(s1), rep(s2)], axis=1)


def _rope256(x, tab):
    n = x.shape[1]
    return (x * tab[:, 0:256] + pltpu.roll(x, 8, 1) * tab[:, 256:512]
            + pltpu.roll(x, n - 8, 1) * tab[:, 512:768])


def _inproj_kernel(x_ref, sc_ref, sh_ref, w_ref, tab_ref,
                   qn_ref, qr_ref, kv_ref, win_ref, kvb_ref, z_ref, xbc_ref, gd_ref):
    h = (x_ref[0] * (1.0 + sc_ref[0]) + sh_ref[0]).astype(BF16)
    tab = tab_ref[...]
    for c in range(4):
        q = _dot(h, w_ref[:, c * 256:(c + 1) * 256])
        qn_ref[0, :, c * 256:(c + 1) * 256] = q.astype(BF16)
        qr_ref[0, :, c * 256:(c + 1) * 256] = _rope256(q, tab).astype(BF16)
    o = D_ATT
    for p in range(6):
        y = _dot(h, w_ref[:, o + p * 256:o + (p + 1) * 256])
        if p in (2, 4):
            y = _rope256(y, tab)
        if p < 4:
            kv_ref[0, :, p * 256:(p + 1) * 256] = y
        else:
            win_ref[0, :, (p - 4) * 256:(p - 3) * 256] = y
        if p >= 2:
            kvb_ref[0, :, (p - 2) * 256:(p - 1) * 256] = y.astype(BF16)
    o += D_KV
    for c in range(2):
        z_ref[0, :, c * 512:(c + 1) * 512] = _dot(h, w_ref[:, o + c * 512:o + (c + 1) * 512]).astype(BF16)
    o += D_SSM
    for c in range(4):
        xbc_ref[0, :, c * 512:(c + 1) * 512] = _dot(h, w_ref[:, o + c * 512:o + (c + 1) * 512])
    o += CONV_CH
    gd_ref[0] = _dot(h, w_ref[:, o:o + 128])


def _in_proj(x, sc, sh, w_bf, tab, tm):
    b, t, _ = x.shape
    nt = t // tm
    row = lambda w: pl.BlockSpec((1, tm, w), lambda i, j: (i, j, 0))
    mod = row(D_MODEL) if sc.shape[1] == t else pl.BlockSpec((1, 1, D_MODEL), lambda i, j: (i, 0, 0))
    outs = [(D_ATT, BF16), (D_ATT, BF16), (1024, F32), (512, F32), (1024, BF16),
            (D_SSM, BF16), (CONV_CH, F32), (128, F32)]
    return pl.pallas_call(
        _inproj_kernel,
        out_shape=[jax.ShapeDtypeStruct((b, t, w), dt) for w, dt in outs],
        grid=(b, nt),
        in_specs=[row(D_MODEL), mod, mod,
                  pl.BlockSpec((D_MODEL, D_INP), lambda i, j: (0, 0), pipeline_mode=pl.Buffered(1)),
                  pl.BlockSpec((tm, 768), lambda i, j: (j, 0))],
        out_specs=[row(w) for w, _ in outs],
        compiler_params=_cparams(("parallel", "parallel")),
        name="in_proj",
    )(x, sc, sh, w_bf, tab)


def _prep_cmp_weights(cmp_pe, cmp_w1, cmp_b1, cmp_w2):
    w1 = cmp_w1.reshape(2, 2, CMP_STRIDE, HEAD_DIM, CMP_HID)
    eye = jnp.eye(2, dtype=cmp_w1.dtype)
    w1p = jnp.einsum('phsdj,ab->psadhbj', w1, eye).reshape(2, CMP_STRIDE * 128, 4 * CMP_HID)
    w2p = jnp.einsum('pjd,ab->pajbd', cmp_w2, eye).reshape(2, 2 * CMP_HID, 2 * HEAD_DIM)
    pe = cmp_pe.reshape(2, 1, CMP_LEN * HEAD_DIM)
    pe = jnp.concatenate([pe, jnp.zeros((2, 7, CMP_LEN * HEAD_DIM), pe.dtype)], axis=1)
    b1 = jnp.concatenate([cmp_b1, cmp_b1], axis=-1).reshape(2, 1, 2 * CMP_HID)
    return w1p.astype(BF16), w2p.astype(BF16), pe, cmp_w1, b1


def _gelu_tanh(x):
    return 0.5 * x * (1.0 + jnp.tanh(0.7978845608028654 * (x + 0.044715 * x * x * x)))


def _compress_core(load, nrows, w1p_ref, w2p_ref, pe_ref, w1_ref, b1_ref, store, prepare=None):
    for p in range(2):
        pe_bias = _dot(pe_ref[p], w1_ref[p], precision=HI)[0:1]
        pe_bias = jnp.concatenate([pe_bias, pe_bias], axis=1)
        for pr in range(2):
            if prepare is not None:
                prepare(2 * p + pr)
            xs = jnp.concatenate([load(s, 2 * p + pr).astype(BF16) for s in range(CMP_STRIDE)], axis=1)
            ab = _dot(xs, w1p_ref[p])
            a, bb = ab[:, 0:256], ab[:, 256:512]
            pre = a + pltpu.roll(bb, nrows - 1, 0) + pe_bias + b1_ref[p]
            hid = _gelu_tanh(pre).astype(BF16)
            store(p, pr, _dot(hid, w2p_ref[p]))


def _compress_prompt_kernel(b0, b1, b2, b3, w1p_ref, w2p_ref, pe_ref, w1_ref, b1_ref, kc_ref, vc_ref):
    blocks = (b0, b1, b2, b3)
    n = b0.shape[1] // CMP_STRIDE
    outs = (kc_ref, vc_ref)

    def load(s, lb):
        return blocks[lb][0, pl.ds(s, n, stride=CMP_STRIDE), :]

    def store(p, pr, val):
        outs[p][0, :, pr * 128:(pr + 1) * 128] = val.astype(BF16)

    _compress_core(load, n, w1p_ref, w2p_ref, pe_ref, w1_ref, b1_ref, store)


def _wspecs():
    z3 = lambda *a: (0, 0, 0)
    return [pl.BlockSpec((2, CMP_STRIDE * 128, 4 * CMP_HID), z3),
            pl.BlockSpec((2, 2 * CMP_HID, 2 * HEAD_DIM), z3),
            pl.BlockSpec((2, 8, CMP_LEN * HEAD_DIM), z3),
            pl.BlockSpec((2, CMP_LEN * HEAD_DIM, CMP_HID), z3),
            pl.BlockSpec((2, 1, 2 * CMP_HID), z3)]


def _compress_prompt(kv_f32, cw):
    b, t, _ = kv_f32.shape
    n = t // CMP_STRIDE
    lane_blk = lambda lb: pl.BlockSpec((1, t, 128), lambda i: (i, 0, lb))
    return pl.pallas_call(
        _compress_prompt_kernel,
        out_shape=[jax.ShapeDtypeStruct((b, n, 256), BF16)] * 2,
        grid=(b,),
        in_specs=[lane_blk(lb) for lb in range(4)] + _wspecs(),
        out_specs=[pl.BlockSpec((1, n, 256), lambda i: (i, 0, 0))] * 2,
        compiler_params=_cparams(("parallel",)),
        name="compress_prompt",
    )(kv_f32, kv_f32, kv_f32, kv_f32, *cw)


CMP_PAGES = 32


def _compress_sample_kernel(pt_ref, *refs):
    npg = CMP_PAGES + 1
    pages = refs[:npg]
    w1p_ref, w2p_ref, pe_ref, w1_ref, b1_ref, kc_ref, vc_ref, tok_ref = refs[npg:]
    cpp = PAGE // CMP_STRIDE
    nrows = CMP_PAGES * cpp
    outs = (kc_ref, vc_ref)
    def prepare(lb):
        for k in range(npg):
            tok_ref[lb, k * PAGE:(k + 1) * PAGE, :] = pages[k][0, lb * 128:(lb + 1) * 128, :].T

    def load(s, lb):
        return tok_ref[lb, pl.ds(s, nrows + cpp, stride=CMP_STRIDE), :]

    def store(p, pr, val):
        outs[p][0, :, pr * 128:(pr + 1) * 128] = val[:nrows].astype(BF16)

    _compress_core(load, nrows + cpp, w1p_ref, w2p_ref, pe_ref, w1_ref, b1_ref, store, prepare)


def _compress_sample(cache_t, page_table, cw):
    b, n_pages = page_table.shape
    steps = n_pages // CMP_PAGES
    nrows = CMP_PAGES * (PAGE // CMP_STRIDE)

    def pspec(k):
        return pl.BlockSpec(
            (1, 512, PAGE),
            lambda i, j, pt: (pt[i, jnp.minimum(j * CMP_PAGES + k, n_pages - 1)], 0, 0))

    gs = pltpu.PrefetchScalarGridSpec(
        num_scalar_prefetch=1,
        grid=(b, steps),
        in_specs=[pspec(k) for k in range(CMP_PAGES + 1)]
                 + [pl.BlockSpec(s.block_shape, lambda i, j, pt: (0, 0, 0)) for s in _wspecs()],
        out_specs=[pl.BlockSpec((1, nrows, 256), lambda i, j, pt: (i, j, 0))] * 2,
        scratch_shapes=[pltpu.VMEM((4, (CMP_PAGES + 1) * PAGE, 128), F32)],
    )
    return pl.pallas_call(
        _compress_sample_kernel,
        out_shape=[jax.ShapeDtypeStruct((b, n_pages * 8, 256), BF16)] * 2,
        grid_spec=gs,
        compiler_params=_cparams(("parallel", "parallel")),
        name="compress_sample",
    )(page_table, *([cache_t] * (CMP_PAGES + 1)), *cw)


DT_LANE = D_GT


def _prep_ssm_params(conv_w, conv_b, dt_bias, a_log, d_skip, ssm_norm_g):
    cw = jnp.concatenate([conv_w, jnp.zeros((8 - CONV_W, CONV_CH), conv_w.dtype)], axis=0)
    lane = lambda v: jnp.zeros((1, 128), F32).at[0, DT_LANE:DT_LANE + N_HEADS_SSM].set(v)
    dsk = jnp.repeat(d_skip, HEAD_DIM).reshape(1, D_SSM)
    return (cw, conv_b.reshape(1, CONV_CH), lane(dt_bias), lane(a_log), dsk,
            ssm_norm_g.reshape(1, D_SSM))


def _ssd_kernel(xbc_ref, z_ref, gd_ref, cprev_ref, h0_ref, cw_ref, cb_ref, dtb_ref, alog_ref,
                dsk_ref, ng_ref, y_ref, st_ref, xpad_ref):
    q = xbc_ref.shape[1]
    c = pl.program_id(1)

    @pl.when(c == 0)
    def _():
        st_ref[0] = h0_ref[0]
        xpad_ref[0:8, :] = cprev_ref[0]

    xpad_ref[8:8 + q, :] = xbc_ref[0]
    conv = cb_ref[...] + xpad_ref[pl.ds(5, q), :] * cw_ref[0:1, :]
    for k in range(1, CONV_W):
        conv = conv + xpad_ref[pl.ds(5 + k, q), :] * cw_ref[k:k + 1, :]
    xpad_ref[0:8, :] = xpad_ref[q:q + 8, :]
    xc = _silu(conv)
    xs = xc[:, 0:D_SSM]

    lane = lax.broadcasted_iota(jnp.int32, (1, 128), 1)
    in_dt = (lane >= DT_LANE) & (lane < DT_LANE + N_HEADS_SSM)
    v = gd_ref[0] + dtb_ref[...]
    dt = jnp.maximum(v, 0.0) + jnp.log1p(jnp.exp(-jnp.abs(v)))
    a = jnp.where(in_dt, -jnp.exp(alog_ref[...]), 0.0) * dt
    er = lax.broadcasted_iota(jnp.int32, (128, D_SSM), 0)
    ec = lax.broadcasted_iota(jnp.int32, (128, D_SSM), 1)
    expand = er - DT_LANE == ec // HEAD_DIM
    ri = lax.broadcasted_iota(jnp.int32, (q, q), 0)
    ci = lax.broadcasted_iota(jnp.int32, (q, q), 1)
    causal = ri >= ci
    acs = _dot_sel_l(causal, a)
    acs_t = acs.T
    dt_x = _dot_sel_r(jnp.where(in_dt, dt, 0.0), expand)
    acs_x = _dot_sel_r(acs, expand)
    last_x = acs_x[q - 1:q, :]
    grow_x = jnp.exp(acs_x)
    decay_x = jnp.exp(last_x - acs_x)
    er2 = lax.broadcasted_iota(jnp.int32, (D_SSM, 128), 0)
    ec2 = lax.broadcasted_iota(jnp.int32, (D_SSM, 128), 1)
    expand_t = ec2 - DT_LANE == er2 // HEAD_DIM
    tot_col = jnp.exp(_dot_sel_l(expand_t, acs_t)[:, q - 1:q])

    xd = xs * dt_x
    xdd = (xd * decay_x).astype(BF16)
    xd_b = xd.astype(BF16)
    half = lax.broadcasted_iota(jnp.int32, (1, 128), 1) < HEAD_DIM
    ys = []
    for g in range(SSM_GROUPS):
        bg = xc[:, D_SSM + g * D_STATE:D_SSM + (g + 1) * D_STATE].astype(BF16)
        cg = xc[:, D_SSM + (SSM_GROUPS + g) * D_STATE:D_SSM + (SSM_GROUPS + g + 1) * D_STATE].astype(BF16)
        cbm = _dot_nt(cg, bg)
        for m in (2 * g, 2 * g + 1):
            sl = slice(128 * m, 128 * (m + 1))
            yh = []
            for hh in (2 * m, 2 * m + 1):
                col = acs[:, DT_LANE + hh:DT_LANE + hh + 1]
                row = acs_t[DT_LANE + hh:DT_LANE + hh + 1, :]
                lm = jnp.where(causal, jnp.exp(jnp.where(causal, col - row, 0.0)), 0.0)
                yh.append(_dot((cbm * lm).astype(BF16), xd_b[:, sl]))
            y_diag = jnp.where(half, yh[0], yh[1])
            st = st_ref[0, sl, :]
            y_off = _dot_nt(cg, st.astype(BF16)) * grow_x[:, sl]
            st_ref[0, sl, :] = st * tot_col[sl, :] + _dot_tn(xdd[:, sl], bg)
            ys.append(y_diag + y_off)
    y = jnp.concatenate(ys, axis=1) + dsk_ref[...] * xs
    y = y * _silu(z_ref[0].astype(F32))
    gw = D_SSM // SSM_GROUPS
    outs = []
    for g in range(SSM_GROUPS):
        blk = y[:, g * gw:(g + 1) * gw]
        ms = jnp.mean(blk * blk, axis=1, keepdims=True)
        outs.append(blk * lax.rsqrt(ms + RMS_EPS))
    y_ref[0] = (jnp.concatenate(outs, axis=1) * ng_ref[...]).astype(BF16)


def _ssd(xbc, z, gd, conv_prev8, h0, sp, q):
    b, t, _ = xbc.shape
    nc = t // q
    row = lambda w: pl.BlockSpec((1, q, w), lambda i, j: (i, j, 0))
    per_b = lambda r, w: pl.BlockSpec((1, r, w), lambda i, j: (i, 0, 0))
    par = lambda r, w: pl.BlockSpec((r, w), lambda i, j: (0, 0))
    return pl.pallas_call(
        _ssd_kernel,
        out_shape=[jax.ShapeDtypeStruct((b, t, D_SSM), BF16),
                   jax.ShapeDtypeStruct((b, D_SSM, D_STATE), F32)],
        grid=(b, nc),
        in_specs=[row(CONV_CH), row(D_SSM), row(128), per_b(8, CONV_CH), per_b(D_SSM, D_STATE),
                  par(8, CONV_CH), par(1, CONV_CH), par(1, 128), par(1, 128), par(1, D_SSM),
                  par(1, D_SSM)],
        out_specs=[row(D_SSM), per_b(D_SSM, D_STATE)],
        scratch_shapes=[pltpu.VMEM((q + 8, CONV_CH), F32)],
        compiler_params=_cparams(("parallel", "arbitrary")),
        name="ssd",
    )(xbc, z, gd, conv_prev8, h0, *sp)


ATT_TQ = 512
SCALE = HEAD_DIM ** -0.5


def _select_blocks_t(ps, n_blk):
    blk = lax.broadcasted_iota(jnp.int32, ps.shape, 0)
    rank = jnp.zeros(ps.shape, F32)
    for i in range(n_blk):
        vi = ps[i:i + 1, :]
        rank = rank + jnp.where(vi > ps, 1.0, jnp.where((vi == ps) & (blk > i), 1.0, 0.0))
    return jnp.where(rank < N_SELECT, 1.0, 0.0)


def _attn_prompt_kernel(qn_ref, qr_ref, kc_ref, vc_ref, ks_ref, vs_ref, kw_ref, vw_ref, gd_ref,
                        o_ref, qz_ref, sel_ref, ocmp_ref, m_ref, acc_ref):
    tq = qn_ref.shape[1]
    t_all = ks_ref.shape[1]
    n_cmp = t_all // CMP_STRIDE - 1
    n_blk = t_all // SLC_BLK
    gp = pl.program_id(1)
    qi = pl.program_id(2)
    rep = lambda a: jnp.concatenate([a] * HPG, axis=1)
    pos = qi * tq + lax.broadcasted_iota(jnp.int32, (1, tq), 1)
    pos4 = rep(pos)
    lane = lax.broadcasted_iota(jnp.int32, (1, 128), 1)
    row = lax.broadcasted_iota(jnp.int32, (128, 1), 0)
    lane_side = (lane < HEAD_DIM, lane >= HEAD_DIM)
    row_side = (row < HEAD_DIM, row >= HEAD_DIM)

    for r, ref in enumerate((qn_ref, qr_ref)):
        for side in range(2):
            for j in range(HPG):
                col = ref[0, :, j * 128:(j + 1) * 128]
                col = jnp.where(lane_side[side], col, jnp.zeros_like(col))
                qz_ref[2 * r + side, j * tq:(j + 1) * tq, :] = col * SCALE

    def v_sides(v):
        vt = v.astype(F32).T
        return [jnp.where(row_side[s], vt, 1.0).astype(BF16) for s in range(2)]

    kc = kc_ref[0]
    vct = vc_ref[0].astype(F32).T.astype(BF16)
    vis = (CMP_STRIDE * row + (CMP_LEN - 1) <= pos4) & (row < n_cmp)
    br = lax.broadcasted_iota(jnp.int32, (128, 128), 0) * SLC_BLK
    cc = lax.broadcasted_iota(jnp.int32, (128, 128), 1) * CMP_STRIDE
    overlap_t = ((cc < br + SLC_BLK) & (cc + CMP_LEN > br)).astype(F32)
    cur = pos // SLC_BLK
    forced = (row == 0) | (row == cur) | (row == cur - 1)
    for side in range(2):
        s = jnp.where(vis, _dot_nt(kc, qz_ref[side]), NEG)
        e = jnp.exp(s - jnp.max(s, axis=0, keepdims=True))
        p = jnp.where(vis, e / jnp.sum(e, axis=0, keepdims=True), 0.0)
        ocmp_ref[side] = _dot(vct, p.astype(BF16))
        psum = p[:, 0:tq]
        for j in range(1, HPG):
            psum = psum + p[:, j * tq:(j + 1) * tq]
        ps = _dot(overlap_t, psum, precision=HI)
        ps = jnp.where(row <= cur, jnp.where(forced, FORCE, ps), NEG)
        sel = _select_blocks_t(ps[0:n_blk, :], n_blk)
        sel_ref[side] = jnp.concatenate([sel, jnp.zeros((128 - n_blk, tq), F32)], axis=0).astype(BF16)

    m_ref[...] = jnp.full(m_ref.shape, NEG, F32)
    acc_ref[...] = jnp.zeros(acc_ref.shape, F32)

    def update(idx, s, vts):
        m_old = m_ref[idx]
        m_new = jnp.maximum(m_old, jnp.max(s, axis=0, keepdims=True))
        p = jnp.exp(s - m_new).astype(BF16)
        acc_ref[idx] = jnp.exp(m_old - m_new) * acc_ref[idx] + _dot(vts, p)
        m_ref[idx] = m_new

    def slc_tile(t, diagonal):
        k0 = pl.multiple_of(t * tq, tq)
        k = ks_ref[0, pl.ds(k0, tq), :]
        vts = v_sides(vs_ref[0, pl.ds(k0, tq), :])
        kpos = k0 + lax.broadcasted_iota(jnp.int32, (tq, 1), 0)
        expand_t = (lax.broadcasted_iota(jnp.int32, (tq, 128), 1) == kpos // SLC_BLK).astype(BF16)
        for side in range(2):
            keep = _dot(expand_t, sel_ref[side])
            if diagonal:
                keep = jnp.where(kpos <= pos, keep, 0.0)
            bias = (keep - 1.0) * (-NEG)
            update(side, _dot_nt(k, qz_ref[2 + side]) + rep(bias), vts[side])

    def slc_body(t, carry):
        slc_tile(t, False)
        return carry

    lax.fori_loop(0, qi, slc_body, 0)
    slc_tile(qi, True)

    def win_tile(t, kind):
        k0 = pl.multiple_of(t * tq, tq)
        k = kw_ref[0, pl.ds(k0, tq), :]
        vts = v_sides(vw_ref[0, pl.ds(k0, tq), :])
        kpos = k0 + lax.broadcasted_iota(jnp.int32, (tq, 1), 0)
        mask = (kpos > pos - WINDOW) if kind == 0 else ((kpos <= pos) if kind == 2 else None)
        bias = None if mask is None else rep(jnp.where(mask, 0.0, NEG))
        for side in range(2):
            s = _dot_nt(k, qz_ref[2 + side])
            if bias is not None:
                s = s + bias
            update(2 + side, s, vts[side])

    n_back = WINDOW // tq
    for d in range(n_back, 0, -1):
        pl.when(qi >= d)(functools.partial(win_tile, qi - d, 0 if d == n_back else 1))
    win_tile(qi, 2)

    gates_t = _sigmoid(gd_ref[0]).T
    for j in range(HPG):
        sl = slice(j * tq, (j + 1) * tq)
        cols = []
        for side in range(2):
            lrow = HEAD_DIM if side == 0 else 0
            gi = (2 * gp + side) * (3 * HPG) + 3 * j
            g = [jnp.sum(jnp.where(row == gi + k, gates_t, 0.0), axis=0, keepdims=True)
                 for k in range(3)]
            a_s = acc_ref[side, :, sl]
            a_w = acc_ref[2 + side, :, sl]
            cols.append(g[0] * ocmp_ref[side, :, sl] + (g[1] / a_s[lrow:lrow + 1, :]) * a_s
                        + (g[2] / a_w[lrow:lrow + 1, :]) * a_w)
        o_ref[0, :, j * 128:(j + 1) * 128] = jnp.where(row_side[0], cols[0], cols[1]).T.astype(BF16)


def _attn_prompt(qn, qr, kc, vc, kvb, gd):
    b, t, _ = qn.shape
    tq = ATT_TQ
    assert WINDOW % tq == 0 and t % tq == 0 and t // SLC_BLK <= 128 and kc.shape[1] == 128
    tw = HPG * tq
    qspec = pl.BlockSpec((1, tq, 512), lambda i, g, q: (i, q, g))
    cspec = pl.BlockSpec((1, kc.shape[1], 128), lambda i, g, q: (i, 0, g))
    kvspec = lambda base: pl.BlockSpec((1, t, 128), lambda i, g, q: (i, 0, base + g))
    return pl.pallas_call(
        _attn_prompt_kernel,
        out_shape=jax.ShapeDtypeStruct((b, t, D_ATT), BF16),
        grid=(b, 2, t // tq),
        in_specs=[qspec, qspec, cspec, cspec, kvspec(0), kvspec(2), kvspec(4), kvspec(6),
                  pl.BlockSpec((1, tq, 128), lambda i, g, q: (i, q, 0))],
        out_specs=qspec,
        scratch_shapes=[pltpu.VMEM((4, tw, 128), BF16), pltpu.VMEM((2, 128, tq), BF16),
                        pltpu.VMEM((2, 128, tw), F32), pltpu.VMEM((4, 1, tw), F32),
                        pltpu.VMEM((4, 128, tw), F32)],
        compiler_params=_cparams(("parallel", "parallel", "arbitrary")),
        name="attn_prompt",
    )(qn, qr, kc, vc, kvb, kvb, kvb, kvb, gd)


ATS_PAGES = 16
N_ROWS_S = 128


def _attn_sample_kernel(pt_ref, *refs, n_steps, past, t_dec):
    npg = ATS_PAGES
    kpages = refs[0:2 * npg:2]
    vpages = refs[1:2 * npg:2]
    (qn_ref, qr_ref, kc_ref, vc_ref, kvn_ref, cwin_ref, wnew_ref, gt_ref,
     o_ref, sel_ref, ocmp_ref, m_ref, l_ref, acc_ref) = refs[2 * npg:]
    s_id = pl.program_id(1)
    nr = N_ROWS_S
    rows = lax.broadcasted_iota(jnp.int32, (nr, 1), 0)
    t_row = (rows % (t_dec * HPG)) // HPG
    pos = past + t_row
    qr = qr_ref[0]

    @pl.when(s_id == 0)
    def _():
        n_c = kc_ref.shape[1]
        n_blk = (past + t_dec + SLC_BLK - 1) // SLC_BLK
        bps = ATS_PAGES * PAGE // SLC_BLK
        nbl = -(-(n_steps * bps + 128) // 128) * 128
        cl = lax.broadcasted_iota(jnp.int32, (1, n_c), 1)
        vis = (CMP_STRIDE * cl + (CMP_LEN - 1) <= pos) & (cl < n_c - 1)
        s = jnp.where(vis, _dot_nt(qn_ref[0], kc_ref[0]) * SCALE, NEG)
        e = jnp.exp(s - jnp.max(s, axis=1, keepdims=True))
        p = jnp.where(vis, e / jnp.sum(e, axis=1, keepdims=True), 0.0)
        ocmp_ref[...] = _dot(p.astype(BF16), vc_ref[0])
        ng = nr // HPG
        gsum = (lax.broadcasted_iota(jnp.int32, (ng, nr), 1) // HPG
                == lax.broadcasted_iota(jnp.int32, (ng, nr), 0)).astype(F32)
        cr = lax.broadcasted_iota(jnp.int32, (n_c, nbl), 0) * CMP_STRIDE
        j0 = lax.broadcasted_iota(jnp.int32, (n_c, nbl), 1) * SLC_BLK
        overlap = ((cr < j0 + SLC_BLK) & (cr + CMP_LEN > j0)).astype(F32)
        ps = _dot(_dot(gsum, p, precision=HI), overlap, precision=HI)
        bl = lax.broadcasted_iota(jnp.int32, (1, nbl), 1)
        g_rows = lax.broadcasted_iota(jnp.int32, (ng, 1), 0)
        cur = (past + g_rows % t_dec) // SLC_BLK
        forced = (bl == 0) | (bl == cur) | (bl == cur - 1)
        ps = jnp.where(bl <= cur, jnp.where(forced, FORCE, ps), NEG)
        rank = jnp.zeros(ps.shape, F32)
        for i in range(n_blk):
            vi = ps[:, i:i + 1]
            rank = rank + ((vi > ps) | ((vi == ps) & (bl > i))).astype(F32)
        sel = jnp.where((rank < N_SELECT) & (bl < n_blk), 1.0, 0.0)
        gexp = (lax.broadcasted_iota(jnp.int32, (nr, ng), 0) // HPG
                == lax.broadcasted_iota(jnp.int32, (nr, ng), 1)).astype(F32)
        sel_rows = _dot(gexp, sel)
        for w in range(sel_ref.shape[0]):
            sel_ref[w] = sel_rows[:, bps * w:bps * w + 128].astype(BF16)
        m_ref[...] = jnp.full(m_ref.shape, NEG, F32)
        l_ref[...] = jnp.zeros(l_ref.shape, F32)
        acc_ref[...] = jnp.zeros(acc_ref.shape, F32)

    def update(s, mask, v, v_transposed=False):
        s = jnp.where(mask, s, NEG)
        m_old = m_ref[...]
        m_new = jnp.maximum(m_old, jnp.max(s, axis=1, keepdims=True))
        p = jnp.where(mask, jnp.exp(s - m_new), 0.0)
        alpha = jnp.exp(m_old - m_new)
        l_ref[...] = alpha * l_ref[...] + jnp.sum(p, axis=1, keepdims=True)
        pv = _dot_nt(p.astype(BF16), v) if v_transposed else _dot(p.astype(BF16), v)
        acc_ref[...] = alpha * acc_ref[...] + pv
        m_ref[...] = m_new

    nk = npg * PAGE
    k_t = jnp.concatenate([r[0] for r in kpages], axis=1).astype(BF16)
    v_t = jnp.concatenate([r[0] for r in vpages], axis=1).astype(BF16)
    expand = (lax.broadcasted_iota(jnp.int32, (128, nk), 0)
              == lax.broadcasted_iota(jnp.int32, (128, nk), 1) // SLC_BLK).astype(BF16)
    kpos = s_id * nk + lax.broadcasted_iota(jnp.int32, (1, nk), 1)
    mask = (_dot(sel_ref[s_id], expand) > 0.5) & (kpos <= pos)
    update(_dot(qr, k_t) * SCALE, mask, v_t, v_transposed=True)

    @pl.when(s_id == n_steps - 1)
    def _():
        kn = kvn_ref[0, :, 512:768].astype(BF16)
        vn = kvn_ref[0, :, 768:1024].astype(BF16)
        tk = lax.broadcasted_iota(jnp.int32, (1, t_dec), 1)
        cur_sel = sel_ref[n_steps][:, 0:1].astype(F32) > 0.5
        update(_dot_nt(qr, kn) * SCALE, cur_sel & (past + tk <= pos), vn)
        o_slc = acc_ref[...] / l_ref[...]
        wb = cwin_ref.shape[1]
        kw = cwin_ref[0, :, 0:256].astype(BF16)
        vw = cwin_ref[0, :, 256:512].astype(BF16)
        kwn = wnew_ref[0, :, 0:256].astype(BF16)
        vwn = wnew_ref[0, :, 256:512].astype(BF16)
        d1 = pos - (past - wb + lax.broadcasted_iota(jnp.int32, (1, wb), 1))
        d2 = pos - (past + tk)
        m1 = (d1 >= 0) & (d1 < WINDOW)
        m2 = (d2 >= 0) & (d2 < WINDOW)
        s1 = jnp.where(m1, _dot_nt(qr, kw) * SCALE, NEG)
        s2 = jnp.where(m2, _dot_nt(qr, kwn) * SCALE, NEG)
        mx = jnp.maximum(jnp.max(s1, axis=1, keepdims=True), jnp.max(s2, axis=1, keepdims=True))
        p1 = jnp.where(m1, jnp.exp(s1 - mx), 0.0)
        p2 = jnp.where(m2, jnp.exp(s2 - mx), 0.0)
        den = jnp.sum(p1, axis=1, keepdims=True) + jnp.sum(p2, axis=1, keepdims=True)
        o_win = (_dot(p1.astype(BF16), vw) + _dot(p2.astype(BF16), vwn)) / den
        g = _sigmoid(gt_ref[0])
        o_ref[0] = g[:, 0:1] * ocmp_ref[...] + g[:, 1:2] * o_slc + g[:, 2:3] * o_win


def _attn_sample(cache_t, page_table, qn_x, qr_x, kc, vc, kv_new, cache_win2, win_new, gt_rows, t_dec):
    b, n_pages = page_table.shape
    past = n_pages * PAGE
    n_steps = n_pages // ATS_PAGES
    nr = N_ROWS_S

    def pspec(k, blk):
        return pl.BlockSpec((1, 256, PAGE), lambda i, s, pt: (pt[i, s * ATS_PAGES + k], blk, 0))

    per_b = lambda r, w: pl.BlockSpec((1, r, w), lambda i, s, pt: (i, 0, 0))
    page_specs = []
    for k in range(ATS_PAGES):
        page_specs += [pspec(k, 2), pspec(k, 3)]
    gs = pltpu.PrefetchScalarGridSpec(
        num_scalar_prefetch=1,
        grid=(b, n_steps),
        in_specs=page_specs + [per_b(nr, 256), per_b(nr, 256), per_b(kc.shape[1], 256),
                               per_b(kc.shape[1], 256), per_b(t_dec, 1024),
                               per_b(cache_win2.shape[1], 512), per_b(t_dec, 512), per_b(nr, 128)],
        out_specs=per_b(nr, 256),
        scratch_shapes=[pltpu.VMEM((n_steps + 1, nr, 128), BF16), pltpu.VMEM((nr, 256), F32),
                        pltpu.VMEM((nr, 1), F32), pltpu.VMEM((nr, 1), F32),
                        pltpu.VMEM((nr, 256), F32)],
    )
    return pl.pallas_call(
        functools.partial(_attn_sample_kernel, n_steps=n_steps, past=past, t_dec=t_dec),
        out_shape=jax.ShapeDtypeStruct((b, nr, 256), F32),
        grid_spec=gs,
        compiler_params=_cparams(("parallel", "arbitrary")),
        name="attn_sample",
    )(page_table, *([cache_t] * (2 * ATS_PAGES)), qn_x, qr_x, kc, vc, kv_new, cache_win2, win_new,
      gt_rows)


def _expand_rows(q):
    b, t, _ = q.shape
    q5 = q.reshape(b, t, N_KV, HPG, HEAD_DIM)
    eye = jnp.eye(N_KV, dtype=q.dtype)
    return jnp.einsum('btghd,gk->bgthkd', q5, eye).reshape(b, N_KV * t * HPG, N_KV * HEAD_DIM)


def _collapse_rows(o, t):
    b = o.shape[0]
    o6 = o.reshape(b, N_KV, t, HPG, N_KV, HEAD_DIM)
    return jnp.einsum('bgthkd,gk->btghd', o6, jnp.eye(N_KV, dtype=o.dtype)).reshape(b, t, D_ATT)


def _layernorm(x, g, b):
    mu = jnp.mean(x, axis=-1, keepdims=True)
    xc = x - mu
    var = jnp.mean(xc * xc, axis=-1, keepdims=True)
    return xc * lax.rsqrt(var + LN_EPS) * g + b


OUT_SUB = 256


def _outproj_kernel(oa_ref, ys_ref, x_ref, g1_ref, sc2_ref, sh2_ref, w_ref, lg_ref, lb_ref,
                    rw_ref, rb_ref, x1_ref, h2_ref, ti_ref, tw_ref):
    tm = x_ref.shape[1]
    sub = min(OUT_SUB, tm)
    for r0 in range(0, tm, sub):
        rs = slice(r0, r0 + sub)
        mix = _dot(oa_ref[0, rs, :], w_ref[0:D_ATT, :]) + _dot(ys_ref[0, rs, :], w_ref[D_ATT:, :])
        x1 = _layernorm(DN_ALPHA * x_ref[0, rs, :] + g1_ref[0] * mix, lg_ref[...], lb_ref[...])
        x1_ref[0, rs, :] = x1
        h2 = x1 * (1.0 + sc2_ref[0]) + sh2_ref[0]
        h_hi = h2.astype(BF16)
        h2_ref[0, rs, :] = _pack_pairs(h2)
        h_lo = (h2 - h_hi.astype(F32)).astype(BF16)
        logits = (_dot(h_hi, rw_ref[0]) + (_dot(h_lo, rw_ref[0]) + _dot(h_hi, rw_ref[1]))
                  + rb_ref[...])
        lane = lax.broadcasted_iota(jnp.int32, logits.shape, 1)
        vals, ids = [], []
        for _ in range(TOP_K):
            mx = jnp.max(logits, axis=1, keepdims=True)
            ix = jnp.min(jnp.where(logits == mx, lane, 128), axis=1, keepdims=True)
            vals.append(mx)
            ids.append(ix)
            logits = jnp.where(lane == ix, -jnp.inf, logits)
        es = [jnp.exp(v - vals[0]) for v in vals]
        inv = 1.0 / (es[0] + es[1] + es[2] + es[3])
        ti = jnp.zeros(lane.shape, jnp.int32)
        tw = jnp.zeros(lane.shape, F32)
        for k in range(TOP_K):
            ti = jnp.where(lane == k, ids[k], ti)
            tw = jnp.where(lane == k, es[k] * inv, tw)
        ti_ref[0, rs, :] = ti
        tw_ref[0, rs, :] = tw


def _out_proj(o_att, y_ssm, x, g1, sc2, sh2, w_out_bf, ln_g, ln_b, rw, rb, tm):
    b, t, _ = x.shape
    row = lambda w: pl.BlockSpec((1, tm, w), lambda i, j: (i, j, 0))
    per_b = pl.BlockSpec((1, 1, D_MODEL), lambda i, j: (i, 0, 0))
    par = lambda r, w, **kw: pl.BlockSpec((r, w), lambda i, j: (0, 0), **kw)
    return pl.pallas_call(
        _outproj_kernel,
        out_shape=[jax.ShapeDtypeStruct((b, t, D_MODEL), F32),
                   jax.ShapeDtypeStruct((b, t, D_MODEL // 2), jnp.uint32),
                   jax.ShapeDtypeStruct((b, t, 128), jnp.int32),
                   jax.ShapeDtypeStruct((b, t, 128), F32)],
        grid=(b, t // tm),
        in_specs=[row(D_ATT), row(D_SSM), row(D_MODEL), per_b, per_b, per_b,
                  par(D_MODEL, D_MODEL, pipeline_mode=pl.Buffered(1)),
                  par(1, D_MODEL), par(1, D_MODEL),
                  pl.BlockSpec((2, D_MODEL, 128), lambda i, j: (0, 0, 0)), par(1, 128)],
        out_specs=[row(D_MODEL), row(D_MODEL // 2), row(128), row(128)],
        compiler_params=_cparams(("parallel", "parallel")),
        name="out_proj",
    )(o_att, y_ssm, x, g1, sc2, sh2, w_out_bf, ln_g, ln_b, rw, rb)


MOE_TM = 1024
MOE_TN = 512
MOE_TN_DOWN = 1024
ROUTE_TM = 256


def _route_kernel(ti_ref, rank_ref, cnt_ref):
    @pl.when(pl.program_id(0) == 0)
    def _():
        cnt_ref[...] = jnp.zeros_like(cnt_ref)

    ti = ti_ref[...]
    tm = ti.shape[0]
    lane = lax.broadcasted_iota(jnp.int32, (tm, 128), 1)
    hits = [lane == ti[:, k:k + 1] for k in range(TOP_K)]
    oh = jnp.zeros((tm, 128), F32)
    for h in hits:
        oh = oh + h.astype(F32)
    ri = lax.broadcasted_iota(jnp.int32, (tm, tm), 0)
    ci = lax.broadcasted_iota(jnp.int32, (tm, tm), 1)
    before = _dot((ri > ci).astype(BF16), oh.astype(BF16)) + cnt_ref[0:1, :]
    rank = jnp.zeros((tm, 128), jnp.int32)
    for k, h in enumerate(hits):
        rk = jnp.sum(jnp.where(h, before, 0.0), axis=1, keepdims=True).astype(jnp.int32)
        rank = jnp.where(lane == k, rk, rank)
    rank_ref[...] = rank
    cnt_ref[...] = cnt_ref[...] + jnp.sum(oh, axis=0, keepdims=True)


def _route(ti_all):
    n = ti_all.shape[0]
    return pl.pallas_call(
        _route_kernel,
        out_shape=[jax.ShapeDtypeStruct((n, 128), jnp.int32),
                   jax.ShapeDtypeStruct((8, 128), F32)],
        grid=(n // ROUTE_TM,),
        in_specs=[pl.BlockSpec((ROUTE_TM, 128), lambda i: (i, 0))],
        out_specs=[pl.BlockSpec((ROUTE_TM, 128), lambda i: (i, 0)),
                   pl.BlockSpec((8, 128), lambda i: (0, 0))],
        compiler_params=_cparams(("arbitrary",)),
        name="moe_route",
    )(ti_all)


DMA_UNROLL = 8


def _dispatch_kernel(pos_ref, h_ref, xs_in_ref, xs_ref, sem):
    del xs_in_ref
    def copy(t, k):
        return pltpu.make_async_copy(h_ref.at[pl.ds(t, 1), :],
                                     xs_ref.at[pl.ds(pos_ref[0, 0, t * TOP_K + k], 1), :], sem)

    def start(t, c):
        for k in range(TOP_K):
            copy(t, k).start()
        return c

    def wait(t, c):
        for k in range(TOP_K):
            copy(t, k).wait()
        return c

    lax.fori_loop(0, ROUTE_TM, start, 0, unroll=DMA_UNROLL // TOP_K)
    lax.fori_loop(0, ROUTE_TM, wait, 0, unroll=DMA_UNROLL // TOP_K)


PAD_CHUNK = 64


def _zero_pad_kernel(start_ref, n_ref, xs_ref, zero_ref, sem):
    e = pl.program_id(0)
    zero_ref[...] = jnp.zeros(zero_ref.shape, zero_ref.dtype)

    def copy(c):
        row0 = pl.multiple_of((start_ref[e] + c) * PAD_CHUNK, PAD_CHUNK)
        return pltpu.make_async_copy(zero_ref, xs_ref.at[pl.ds(row0, PAD_CHUNK), :], sem)

    def start(c, carry):
        copy(c).start()
        return carry

    def wait(c, carry):
        copy(c).wait()
        return carry

    lax.fori_loop(0, n_ref[e], start, 0)
    lax.fori_loop(0, n_ref[e], wait, 0)


def _zero_pad(chunk_start, n_chunks, n_rows, width):
    gs = pltpu.PrefetchScalarGridSpec(
        num_scalar_prefetch=2,
        grid=(N_EXPERTS,),
        in_specs=[],
        out_specs=pl.BlockSpec(memory_space=pl.ANY),
        scratch_shapes=[pltpu.VMEM((PAD_CHUNK, width), jnp.uint32), pltpu.SemaphoreType.DMA(())],
    )
    return pl.pallas_call(
        _zero_pad_kernel,
        out_shape=jax.ShapeDtypeStruct((n_rows, width), jnp.uint32),
        grid_spec=gs,
        compiler_params=_cparams(("arbitrary",)),
        name="moe_zero_pad",
    )(chunk_start, n_chunks)


def _dispatch(pos, h, xs):
    n, w = h.shape
    assert n % ROUTE_TM == 0
    steps = n // ROUTE_TM
    return pl.pallas_call(
        _dispatch_kernel,
        out_shape=jax.ShapeDtypeStruct(xs.shape, xs.dtype),
        grid=(steps,),
        in_specs=[pl.BlockSpec((1, 1, ROUTE_TM * TOP_K), lambda i: (i, 0, 0), memory_space=pltpu.SMEM),
                  pl.BlockSpec((ROUTE_TM, w), lambda i: (i, 0)),
                  pl.BlockSpec(memory_space=pl.ANY)],
        out_specs=pl.BlockSpec(memory_space=pl.ANY),
        scratch_shapes=[pltpu.SemaphoreType.DMA(())],
        input_output_aliases={2: 0},
        compiler_params=_cparams(("arbitrary",), no_bounds_checks=True),
        name="moe_dispatch",
    )(pos.reshape(steps, 1, ROUTE_TM * TOP_K), h, xs)


def _moe_up_kernel(te_ref, nt_ref, tv_ref, xs_ref, wg_ref, wu_ref, bg_ref, bu_ref, act_ref, x_ref):
    hk = D_MODEL // 2
    live = tv_ref[pl.program_id(0)] > 0

    @pl.when(live & (pl.program_id(1) == 0))
    def _():
        lo, hi = _unpack_pairs(xs_ref[...])
        x_ref[0] = lo.astype(BF16)
        x_ref[1] = hi.astype(BF16)

    @pl.when(live)
    def _():
        lo = x_ref[0]
        hi = x_ref[1]
        wg = wg_ref[0].astype(BF16)
        wu = wu_ref[0].astype(BF16)
        g = _dot(lo, wg[:hk]) + _dot(hi, wg[hk:]) + bg_ref[0]
        u = _dot(lo, wu[:hk]) + _dot(hi, wu[hk:]) + bu_ref[0]
        g = jnp.minimum(g, SWIGLU_LIMIT)
        u = jnp.clip(u, -SWIGLU_LIMIT, SWIGLU_LIMIT)
        act_ref[...] = ((u + 1.0) * g * _sigmoid(SWIGLU_ALPHA * g)).astype(BF16)

    @pl.when(tv_ref[pl.program_id(0)] <= 0)
    def _():
        act_ref[...] = jnp.zeros_like(act_ref)


def _moe_up(tile_e, n_used, tile_valid, xs, w_gu, b_gu3):
    n_rows = xs.shape[0]
    nj = D_FF // MOE_TN
    row_i = lambda i, j, te, nt, tv: (jnp.minimum(i, nt[0] - 1), 0)
    jc = lambda i, j, nt: jnp.where(i < nt[0], j, nj - 1)
    wspec = lambda off: pl.BlockSpec((1, D_MODEL, MOE_TN),
                                     lambda i, j, te, nt, tv: (te[i], 0, jc(i, j, nt) + off))
    bspec = lambda off: pl.BlockSpec((1, 1, MOE_TN),
                                     lambda i, j, te, nt, tv: (te[i], 0, jc(i, j, nt) + off))
    gs = pltpu.PrefetchScalarGridSpec(
        num_scalar_prefetch=3,
        grid=(n_rows // MOE_TM, nj),
        in_specs=[pl.BlockSpec((MOE_TM, D_MODEL // 2), row_i), wspec(0), wspec(nj), bspec(0), bspec(nj)],
        out_specs=pl.BlockSpec((MOE_TM, MOE_TN), lambda i, j, te, nt, tv: (i, j)),
        scratch_shapes=[pltpu.VMEM((2, MOE_TM, D_MODEL // 2), BF16)],
    )
    return pl.pallas_call(
        _moe_up_kernel,
        out_shape=jax.ShapeDtypeStruct((n_rows, D_FF), BF16),
        grid_spec=gs,
        compiler_params=_cparams(("parallel", "arbitrary")),
        name="moe_up",
    )(tile_e, n_used, tile_valid, xs, w_gu, w_gu, b_gu3, b_gu3)


def _moe_down_kernel(te_ref, nt_ref, tv_ref, a_ref, w_ref, b_ref, y_ref):
    @pl.when(tv_ref[pl.program_id(0)] > 0)
    def _():
        y_ref[...] = _pack_pairs(_dot(a_ref[...], w_ref[0].astype(BF16)) + b_ref[0])

    @pl.when(tv_ref[pl.program_id(0)] <= 0)
    def _():
        y_ref[...] = jnp.zeros_like(y_ref)


def _moe_down(tile_e, n_used, tile_valid, act, w_down, b_down3):
    n_rows = act.shape[0]
    tn = MOE_TN_DOWN
    nj = D_MODEL // tn
    row_i = lambda i, j, te, nt, tv: (jnp.minimum(i, nt[0] - 1), 0)
    jc = lambda i, j, nt: jnp.where(i < nt[0], j, nj - 1)
    gs = pltpu.PrefetchScalarGridSpec(
        num_scalar_prefetch=3,
        grid=(n_rows // MOE_TM, nj),
        in_specs=[pl.BlockSpec((MOE_TM, D_FF), row_i),
                  pl.BlockSpec((1, D_FF, tn), lambda i, j, te, nt, tv: (te[i], 0, jc(i, j, nt))),
                  pl.BlockSpec((1, 1, tn), lambda i, j, te, nt, tv: (te[i], 0, jc(i, j, nt)))],
        out_specs=pl.BlockSpec((MOE_TM, tn // 2), lambda i, j, te, nt, tv: (i, j)),
    )
    return pl.pallas_call(
        _moe_down_kernel,
        out_shape=jax.ShapeDtypeStruct((n_rows, D_MODEL // 2), jnp.uint32),
        grid_spec=gs,
        compiler_params=_cparams(("parallel", "arbitrary")),
        name="moe_down",
    )(tile_e, n_used, tile_valid, act, w_down, b_down3)


def _combine_kernel(pos_ref, posn_ref, yd_ref, x1_ref, tw_ref, g2_ref, lg_ref, lb_ref, o_ref, buf, sem):
    tm = buf.shape[2]
    step = pl.program_id(0) * pl.num_programs(1) + pl.program_id(1)
    n_steps = pl.num_programs(0) * pl.num_programs(1)
    slot = step % 2

    def copy(p_ref, sl, t, k):
        return pltpu.make_async_copy(yd_ref.at[pl.ds(p_ref[0, 0, t * TOP_K + k], 1), :],
                                     buf.at[sl, k, pl.ds(t, 1), :], sem.at[sl])

    def request(p_ref, sl):
        def body(t, c):
            for k in range(TOP_K):
                copy(p_ref, sl, t, k).start()
            return c
        lax.fori_loop(0, tm, body, 0, unroll=DMA_UNROLL // TOP_K)

    @pl.when(step == 0)
    def _():
        request(pos_ref, 0)

    @pl.when(step + 1 < n_steps)
    def _():
        request(posn_ref, 1 - slot)

    def wait(t, c):
        for k in range(TOP_K):
            copy(pos_ref, slot, t, k).wait()
        return c

    lax.fori_loop(0, tm, wait, 0, unroll=DMA_UNROLL // TOP_K)
    tw = tw_ref[0]
    hw = MOE_TN_DOWN // 2
    moe = None
    for k in range(TOP_K):
        parts = []
        for j in range(buf.shape[3] // hw):
            parts += list(_unpack_pairs(buf[slot, k, :, j * hw:(j + 1) * hw]))
        term = jnp.concatenate(parts, axis=1) * tw[:, k:k + 1]
        moe = term if moe is None else moe + term
    o_ref[0] = _layernorm(DN_ALPHA * x1_ref[0] + g2_ref[0] * moe, lg_ref[...], lb_ref[...])


def _combine(pos, yd, x1, tw, g2, ln_g, ln_b, tok0):
    b, t, _ = x1.shape
    tm = min(ROUTE_TM, t)
    per = t // tm
    step0 = tok0 // tm
    last = step0 + b * per - 1
    row = lambda w: pl.BlockSpec((1, tm, w), lambda i, j: (i, j, 0))
    par = pl.BlockSpec((1, D_MODEL), lambda i, j: (0, 0))
    pspec = lambda ahead: pl.BlockSpec(
        (1, 1, tm * TOP_K), lambda i, j: (jnp.minimum(step0 + i * per + j + ahead, last), 0, 0),
        memory_space=pltpu.SMEM)
    pos3 = pos.reshape(-1, 1, tm * TOP_K)
    return pl.pallas_call(
        _combine_kernel,
        out_shape=jax.ShapeDtypeStruct((b, t, D_MODEL), F32),
        grid=(b, per),
        in_specs=[pspec(0), pspec(1),
                  pl.BlockSpec(memory_space=pl.ANY),
                  row(D_MODEL), row(128),
                  pl.BlockSpec((1, 1, D_MODEL), lambda i, j: (i, 0, 0)), par, par],
        out_specs=row(D_MODEL),
        scratch_shapes=[pltpu.VMEM((2, TOP_K, tm, D_MODEL // 2), jnp.uint32),
                        pltpu.SemaphoreType.DMA((2,))],
        compiler_params=_cparams(("arbitrary", "arbitrary"), no_bounds_checks=True),
        name="moe_combine",
    )(pos3, pos3, yd, x1, tw, g2, ln_g, ln_b)


IN_TM = 256
SSD_Q = 256
OUT_TM = 512


def kernel(x_prompt, x_sample, cache_kv, cache_win, state_conv, state_ssm, page_table,
           c_prompt, c_sample, w_ada, b_ada, w_in, cmp_pe, cmp_w1, cmp_b1, cmp_w2,
           conv_w, conv_b, dt_bias, a_log, d_skip, ssm_norm_g, w_out, ln1_g, ln1_b,
           router_w, router_b, w_gu, b_gu, w_down, b_down, ln2_g, ln2_b):
    bp, tp, _ = x_prompt.shape
    bs, ts, _ = x_sample.shape
    past = page_table.shape[1] * PAGE

    m = _ada(jnp.concatenate([c_prompt, c_sample], axis=0), w_ada[0], b_ada[0])
    mod = [m[:, None, k * D_MODEL:(k + 1) * D_MODEL] for k in range(6)]
    mod_p = [a[:bp] for a in mod]
    mod_s = [a[bp:] for a in mod]

    perm = _q_perm()
    inv_perm = np.argsort(perm)
    w_bf = _prep_w_in(w_in[0])
    cw = _prep_cmp_weights(cmp_pe[0], cmp_w1[0], cmp_b1[0], cmp_w2[0])
    sp = _prep_ssm_params(conv_w[0], conv_b[0], dt_bias[0], a_log[0], d_skip[0], ssm_norm_g[0])
    w_out_bf = jnp.concatenate([w_out[0][:D_ATT][perm], w_out[0][D_ATT:]], axis=0).astype(BF16)
    rw = jnp.concatenate([router_w[0], jnp.zeros((D_MODEL, 128 - N_EXPERTS), F32)], axis=1)
    rw_hi = rw.astype(BF16)
    rw = jnp.stack([rw_hi, (rw - rw_hi.astype(F32)).astype(BF16)])
    rb = jnp.concatenate([router_b[0], jnp.full((128 - N_EXPERTS,), NEG, F32)]).reshape(1, 128)
    ln1 = (ln1_g[0].reshape(1, D_MODEL), ln1_b[0].reshape(1, D_MODEL))
    ln2 = (ln2_g[0].reshape(1, D_MODEL), ln2_b[0].reshape(1, D_MODEL))

    tab_p = _rope_tables(jnp.arange(tp))
    qn, qr, kv_p, win_p, kvb, z, xbc, gd = _in_proj(x_prompt, mod_p[1], mod_p[0], w_bf, tab_p, IN_TM)
    kc, vc = _compress_prompt(kv_p, cw)
    o_att = _attn_prompt(qn, qr, kc, vc, kvb, gd)
    y_ssm, ssm_p = _ssd(xbc, z, gd, jnp.zeros((bp, 8, CONV_CH), F32),
                        jnp.zeros((bp, D_SSM, D_STATE), F32), sp, SSD_Q)
    x1_p, h2_p, ti_p, tw_p = _out_proj(o_att, y_ssm, x_prompt, mod_p[2], mod_p[4], mod_p[3],
                                       w_out_bf, *ln1, rw, rb, OUT_TM)

    tab_s = _rope_tables(past + jnp.arange(ts))
    n_s = bs * ts
    per_tok = lambda a: jnp.broadcast_to(a, (bs, ts, D_MODEL)).reshape(1, n_s, D_MODEL)
    s_outs = _in_proj(x_sample.reshape(1, n_s, D_MODEL), per_tok(mod_s[1]), per_tok(mod_s[0]),
                      w_bf, jnp.tile(tab_s, (bs, 1)), n_s)
    qn_s, qr_s, kv_s, win_s, _, z_s, xbc_s, gd_s = [a.reshape(bs, ts, a.shape[-1]) for a in s_outs]
    cache_t = jnp.transpose(cache_kv[0], (0, 2, 3, 4, 1)).reshape(
        cache_kv.shape[1], 4 * N_KV * HEAD_DIM, PAGE)
    kc_s, vc_s = _compress_sample(cache_t, page_table, cw)
    wb = cache_win.shape[2]
    cwin2 = cache_win[0].reshape(bs, wb, 2 * N_KV * HEAD_DIM)
    gt_rows = gd_s[:, :, :D_GT].reshape(bs, ts, N_KV, HPG, 3).transpose(0, 2, 1, 3, 4)
    gt_rows = jnp.concatenate([gt_rows.reshape(bs, N_ROWS_S, 3),
                               jnp.zeros((bs, N_ROWS_S, 125), F32)], axis=-1)
    o_rows = _attn_sample(cache_t, page_table, _expand_rows(qn_s[:, :, inv_perm]),
                          _expand_rows(qr_s[:, :, inv_perm]), kc_s, vc_s, kv_s, cwin2, win_s,
                          gt_rows, ts)
    o_att_s = _collapse_rows(o_rows, ts)[:, :, perm].astype(BF16)
    cprev = jnp.concatenate([jnp.zeros((bs, 5, CONV_CH), F32), state_conv[0]], axis=1)
    y_ssm_s, ssm_s = _ssd(xbc_s, z_s, gd_s, cprev, state_ssm[0].reshape(bs, D_SSM, D_STATE), sp, ts)
    x1_s, h2_s, ti_s, tw_s = _out_proj(o_att_s, y_ssm_s, x_sample, mod_s[2], mod_s[4], mod_s[3],
                                       w_out_bf, *ln1, rw, rb, ts)

    n_tok = bp * tp + bs * ts
    n_tiles = -(-(n_tok * TOP_K + N_EXPERTS * (MOE_TM - 1)) // MOE_TM)
    ti_all = jnp.concatenate([ti_p.reshape(-1, 128), ti_s.reshape(-1, 128)], axis=0)
    rank, cnt = _route(ti_all)
    counts = cnt[0, :N_EXPERTS].astype(jnp.int32)
    padded = (counts + MOE_TM - 1) // MOE_TM * MOE_TM
    ends = jnp.cumsum(padded)
    offs = ends - padded
    n_used = (ends[-1] // MOE_TM).astype(jnp.int32).reshape(1)
    tiles = jnp.minimum(jnp.arange(n_tiles, dtype=jnp.int32), n_used[0] - 1)
    tile_e = jnp.sum((tiles[:, None] * MOE_TM >= ends[None, :]).astype(jnp.int32), axis=1)
    tile_e = jnp.minimum(tile_e, N_EXPERTS - 1).astype(jnp.int32)
    first_tile = offs // MOE_TM
    tile_valid = jnp.clip(counts[tile_e] - (tiles - first_tile[tile_e]) * MOE_TM, 0, MOE_TM)
    tile_valid = jnp.where(jnp.arange(n_tiles) < n_used[0], tile_valid, 0).astype(jnp.int32)
    pos = (offs[ti_all[:, :TOP_K]] + rank[:, :TOP_K]).astype(jnp.int32)
    n_p = bp * tp
    chunk_start = ((offs + counts) // PAD_CHUNK).astype(jnp.int32)
    n_chunks = (ends // PAD_CHUNK - chunk_start).astype(jnp.int32)
    xs = _zero_pad(chunk_start, n_chunks, n_tiles * MOE_TM, D_MODEL // 2)
    xs = _dispatch(pos[:n_p], h2_p.reshape(n_p, D_MODEL // 2), xs)
    xs = _dispatch(pos[n_p:], h2_s.reshape(bs * ts, D_MODEL // 2), xs)
    act = _moe_up(tile_e, n_used, tile_valid, xs, w_gu[0], b_gu[0].reshape(N_EXPERTS, 1, 2 * D_FF))
    yd = _moe_down(tile_e, n_used, tile_valid, act, w_down[0],
                   b_down[0].reshape(N_EXPERTS, 1, D_MODEL))
    y_p = _combine(pos, yd, x1_p, tw_p, mod_p[5], *ln2, 0)
    y_s = _combine(pos, yd, x1_s, tw_s, mod_s[5], *ln2, bp * tp)

    kv_shape = (4, N_KV, HEAD_DIM)
    win_shape = (2, N_KV, HEAD_DIM)
    win_prompt = win_p[:, tp - min(WINDOW, tp):].reshape((1, bp, min(WINDOW, tp)) + win_shape)
    win_sample = jnp.concatenate([cwin2, win_s], axis=1)[:, -wb:].reshape((1, bs, wb) + win_shape)
    return (y_p, y_s,
            kv_p.reshape((1, bp, tp) + kv_shape), kv_s.reshape((1, bs, ts) + kv_shape),
            win_prompt, win_sample,
            xbc[None, :, tp - (CONV_W - 1):], xbc_s[None, :, ts - (CONV_W - 1):],
            ssm_p.reshape(1, bp, N_HEADS_SSM, HEAD_DIM, D_STATE),
            ssm_s.reshape(1, bs, N_HEADS_SSM, HEAD_DIM, D_STATE))
```

```python
import functools
import math

import jax
import jax.numpy as jnp
import numpy as np
from jax import lax
from jax.experimental import pallas as pl
from jax.experimental.pallas import tpu as pltpu

D_MODEL = 2048
D_ATT = 1024
D_SSM = 1024
HEAD_DIM = 64
N_HEADS_ATT = 16
N_KV = 4
HPG = 4
ROT_DIM = 16
ROPE_THETA = 500000.0
CMP_LEN = 32
CMP_STRIDE = 16
CMP_HID = 128
SLC_BLK = 64
N_SELECT = 16
WINDOW = 512
N_HEADS_SSM = 16
SSM_GROUPS = 4
D_STATE = 128
CONV_W = 4
CONV_CH = 2048
N_EXPERTS = 32
TOP_K = 4
D_FF = 2048
SWIGLU_LIMIT = 7.0
SWIGLU_ALPHA = 1.702
DN_ALPHA = 2.0 ** 0.25
LN_EPS = 1e-5
RMS_EPS = 1e-5
NEG = -1e30
FORCE = 1e6
PAGE = 128

D_KV = 6 * N_KV * HEAD_DIM
D_GT = 3 * N_HEADS_ATT
D_IN = D_ATT + D_KV + D_GT + D_SSM + CONV_CH + N_HEADS_SSM
D_INP = D_ATT + D_KV + D_SSM + CONV_CH + 128

LANES = 128
VMEM_LIMIT = 56 * 1024 * 1024

F32 = jnp.float32
BF16 = jnp.bfloat16
HI = lax.Precision.HIGHEST


def _cparams(sem, vmem=VMEM_LIMIT, no_bounds_checks=False):
    return pltpu.CompilerParams(dimension_semantics=sem, vmem_limit_bytes=vmem,
                                disable_bounds_checks=no_bounds_checks)


def _dot(a, b, precision=None):
    return jnp.dot(a, b, preferred_element_type=F32, precision=precision)


def _dot_nt(a, b, precision=None):
    return lax.dot_general(a, b, (((1,), (1,)), ((), ())), preferred_element_type=F32,
                           precision=precision)


def _dot_tn(a, b, precision=None):
    return lax.dot_general(a, b, (((0,), (0,)), ((), ())), preferred_element_type=F32,
                           precision=precision)


def _split3(x):
    x1 = x.astype(BF16)
    r = x - x1.astype(F32)
    x2 = r.astype(BF16)
    return x1, x2, (r - x2.astype(F32)).astype(BF16)


def _dot_sel_l(sel, x):
    s = sel.astype(BF16)
    x1, x2, x3 = _split3(x)
    return _dot(s, x1) + (_dot(s, x2) + _dot(s, x3))


def _dot_sel_r(x, sel):
    s = sel.astype(BF16)
    x1, x2, x3 = _split3(x)
    return _dot(x1, s) + (_dot(x2, s) + _dot(x3, s))


def _pack_pairs(x):
    h = x.shape[1] // 2
    bits = lax.bitcast_convert_type(x.astype(BF16).astype(F32), jnp.uint32)
    return (bits[:, :h] >> 16) | (bits[:, h:] & jnp.uint32(0xFFFF0000))


def _unpack_pairs(w):
    lo = lax.bitcast_convert_type(w << 16, F32)
    hi = lax.bitcast_convert_type(w & jnp.uint32(0xFFFF0000), F32)
    return lo, hi


def _sigmoid(x):
    return 1.0 / (1.0 + jnp.exp(-x))


def _silu(x):
    return x * _sigmoid(x)


def _ada_kernel(c_ref, w_ref, b_ref, o_ref):
    c = c_ref[...]
    a = _silu(c).astype(BF16)
    o_ref[...] = _dot(a, w_ref[...].astype(BF16)) + b_ref[...]


def _ada(c_all, w_ada, b_ada):
    nb = c_all.shape[0]
    tn = 1024
    n = w_ada.shape[1]
    return pl.pallas_call(
        _ada_kernel,
        out_shape=jax.ShapeDtypeStruct((nb, n), F32),
        grid=(n // tn,),
        in_specs=[pl.BlockSpec((nb, D_MODEL), lambda j: (0, 0)),
                  pl.BlockSpec((D_MODEL, tn), lambda j: (0, j)),
                  pl.BlockSpec((1, tn), lambda j: (0, j))],
        out_specs=pl.BlockSpec((nb, tn), lambda j: (0, j)),
        compiler_params=_cparams(("parallel",)),
        name="ada",
    )(c_all, w_ada, b_ada.reshape(1, n))


def _q_perm():
    idx = np.zeros(D_ATT, np.int32)
    for gp in range(2):
        for j in range(HPG):
            for side in range(2):
                g = 2 * gp + side
                for d in range(HEAD_DIM):
                    idx[gp * 512 + j * 128 + side * 64 + d] = g * 256 + j * 64 + d
    return idx


def _prep_w_in(w_in):
    o_kv = D_ATT
    o_gt = o_kv + D_KV
    o_z = o_gt + D_GT
    o_xbc = o_z + D_SSM
    o_dt = o_xbc + CONV_CH
    wq = w_in[:, :D_ATT][:, _q_perm()]
    pad = jnp.zeros((D_MODEL, 128 - D_GT - N_HEADS_SSM), w_in.dtype)
    w = jnp.concatenate([wq, w_in[:, o_kv:o_gt], w_in[:, o_z:o_xbc], w_in[:, o_xbc:o_dt],
                         w_in[:, o_gt:o_z], w_in[:, o_dt:], pad], axis=1)
    return w.astype(BF16)


def _rope_tables(pos):
    half = ROT_DIM // 2
    inv = ROPE_THETA ** (-jnp.arange(half, dtype=F32) / half)
    ang = pos.astype(F32)[:, None] * inv
    cos, sin = jnp.cos(ang), jnp.sin(ang)
    t = pos.shape[0]
    one = jnp.ones((t, HEAD_DIM - ROT_DIM), F32)
    zero = jnp.zeros((t, HEAD_DIM - ROT_DIM), F32)
    z8 = jnp.zeros((t, half), F32)
    c64 = jnp.concatenate([cos, cos, one], 1)
    s1 = jnp.concatenate([z8, sin, zero], 1)
    s2 = jnp.concatenate([-sin, z8, zero], 1)
    rep = lambda a: jnp.tile(a, (1, 4))
    return jnp.concatenate([rep(c64), rep(s1), rep(s2)], axis=1)


def _rope256(x, tab):
    n = x.shape[1]
    return (x * tab[:, 0:256] + pltpu.roll(x, 8, 1) * tab[:, 256:512]
            + pltpu.roll(x, n - 8, 1) * tab[:, 512:768])


def _inproj_kernel(x_ref, sc_ref, sh_ref, w_ref, tab_ref,
                   qn_ref, qr_ref, kv_ref, win_ref, kvb_ref, z_ref, xbc_ref, gd_ref):
    h = (x_ref[0] * (1.0 + sc_ref[0]) + sh_ref[0]).astype(BF16)
    tab = tab_ref[...]
    for c in range(4):
        q = _dot(h, w_ref[:, c * 256:(c + 1) * 256])
        qn_ref[0, :, c * 256:(c + 1) * 256] = q.astype(BF16)
        qr_ref[0, :, c * 256:(c + 1) * 256] = _rope256(q, tab).astype(BF16)
    o = D_ATT
    for p in range(6):
        y = _dot(h, w_ref[:, o + p * 256:o + (p + 1) * 256])
        if p in (2, 4):
            y = _rope256(y, tab)
        if p < 4:
            kv_ref[0, :, p * 256:(p + 1) * 256] = y
        else:
            win_ref[0, :, (p - 4) * 256:(p - 3) * 256] = y
        if p >= 2:
            kvb_ref[0, :, (p - 2) * 256:(p - 1) * 256] = y.astype(BF16)
    o += D_KV
    for c in range(2):
        z_ref[0, :, c * 512:(c + 1) * 512] = _dot(h, w_ref[:, o + c * 512:o + (c + 1) * 512]).astype(BF16)
    o += D_SSM
    for c in range(4):
        xbc_ref[0, :, c * 512:(c + 1) * 512] = _dot(h, w_ref[:, o + c * 512:o + (c + 1) * 512])
    o += CONV_CH
    gd_ref[0] = _dot(h, w_ref[:, o:o + 128])


def _in_proj(x, sc, sh, w_bf, tab, tm):
    b, t, _ = x.shape
    nt = t // tm
    row = lambda w: pl.BlockSpec((1, tm, w), lambda i, j: (i, j, 0))
    mod = row(D_MODEL) if sc.shape[1] == t else pl.BlockSpec((1, 1, D_MODEL), lambda i, j: (i, 0, 0))
    outs = [(D_ATT, BF16), (D_ATT, BF16), (1024, F32), (512, F32), (1024, BF16),
            (D_SSM, BF16), (CONV_CH, F32), (128, F32)]
    return pl.pallas_call(
        _inproj_kernel,
        out_shape=[jax.ShapeDtypeStruct((b, t, w), dt) for w, dt in outs],
        grid=(b, nt),
        in_specs=[row(D_MODEL), mod, mod,
                  pl.BlockSpec((D_MODEL, D_INP), lambda i, j: (0, 0), pipeline_mode=pl.Buffered(1)),
                  pl.BlockSpec((tm, 768), lambda i, j: (j, 0))],
        out_specs=[row(w) for w, _ in outs],
        compiler_params=_cparams(("parallel", "parallel")),
        name="in_proj",
    )(x, sc, sh, w_bf, tab)


def _prep_cmp_weights(cmp_pe, cmp_w1, cmp_b1, cmp_w2):
    w1 = cmp_w1.reshape(2, 2, CMP_STRIDE, HEAD_DIM, CMP_HID)
    eye = jnp.eye(2, dtype=cmp_w1.dtype)
    w1p = jnp.einsum('phsdj,ab->psadhbj', w1, eye).reshape(2, CMP_STRIDE * 128, 4 * CMP_HID)
    w2p = jnp.einsum('pjd,ab->pajbd', cmp_w2, eye).reshape(2, 2 * CMP_HID, 2 * HEAD_DIM)
    pe = cmp_pe.reshape(2, 1, CMP_LEN * HEAD_DIM)
    pe = jnp.concatenate([pe, jnp.zeros((2, 7, CMP_LEN * HEAD_DIM), pe.dtype)], axis=1)
    b1 = jnp.concatenate([cmp_b1, cmp_b1], axis=-1).reshape(2, 1, 2 * CMP_HID)
    return w1p.astype(BF16), w2p.astype(BF16), pe, cmp_w1, b1


def _gelu_tanh(x):
    return 0.5 * x * (1.0 + jnp.tanh(0.7978845608028654 * (x + 0.044715 * x * x * x)))


def _compress_core(load, nrows, w1p_ref, w2p_ref, pe_ref, w1_ref, b1_ref, store, prepare=None):
    for p in range(2):
        pe_bias = _dot(pe_ref[p], w1_ref[p], precision=HI)[0:1]
        pe_bias = jnp.concatenate([pe_bias, pe_bias], axis=1)
        for pr in range(2):
            if prepare is not None:
                prepare(2 * p + pr)
            xs = jnp.concatenate([load(s, 2 * p + pr).astype(BF16) for s in range(CMP_STRIDE)], axis=1)
            ab = _dot(xs, w1p_ref[p])
            a, bb = ab[:, 0:256], ab[:, 256:512]
            pre = a + pltpu.roll(bb, nrows - 1, 0) + pe_bias + b1_ref[p]
            hid = _gelu_tanh(pre).astype(BF16)
            store(p, pr, _dot(hid, w2p_ref[p]))


def _compress_prompt_kernel(b0, b1, b2, b3, w1p_ref, w2p_ref, pe_ref, w1_ref, b1_ref, kc_ref, vc_ref):
    blocks = (b0, b1, b2, b3)
    n = b0.shape[1] // CMP_STRIDE
    outs = (kc_ref, vc_ref)

    def load(s, lb):
        return blocks[lb][0, pl.ds(s, n, stride=CMP_STRIDE), :]

    def store(p, pr, val):
        outs[p][0, :, pr * 128:(pr + 1) * 128] = val.astype(BF16)

    _compress_core(load, n, w1p_ref, w2p_ref, pe_ref, w1_ref, b1_ref, store)


def _wspecs():
    z3 = lambda *a: (0, 0, 0)
    return [pl.BlockSpec((2, CMP_STRIDE * 128, 4 * CMP_HID), z3),
            pl.BlockSpec((2, 2 * CMP_HID, 2 * HEAD_DIM), z3),
            pl.BlockSpec((2, 8, CMP_LEN * HEAD_DIM), z3),
            pl.BlockSpec((2, CMP_LEN * HEAD_DIM, CMP_HID), z3),
            pl.BlockSpec((2, 1, 2 * CMP_HID), z3)]


def _compress_prompt(kv_f32, cw):
    b, t, _ = kv_f32.shape
    n = t // CMP_STRIDE
    lane_blk = lambda lb: pl.BlockSpec((1, t, 128), lambda i: (i, 0, lb))
    return pl.pallas_call(
        _compress_prompt_kernel,
        out_shape=[jax.ShapeDtypeStruct((b, n, 256), BF16)] * 2,
        grid=(b,),
        in_specs=[lane_blk(lb) for lb in range(4)] + _wspecs(),
        out_specs=[pl.BlockSpec((1, n, 256), lambda i: (i, 0, 0))] * 2,
        compiler_params=_cparams(("parallel",)),
        name="compress_prompt",
    )(kv_f32, kv_f32, kv_f32, kv_f32, *cw)


CMP_PAGES = 32


def _compress_sample_kernel(pt_ref, *refs):
    npg = CMP_PAGES + 1
    pages = refs[:npg]
    w1p_ref, w2p_ref, pe_ref, w1_ref, b1_ref, kc_ref, vc_ref, tok_ref = refs[npg:]
    cpp = PAGE // CMP_STRIDE
    nrows = CMP_PAGES * cpp
    outs = (kc_ref, vc_ref)
    def prepare(lb):
        for k in range(npg):
            tok_ref[lb, k * PAGE:(k + 1) * PAGE, :] = pages[k][0, lb * 128:(lb + 1) * 128, :].T

    def load(s, lb):
        return tok_ref[lb, pl.ds(s, nrows + cpp, stride=CMP_STRIDE), :]

    def store(p, pr, val):
        outs[p][0, :, pr * 128:(pr + 1) * 128] = val[:nrows].astype(BF16)

    _compress_core(load, nrows + cpp, w1p_ref, w2p_ref, pe_ref, w1_ref, b1_ref, store, prepare)


def _compress_sample(cache_t, page_table, cw):
    b, n_pages = page_table.shape
    steps = n_pages // CMP_PAGES
    nrows = CMP_PAGES * (PAGE // CMP_STRIDE)

    def pspec(k):
        return pl.BlockSpec(
            (1, 512, PAGE),
            lambda i, j, pt: (pt[i, jnp.minimum(j * CMP_PAGES + k, n_pages - 1)], 0, 0))

    gs = pltpu.PrefetchScalarGridSpec(
        num_scalar_prefetch=1,
        grid=(b, steps),
        in_specs=[pspec(k) for k in range(CMP_PAGES + 1)]
                 + [pl.BlockSpec(s.block_shape, lambda i, j, pt: (0, 0, 0)) for s in _wspecs()],
        out_specs=[pl.BlockSpec((1, nrows, 256), lambda i, j, pt: (i, j, 0))] * 2,
        scratch_shapes=[pltpu.VMEM((4, (CMP_PAGES + 1) * PAGE, 128), F32)],
    )
    return pl.pallas_call(
        _compress_sample_kernel,
        out_shape=[jax.ShapeDtypeStruct((b, n_pages * 8, 256), BF16)] * 2,
        grid_spec=gs,
        compiler_params=_cparams(("parallel", "parallel")),
        name="compress_sample",
    )(page_table, *([cache_t] * (CMP_PAGES + 1)), *cw)


DT_LANE = D_GT


def _prep_ssm_params(conv_w, conv_b, dt_bias, a_log, d_skip, ssm_norm_g):
    cw = jnp.concatenate([conv_w, jnp.zeros((8 - CONV_W, CONV_CH), conv_w.dtype)], axis=0)
    lane = lambda v: jnp.zeros((1, 128), F32).at[0, DT_LANE:DT_LANE + N_HEADS_SSM].set(v)
    dsk = jnp.repeat(d_skip, HEAD_DIM).reshape(1, D_SSM)
    return (cw, conv_b.reshape(1, CONV_CH), lane(dt_bias), lane(a_log), dsk,
            ssm_norm_g.reshape(1, D_SSM))


def _ssd_kernel(xbc_ref, z_ref, gd_ref, cprev_ref, h0_ref, cw_ref, cb_ref, dtb_ref, alog_ref,
                dsk_ref, ng_ref, y_ref, st_ref, xpad_ref):
    q = xbc_ref.shape[1]
    c = pl.program_id(1)

    @pl.when(c == 0)
    def _():
        st_ref[0] = h0_ref[0]
        xpad_ref[0:8, :] = cprev_ref[0]

    xpad_ref[8:8 + q, :] = xbc_ref[0]
    conv = cb_ref[...] + xpad_ref[pl.ds(5, q), :] * cw_ref[0:1, :]
    for k in range(1, CONV_W):
        conv = conv + xpad_ref[pl.ds(5 + k, q), :] * cw_ref[k:k + 1, :]
    xpad_ref[0:8, :] = xpad_ref[q:q + 8, :]
    xc = _silu(conv)
    xs = xc[:, 0:D_SSM]

    lane = lax.broadcasted_iota(jnp.int32, (1, 128), 1)
    in_dt = (lane >= DT_LANE) & (lane < DT_LANE + N_HEADS_SSM)
    v = gd_ref[0] + dtb_ref[...]
    dt = jnp.maximum(v, 0.0) + jnp.log1p(jnp.exp(-jnp.abs(v)))
    a = jnp.where(in_dt, -jnp.exp(alog_ref[...]), 0.0) * dt
    er = lax.broadcasted_iota(jnp.int32, (128, D_SSM), 0)
    ec = lax.broadcasted_iota(jnp.int32, (128, D_SSM), 1)
    expand = er - DT_LANE == ec // HEAD_DIM
    ri = lax.broadcasted_iota(jnp.int32, (q, q), 0)
    ci = lax.broadcasted_iota(jnp.int32, (q, q), 1)
    causal = ri >= ci
    acs = _dot_sel_l(causal, a)
    acs_t = acs.T
    dt_x = _dot_sel_r(jnp.where(in_dt, dt, 0.0), expand)
    acs_x = _dot_sel_r(acs, expand)
    last_x = acs_x[q - 1:q, :]
    grow_x = jnp.exp(acs_x)
    decay_x = jnp.exp(last_x - acs_x)
    er2 = lax.broadcasted_iota(jnp.int32, (D_SSM, 128), 0)
    ec2 = lax.broadcasted_iota(jnp.int32, (D_SSM, 128), 1)
    expand_t = ec2 - DT_LANE == er2 // HEAD_DIM
    tot_col = jnp.exp(_dot_sel_l(expand_t, acs_t)[:, q - 1:q])

    xd = xs * dt_x
    xdd = (xd * decay_x).astype(BF16)
    xd_b = xd.astype(BF16)
    half = lax.broadcasted_iota(jnp.int32, (1, 128), 1) < HEAD_DIM
    ys = []
    for g in range(SSM_GROUPS):
        bg = xc[:, D_SSM + g * D_STATE:D_SSM + (g + 1) * D_STATE].astype(BF16)
        cg = xc[:, D_SSM + (SSM_GROUPS + g) * D_STATE:D_SSM + (SSM_GROUPS + g + 1) * D_STATE].astype(BF16)
        cbm = _dot_nt(cg, bg)
        for m in (2 * g, 2 * g + 1):
            sl = slice(128 * m, 128 * (m + 1))
            yh = []
            for hh in (2 * m, 2 * m + 1):
                col = acs[:, DT_LANE + hh:DT_LANE + hh + 1]
                row = acs_t[DT_LANE + hh:DT_LANE + hh + 1, :]
                lm = jnp.where(causal, jnp.exp(jnp.where(causal, col - row, 0.0)), 0.0)
                yh.append(_dot((cbm * lm).astype(BF16), xd_b[:, sl]))
            y_diag = jnp.where(half, yh[0], yh[1])
            st = st_ref[0, sl, :]
            y_off = _dot_nt(cg, st.astype(BF16)) * grow_x[:, sl]
            st_ref[0, sl, :] = st * tot_col[sl, :] + _dot_tn(xdd[:, sl], bg)
            ys.append(y_diag + y_off)
    y = jnp.concatenate(ys, axis=1) + dsk_ref[...] * xs
    y = y * _silu(z_ref[0].astype(F32))
    gw = D_SSM // SSM_GROUPS
    outs = []
    for g in range(SSM_GROUPS):
        blk = y[:, g * gw:(g + 1) * gw]
        ms = jnp.mean(blk * blk, axis=1, keepdims=True)
        outs.append(blk * lax.rsqrt(ms + RMS_EPS))
    y_ref[0] = (jnp.concatenate(outs, axis=1) * ng_ref[...]).astype(BF16)


def _ssd(xbc, z, gd, conv_prev8, h0, sp, q):
    b, t, _ = xbc.shape
    nc = t // q
    row = lambda w: pl.BlockSpec((1, q, w), lambda i, j: (i, j, 0))
    per_b = lambda r, w: pl.BlockSpec((1, r, w), lambda i, j: (i, 0, 0))
    par = lambda r, w: pl.BlockSpec((r, w), lambda i, j: (0, 0))
    return pl.pallas_call(
        _ssd_kernel,
        out_shape=[jax.ShapeDtypeStruct((b, t, D_SSM), BF16),
                   jax.ShapeDtypeStruct((b, D_SSM, D_STATE), F32)],
        grid=(b, nc),
        in_specs=[row(CONV_CH), row(D_SSM), row(128), per_b(8, CONV_CH), per_b(D_SSM, D_STATE),
                  par(8, CONV_CH), par(1, CONV_CH), par(1, 128), par(1, 128), par(1, D_SSM),
                  par(1, D_SSM)],
        out_specs=[row(D_SSM), per_b(D_SSM, D_STATE)],
        scratch_shapes=[pltpu.VMEM((q + 8, CONV_CH), F32)],
        compiler_params=_cparams(("parallel", "arbitrary")),
        name="ssd",
    )(xbc, z, gd, conv_prev8, h0, *sp)


ATT_TQ = 512
SCALE = HEAD_DIM ** -0.5


def _select_blocks_t(ps, n_blk):
    blk = lax.broadcasted_iota(jnp.int32, ps.shape, 0)
    rank = jnp.zeros(ps.shape, F32)
    for i in range(n_blk):
        vi = ps[i:i + 1, :]
        rank = rank + jnp.where(vi > ps, 1.0, jnp.where((vi == ps) & (blk > i), 1.0, 0.0))
    return jnp.where(rank < N_SELECT, 1.0, 0.0)


def _attn_prompt_kernel(qn_ref, qr_ref, kc_ref, vc_ref, ks_ref, vs_ref, kw_ref, vw_ref, gd_ref,
                        o_ref, qz_ref, sel_ref, ocmp_ref, m_ref, acc_ref):
    tq = qn_ref.shape[1]
    t_all = ks_ref.shape[1]
    n_cmp = t_all // CMP_STRIDE - 1
    n_blk = t_all // SLC_BLK
    gp = pl.program_id(1)
    qi = pl.program_id(2)
    rep = lambda a: jnp.concatenate([a] * HPG, axis=1)
    pos = qi * tq + lax.broadcasted_iota(jnp.int32, (1, tq), 1)
    pos4 = rep(pos)
    lane = lax.broadcasted_iota(jnp.int32, (1, 128), 1)
    row = lax.broadcasted_iota(jnp.int32, (128, 1), 0)
    lane_side = (lane < HEAD_DIM, lane >= HEAD_DIM)
    row_side = (row < HEAD_DIM, row >= HEAD_DIM)

    for r, ref in enumerate((qn_ref, qr_ref)):
        for side in range(2):
            for j in range(HPG):
                col = ref[0, :, j * 128:(j + 1) * 128]
                col = jnp.where(lane_side[side], col, jnp.zeros_like(col))
                qz_ref[2 * r + side, j * tq:(j + 1) * tq, :] = col * SCALE

    def v_sides(v):
        vt = v.astype(F32).T
        return [jnp.where(row_side[s], vt, 1.0).astype(BF16) for s in range(2)]

    kc = kc_ref[0]
    vct = vc_ref[0].astype(F32).T.astype(BF16)
    vis = (CMP_STRIDE * row + (CMP_LEN - 1) <= pos4) & (row < n_cmp)
    br = lax.broadcasted_iota(jnp.int32, (128, 128), 0) * SLC_BLK
    cc = lax.broadcasted_iota(jnp.int32, (128, 128), 1) * CMP_STRIDE
    overlap_t = ((cc < br + SLC_BLK) & (cc + CMP_LEN > br)).astype(F32)
    cur = pos // SLC_BLK
    forced = (row == 0) | (row == cur) | (row == cur - 1)
    for side in range(2):
        s = jnp.where(vis, _dot_nt(kc, qz_ref[side]), NEG)
        e = jnp.exp(s - jnp.max(s, axis=0, keepdims=True))
        p = jnp.where(vis, e / jnp.sum(e, axis=0, keepdims=True), 0.0)
        ocmp_ref[side] = _dot(vct, p.astype(BF16))
        psum = p[:, 0:tq]
        for j in range(1, HPG):
            psum = psum + p[:, j * tq:(j + 1) * tq]
        ps = _dot(overlap_t, psum, precision=HI)
        ps = jnp.where(row <= cur, jnp.where(forced, FORCE, ps), NEG)
        sel = _select_blocks_t(ps[0:n_blk, :], n_blk)
        sel_ref[side] = jnp.concatenate([sel, jnp.zeros((128 - n_blk, tq), F32)], axis=0).astype(BF16)

    m_ref[...] = jnp.full(m_ref.shape, NEG, F32)
    acc_ref[...] = jnp.zeros(acc_ref.shape, F32)

    def update(idx, s, vts):
        m_old = m_ref[idx]
        m_new = jnp.maximum(m_old, jnp.max(s, axis=0, keepdims=True))
        p = jnp.exp(s - m_new).astype(BF16)
        acc_ref[idx] = jnp.exp(m_old - m_new) * acc_ref[idx] + _dot(vts, p)
        m_ref[idx] = m_new

    def slc_tile(t, diagonal):
        k0 = pl.multiple_of(t * tq, tq)
        k = ks_ref[0, pl.ds(k0, tq), :]
        vts = v_sides(vs_ref[0, pl.ds(k0, tq), :])
        kpos = k0 + lax.broadcasted_iota(jnp.int32, (tq, 1), 0)
        expand_t = (lax.broadcasted_iota(jnp.int32, (tq, 128), 1) == kpos // SLC_BLK).astype(BF16)
        for side in range(2):
            keep = _dot(expand_t, sel_ref[side])
            if diagonal:
                keep = jnp.where(kpos <= pos, keep, 0.0)
            bias = (keep - 1.0) * (-NEG)
            update(side, _dot_nt(k, qz_ref[2 + side]) + rep(bias), vts[side])

    def slc_body(t, carry):
        slc_tile(t, False)
        return carry

    lax.fori_loop(0, qi, slc_body, 0)
    slc_tile(qi, True)

    def win_tile(t, kind):
        k0 = pl.multiple_of(t * tq, tq)
        k = kw_ref[0, pl.ds(k0, tq), :]
        vts = v_sides(vw_ref[0, pl.ds(k0, tq), :])
        kpos = k0 + lax.broadcasted_iota(jnp.int32, (tq, 1), 0)
        mask = (kpos > pos - WINDOW) if kind == 0 else ((kpos <= pos) if kind == 2 else None)
        bias = None if mask is None else rep(jnp.where(mask, 0.0, NEG))
        for side in range(2):
            s = _dot_nt(k, qz_ref[2 + side])
            if bias is not None:
                s = s + bias
            update(2 + side, s, vts[side])

    n_back = WINDOW // tq
    for d in range(n_back, 0, -1):
        pl.when(qi >= d)(functools.partial(win_tile, qi - d, 0 if d == n_back else 1))
    win_tile(qi, 2)

    gates_t = _sigmoid(gd_ref[0]).T
    for j in range(HPG):
        sl = slice(j * tq, (j + 1) * tq)
        cols = []
        for side in range(2):
            lrow = HEAD_DIM if side == 0 else 0
            gi = (2 * gp + side) * (3 * HPG) + 3 * j
            g = [jnp.sum(jnp.where(row == gi + k, gates_t, 0.0), axis=0, keepdims=True)
                 for k in range(3)]
            a_s = acc_ref[side, :, sl]
            a_w = acc_ref[2 + side, :, sl]
            cols.append(g[0] * ocmp_ref[side, :, sl] + (g[1] / a_s[lrow:lrow + 1, :]) * a_s
                        + (g[2] / a_w[lrow:lrow + 1, :]) * a_w)
        o_ref[0, :, j * 128:(j + 1) * 128] = jnp.where(row_side[0], cols[0], cols[1]).T.astype(BF16)


def _attn_prompt(qn, qr, kc, vc, kvb, gd):
    b, t, _ = qn.shape
    tq = ATT_TQ
    assert WINDOW % tq == 0 and t % tq == 0 and t // SLC_BLK <= 128 and kc.shape[1] == 128
    tw = HPG * tq
    qspec = pl.BlockSpec((1, tq, 512), lambda i, g, q: (i, q, g))
    cspec = pl.BlockSpec((1, kc.shape[1], 128), lambda i, g, q: (i, 0, g))
    kvspec = lambda base: pl.BlockSpec((1, t, 128), lambda i, g, q: (i, 0, base + g))
    return pl.pallas_call(
        _attn_prompt_kernel,
        out_shape=jax.ShapeDtypeStruct((b, t, D_ATT), BF16),
        grid=(b, 2, t // tq),
        in_specs=[qspec, qspec, cspec, cspec, kvspec(0), kvspec(2), kvspec(4), kvspec(6),
                  pl.BlockSpec((1, tq, 128), lambda i, g, q: (i, q, 0))],
        out_specs=qspec,
        scratch_shapes=[pltpu.VMEM((4, tw, 128), BF16), pltpu.VMEM((2, 128, tq), BF16),
                        pltpu.VMEM((2, 128, tw), F32), pltpu.VMEM((4, 1, tw), F32),
                        pltpu.VMEM((4, 128, tw), F32)],
        compiler_params=_cparams(("parallel", "parallel", "arbitrary")),
        name="attn_prompt",
    )(qn, qr, kc, vc, kvb, kvb, kvb, kvb, gd)


ATS_PAGES = 16
N_ROWS_S = 128


def _attn_sample_kernel(pt_ref, *refs, n_steps, past, t_dec):
    npg = ATS_PAGES
    kpages = refs[0:2 * npg:2]
    vpages = refs[1:2 * npg:2]
    (qn_ref, qr_ref, kc_ref, vc_ref, kvn_ref, cwin_ref, wnew_ref, gt_ref,
     o_ref, sel_ref, ocmp_ref, m_ref, l_ref, acc_ref) = refs[2 * npg:]
    s_id = pl.program_id(1)
    nr = N_ROWS_S
    rows = lax.broadcasted_iota(jnp.int32, (nr, 1), 0)
    t_row = (rows % (t_dec * HPG)) // HPG
    pos = past + t_row
    qr = qr_ref[0]

    @pl.when(s_id == 0)
    def _():
        n_c = kc_ref.shape[1]
        n_blk = (past + t_dec + SLC_BLK - 1) // SLC_BLK
        bps = ATS_PAGES * PAGE // SLC_BLK
        nbl = -(-(n_steps * bps + 128) // 128) * 128
        cl = lax.broadcasted_iota(jnp.int32, (1, n_c), 1)
        vis = (CMP_STRIDE * cl + (CMP_LEN - 1) <= pos) & (cl < n_c - 1)
        s = jnp.where(vis, _dot_nt(qn_ref[0], kc_ref[0]) * SCALE, NEG)
        e = jnp.exp(s - jnp.max(s, axis=1, keepdims=True))
        p = jnp.where(vis, e / jnp.sum(e, axis=1, keepdims=True), 0.0)
        ocmp_ref[...] = _dot(p.astype(BF16), vc_ref[0])
        ng = nr // HPG
        gsum = (lax.broadcasted_iota(jnp.int32, (ng, nr), 1) // HPG
                == lax.broadcasted_iota(jnp.int32, (ng, nr), 0)).astype(F32)
        cr = lax.broadcasted_iota(jnp.int32, (n_c, nbl), 0) * CMP_STRIDE
        j0 = lax.broadcasted_iota(jnp.int32, (n_c, nbl), 1) * SLC_BLK
        overlap = ((cr < j0 + SLC_BLK) & (cr + CMP_LEN > j0)).astype(F32)
        ps = _dot(_dot(gsum, p, precision=HI), overlap, precision=HI)
        bl = lax.broadcasted_iota(jnp.int32, (1, nbl), 1)
        g_rows = lax.broadcasted_iota(jnp.int32, (ng, 1), 0)
        cur = (past + g_rows % t_dec) // SLC_BLK
        forced = (bl == 0) | (bl == cur) | (bl == cur - 1)
        ps = jnp.where(bl <= cur, jnp.where(forced, FORCE, ps), NEG)
        rank = jnp.zeros(ps.shape, F32)
        for i in range(n_blk):
            vi = ps[:, i:i + 1]
            rank = rank + ((vi > ps) | ((vi == ps) & (bl > i))).astype(F32)
        sel = jnp.where((rank < N_SELECT) & (bl < n_blk), 1.0, 0.0)
        gexp = (lax.broadcasted_iota(jnp.int32, (nr, ng), 0) // HPG
                == lax.broadcasted_iota(jnp.int32, (nr, ng), 1)).astype(F32)
        sel_rows = _dot(gexp, sel)
        for w in range(sel_ref.shape[0]):
            sel_ref[w] = sel_rows[:, bps * w:bps * w + 128].astype(BF16)
        m_ref[...] = jnp.full(m_ref.shape, NEG, F32)
        l_ref[...] = jnp.zeros(l_ref.shape, F32)
        acc_ref[...] = jnp.zeros(acc_ref.shape, F32)

    def update(s, mask, v, v_transposed=False):
        s = jnp.where(mask, s, NEG)
        m_old = m_ref[...]
        m_new = jnp.maximum(m_old, jnp.max(s, axis=1, keepdims=True))
        p = jnp.where(mask, jnp.exp(s - m_new), 0.0)
        alpha = jnp.exp(m_old - m_new)
        l_ref[...] = alpha * l_ref[...] + jnp.sum(p, axis=1, keepdims=True)
        pv = _dot_nt(p.astype(BF16), v) if v_transposed else _dot(p.astype(BF16), v)
        acc_ref[...] = alpha * acc_ref[...] + pv
        m_ref[...] = m_new

    nk = npg * PAGE
    k_t = jnp.concatenate([r[0] for r in kpages], axis=1).astype(BF16)
    v_t = jnp.concatenate([r[0] for r in vpages], axis=1).astype(BF16)
    expand = (lax.broadcasted_iota(jnp.int32, (128, nk), 0)
              == lax.broadcasted_iota(jnp.int32, (128, nk), 1) // SLC_BLK).astype(BF16)
    kpos = s_id * nk + lax.broadcasted_iota(jnp.int32, (1, nk), 1)
    mask = (_dot(sel_ref[s_id], expand) > 0.5) & (kpos <= pos)
    update(_dot(qr, k_t) * SCALE, mask, v_t, v_transposed=True)

    @pl.when(s_id == n_steps - 1)
    def _():
        kn = kvn_ref[0, :, 512:768].astype(BF16)
        vn = kvn_ref[0, :, 768:1024].astype(BF16)
        tk = lax.broadcasted_iota(jnp.int32, (1, t_dec), 1)
        cur_sel = sel_ref[n_steps][:, 0:1].astype(F32) > 0.5
        update(_dot_nt(qr, kn) * SCALE, cur_sel & (past + tk <= pos), vn)
        o_slc = acc_ref[...] / l_ref[...]
        wb = cwin_ref.shape[1]
        kw = cwin_ref[0, :, 0:256].astype(BF16)
        vw = cwin_ref[0, :, 256:512].astype(BF16)
        kwn = wnew_ref[0, :, 0:256].astype(BF16)
        vwn = wnew_ref[0, :, 256:512].astype(BF16)
        d1 = pos - (past - wb + lax.broadcasted_iota(jnp.int32, (1, wb), 1))
        d2 = pos - (past + tk)
        m1 = (d1 >= 0) & (d1 < WINDOW)
        m2 = (d2 >= 0) & (d2 < WINDOW)
        s1 = jnp.where(m1, _dot_nt(qr, kw) * SCALE, NEG)
        s2 = jnp.where(m2, _dot_nt(qr, kwn) * SCALE, NEG)
        mx = jnp.maximum(jnp.max(s1, axis=1, keepdims=True), jnp.max(s2, axis=1, keepdims=True))
        p1 = jnp.where(m1, jnp.exp(s1 - mx), 0.0)
        p2 = jnp.where(m2, jnp.exp(s2 - mx), 0.0)
        den = jnp.sum(p1, axis=1, keepdims=True) + jnp.sum(p2, axis=1, keepdims=True)
        o_win = (_dot(p1.astype(BF16), vw) + _dot(p2.astype(BF16), vwn)) / den
        g = _sigmoid(gt_ref[0])
        o_ref[0] = g[:, 0:1] * ocmp_ref[...] + g[:, 1:2] * o_slc + g[:, 2:3] * o_win


def _attn_sample(cache_t, page_table, qn_x, qr_x, kc, vc, kv_new, cache_win2, win_new, gt_rows, t_dec):
    b, n_pages = page_table.shape
    past = n_pages * PAGE
    n_steps = n_pages // ATS_PAGES
    nr = N_ROWS_S

    def pspec(k, blk):
        return pl.BlockSpec((1, 256, PAGE), lambda i, s, pt: (pt[i, s * ATS_PAGES + k], blk, 0))

    per_b = lambda r, w: pl.BlockSpec((1, r, w), lambda i, s, pt: (i, 0, 0))
    page_specs = []
    for k in range(ATS_PAGES):
        page_specs += [pspec(k, 2), pspec(k, 3)]
    gs = pltpu.PrefetchScalarGridSpec(
        num_scalar_prefetch=1,
        grid=(b, n_steps),
        in_specs=page_specs + [per_b(nr, 256), per_b(nr, 256), per_b(kc.shape[1], 256),
                               per_b(kc.shape[1], 256), per_b(t_dec, 1024),
                               per_b(cache_win2.shape[1], 512), per_b(t_dec, 512), per_b(nr, 128)],
        out_specs=per_b(nr, 256),
        scratch_shapes=[pltpu.VMEM((n_steps + 1, nr, 128), BF16), pltpu.VMEM((nr, 256), F32),
                        pltpu.VMEM((nr, 1), F32), pltpu.VMEM((nr, 1), F32),
                        pltpu.VMEM((nr, 256), F32)],
    )
    return pl.pallas_call(
        functools.partial(_attn_sample_kernel, n_steps=n_steps, past=past, t_dec=t_dec),
        out_shape=jax.ShapeDtypeStruct((b, nr, 256), F32),
        grid_spec=gs,
        compiler_params=_cparams(("parallel", "arbitrary")),
        name="attn_sample",
    )(page_table, *([cache_t] * (2 * ATS_PAGES)), qn_x, qr_x, kc, vc, kv_new, cache_win2, win_new,
      gt_rows)


def _expand_rows(q):
    b, t, _ = q.shape
    q5 = q.reshape(b, t, N_KV, HPG, HEAD_DIM)
    eye = jnp.eye(N_KV, dtype=q.dtype)
    return jnp.einsum('btghd,gk->bgthkd', q5, eye).reshape(b, N_KV * t * HPG, N_KV * HEAD_DIM)


def _collapse_rows(o, t):
    b = o.shape[0]
    o6 = o.reshape(b, N_KV, t, HPG, N_KV, HEAD_DIM)
    return jnp.einsum('bgthkd,gk->btghd', o6, jnp.eye(N_KV, dtype=o.dtype)).reshape(b, t, D_ATT)


def _layernorm(x, g, b):
    mu = jnp.mean(x, axis=-1, keepdims=True)
    xc = x - mu
    var = jnp.mean(xc * xc, axis=-1, keepdims=True)
    return xc * lax.rsqrt(var + LN_EPS) * g + b


OUT_SUB = 256


def _outproj_kernel(oa_ref, ys_ref, x_ref, g1_ref, sc2_ref, sh2_ref, w_ref, lg_ref, lb_ref,
                    rw_ref, rb_ref, x1_ref, h2_ref, ti_ref, tw_ref):
    tm = x_ref.shape[1]
    sub = min(OUT_SUB, tm)
    for r0 in range(0, tm, sub):
        rs = slice(r0, r0 + sub)
        mix = _dot(oa_ref[0, rs, :], w_ref[0:D_ATT, :]) + _dot(ys_ref[0, rs, :], w_ref[D_ATT:, :])
        x1 = _layernorm(DN_ALPHA * x_ref[0, rs, :] + g1_ref[0] * mix, lg_ref[...], lb_ref[...])
        x1_ref[0, rs, :] = x1
        h2 = x1 * (1.0 + sc2_ref[0]) + sh2_ref[0]
        h_hi = h2.astype(BF16)
        h2_ref[0, rs, :] = _pack_pairs(h2)
        h_lo = (h2 - h_hi.astype(F32)).astype(BF16)
        logits = (_dot(h_hi, rw_ref[0]) + (_dot(h_lo, rw_ref[0]) + _dot(h_hi, rw_ref[1]))
                  + rb_ref[...])
        lane = lax.broadcasted_iota(jnp.int32, logits.shape, 1)
        vals, ids = [], []
        for _ in range(TOP_K):
            mx = jnp.max(logits, axis=1, keepdims=True)
            ix = jnp.min(jnp.where(logits == mx, lane, 128), axis=1, keepdims=True)
            vals.append(mx)
            ids.append(ix)
            logits = jnp.where(lane == ix, -jnp.inf, logits)
        es = [jnp.exp(v - vals[0]) for v in vals]
        inv = 1.0 / (es[0] + es[1] + es[2] + es[3])
        ti = jnp.zeros(lane.shape, jnp.int32)
        tw = jnp.zeros(lane.shape, F32)
        for k in range(TOP_K):
            ti = jnp.where(lane == k, ids[k], ti)
            tw = jnp.where(lane == k, es[k] * inv, tw)
        ti_ref[0, rs, :] = ti
        tw_ref[0, rs, :] = tw


def _out_proj(o_att, y_ssm, x, g1, sc2, sh2, w_out_bf, ln_g, ln_b, rw, rb, tm):
    b, t, _ = x.shape
    row = lambda w: pl.BlockSpec((1, tm, w), lambda i, j: (i, j, 0))
    per_b = pl.BlockSpec((1, 1, D_MODEL), lambda i, j: (i, 0, 0))
    par = lambda r, w, **kw: pl.BlockSpec((r, w), lambda i, j: (0, 0), **kw)
    return pl.pallas_call(
        _outproj_kernel,
        out_shape=[jax.ShapeDtypeStruct((b, t, D_MODEL), F32),
                   jax.ShapeDtypeStruct((b, t, D_MODEL // 2), jnp.uint32),
                   jax.ShapeDtypeStruct((b, t, 128), jnp.int32),
                   jax.ShapeDtypeStruct((b, t, 128), F32)],
        grid=(b, t // tm),
        in_specs=[row(D_ATT), row(D_SSM), row(D_MODEL), per_b, per_b, per_b,
                  par(D_MODEL, D_MODEL, pipeline_mode=pl.Buffered(1)),
                  par(1, D_MODEL), par(1, D_MODEL),
                  pl.BlockSpec((2, D_MODEL, 128), lambda i, j: (0, 0, 0)), par(1, 128)],
        out_specs=[row(D_MODEL), row(D_MODEL // 2), row(128), row(128)],
        compiler_params=_cparams(("parallel", "parallel")),
        name="out_proj",
    )(o_att, y_ssm, x, g1, sc2, sh2, w_out_bf, ln_g, ln_b, rw, rb)


MOE_TM = 1024
MOE_TN = 512
MOE_TN_DOWN = 1024
ROUTE_TM = 256


def _route_kernel(ti_ref, rank_ref, cnt_ref):
    @pl.when(pl.program_id(0) == 0)
    def _():
        cnt_ref[...] = jnp.zeros_like(cnt_ref)

    ti = ti_ref[...]
    tm = ti.shape[0]
    lane = lax.broadcasted_iota(jnp.int32, (tm, 128), 1)
    hits = [lane == ti[:, k:k + 1] for k in range(TOP_K)]
    oh = jnp.zeros((tm, 128), F32)
    for h in hits:
        oh = oh + h.astype(F32)
    ri = lax.broadcasted_iota(jnp.int32, (tm, tm), 0)
    ci = lax.broadcasted_iota(jnp.int32, (tm, tm), 1)
    before = _dot((ri > ci).astype(BF16), oh.astype(BF16)) + cnt_ref[0:1, :]
    rank = jnp.zeros((tm, 128), jnp.int32)
    for k, h in enumerate(hits):
        rk = jnp.sum(jnp.where(h, before, 0.0), axis=1, keepdims=True).astype(jnp.int32)
        rank = jnp.where(lane == k, rk, rank)
    rank_ref[...] = rank
    cnt_ref[...] = cnt_ref[...] + jnp.sum(oh, axis=0, keepdims=True)


def _route(ti_all):
    n = ti_all.shape[0]
    return pl.pallas_call(
        _route_kernel,
        out_shape=[jax.ShapeDtypeStruct((n, 128), jnp.int32),
                   jax.ShapeDtypeStruct((8, 128), F32)],
        grid=(n // ROUTE_TM,),
        in_specs=[pl.BlockSpec((ROUTE_TM, 128), lambda i: (i, 0))],
        out_specs=[pl.BlockSpec((ROUTE_TM, 128), lambda i: (i, 0)),
                   pl.BlockSpec((8, 128), lambda i: (0, 0))],
        compiler_params=_cparams(("arbitrary",)),
        name="moe_route",
    )(ti_all)


DMA_UNROLL = 8


def _dispatch_kernel(pos_ref, h_ref, xs_in_ref, xs_ref, sem):
    del xs_in_ref
    def copy(t, k):
        return pltpu.make_async_copy(h_ref.at[pl.ds(t, 1), :],
                                     xs_ref.at[pl.ds(pos_ref[0, 0, t * TOP_K + k], 1), :], sem)

    def start(t, c):
        for k in range(TOP_K):
            copy(t, k).start(priority=k % 2)
        return c

    def wait(t, c):
        for k in range(TOP_K):
            copy(t, k).wait()
        return c

    lax.fori_loop(0, ROUTE_TM, start, 0, unroll=DMA_UNROLL // TOP_K)
    lax.fori_loop(0, ROUTE_TM, wait, 0, unroll=DMA_UNROLL // TOP_K)


PAD_CHUNK = 64


def _zero_pad_kernel(start_ref, n_ref, xs_ref, zero_ref, sem):
    e = pl.program_id(0)
    zero_ref[...] = jnp.zeros(zero_ref.shape, zero_ref.dtype)

    def copy(c):
        row0 = pl.multiple_of((start_ref[e] + c) * PAD_CHUNK, PAD_CHUNK)
        return pltpu.make_async_copy(zero_ref, xs_ref.at[pl.ds(row0, PAD_CHUNK), :], sem)

    def start(c, carry):
        copy(c).start()
        return carry

    def wait(c, carry):
        copy(c).wait()
        return carry

    lax.fori_loop(0, n_ref[e], start, 0)
    lax.fori_loop(0, n_ref[e], wait, 0)


def _zero_pad(chunk_start, n_chunks, n_rows, width):
    gs = pltpu.PrefetchScalarGridSpec(
        num_scalar_prefetch=2,
        grid=(N_EXPERTS,),
        in_specs=[],
        out_specs=pl.BlockSpec(memory_space=pl.ANY),
        scratch_shapes=[pltpu.VMEM((PAD_CHUNK, width), jnp.uint32), pltpu.SemaphoreType.DMA(())],
    )
    return pl.pallas_call(
        _zero_pad_kernel,
        out_shape=jax.ShapeDtypeStruct((n_rows, width), jnp.uint32),
        grid_spec=gs,
        compiler_params=_cparams(("arbitrary",)),
        name="moe_zero_pad",
    )(chunk_start, n_chunks)


def _dispatch(pos, h, xs):
    n, w = h.shape
    assert n % ROUTE_TM == 0
    steps = n // ROUTE_TM
    return pl.pallas_call(
        _dispatch_kernel,
        out_shape=jax.ShapeDtypeStruct(xs.shape, xs.dtype),
        grid=(steps,),
        in_specs=[pl.BlockSpec((1, 1, ROUTE_TM * TOP_K), lambda i: (i, 0, 0), memory_space=pltpu.SMEM),
                  pl.BlockSpec((ROUTE_TM, w), lambda i: (i, 0)),
                  pl.BlockSpec(memory_space=pl.ANY)],
        out_specs=pl.BlockSpec(memory_space=pl.ANY),
        scratch_shapes=[pltpu.SemaphoreType.DMA(())],
        input_output_aliases={2: 0},
        compiler_params=_cparams(("arbitrary",), no_bounds_checks=True),
        name="moe_dispatch",
    )(pos.reshape(steps, 1, ROUTE_TM * TOP_K), h, xs)


def _moe_up_kernel(te_ref, nt_ref, tv_ref, xs_ref, wg_ref, wu_ref, bg_ref, bu_ref, act_ref):
    hk = D_MODEL // 2

    @pl.when(tv_ref[pl.program_id(0)] > 0)
    def _():
        lo, hi = [a.astype(BF16) for a in _unpack_pairs(xs_ref[...])]
        wg = wg_ref[0].astype(BF16)
        wu = wu_ref[0].astype(BF16)
        g = _dot(lo, wg[:hk]) + _dot(hi, wg[hk:]) + bg_ref[0]
        u = _dot(lo, wu[:hk]) + _dot(hi, wu[hk:]) + bu_ref[0]
        g = jnp.minimum(g, SWIGLU_LIMIT)
        u = jnp.clip(u, -SWIGLU_LIMIT, SWIGLU_LIMIT)
        act_ref[...] = ((u + 1.0) * g * _sigmoid(SWIGLU_ALPHA * g)).astype(BF16)

    @pl.when(tv_ref[pl.program_id(0)] <= 0)
    def _():
        act_ref[...] = jnp.zeros_like(act_ref)


def _moe_up(tile_e, n_used, tile_valid, xs, w_gu, b_gu3):
    n_rows = xs.shape[0]
    nj = D_FF // MOE_TN
    row_i = lambda i, j, te, nt, tv: (jnp.minimum(i, nt[0] - 1), 0)
    jc = lambda i, j, nt: jnp.where(i < nt[0], j, nj - 1)
    wspec = lambda off: pl.BlockSpec((1, D_MODEL, MOE_TN),
                                     lambda i, j, te, nt, tv: (te[i], 0, jc(i, j, nt) + off))
    bspec = lambda off: pl.BlockSpec((1, 1, MOE_TN),
                                     lambda i, j, te, nt, tv: (te[i], 0, jc(i, j, nt) + off))
    gs = pltpu.PrefetchScalarGridSpec(
        num_scalar_prefetch=3,
        grid=(n_rows // MOE_TM, nj),
        in_specs=[pl.BlockSpec((MOE_TM, D_MODEL // 2), row_i), wspec(0), wspec(nj), bspec(0), bspec(nj)],
        out_specs=pl.BlockSpec((MOE_TM, MOE_TN), lambda i, j, te, nt, tv: (i, j)),
    )
    return pl.pallas_call(
        _moe_up_kernel,
        out_shape=jax.ShapeDtypeStruct((n_rows, D_FF), BF16),
        grid_spec=gs,
        compiler_params=_cparams(("parallel", "arbitrary")),
        name="moe_up",
    )(tile_e, n_used, tile_valid, xs, w_gu, w_gu, b_gu3, b_gu3)


def _moe_down_kernel(te_ref, nt_ref, tv_ref, a_ref, w_ref, b_ref, y_ref):
    @pl.when(tv_ref[pl.program_id(0)] > 0)
    def _():
        y_ref[...] = _pack_pairs(_dot(a_ref[...], w_ref[0].astype(BF16)) + b_ref[0])

    @pl.when(tv_ref[pl.program_id(0)] <= 0)
    def _():
        y_ref[...] = jnp.zeros_like(y_ref)


def _moe_down(tile_e, n_used, tile_valid, act, w_down, b_down3):
    n_rows = act.shape[0]
    tn = MOE_TN_DOWN
    nj = D_MODEL // tn
    row_i = lambda i, j, te, nt, tv: (jnp.minimum(i, nt[0] - 1), 0)
    jc = lambda i, j, nt: jnp.where(i < nt[0], j, nj - 1)
    gs = pltpu.PrefetchScalarGridSpec(
        num_scalar_prefetch=3,
        grid=(n_rows // MOE_TM, nj),
        in_specs=[pl.BlockSpec((MOE_TM, D_FF), row_i),
                  pl.BlockSpec((1, D_FF, tn), lambda i, j, te, nt, tv: (te[i], 0, jc(i, j, nt))),
                  pl.BlockSpec((1, 1, tn), lambda i, j, te, nt, tv: (te[i], 0, jc(i, j, nt)))],
        out_specs=pl.BlockSpec((MOE_TM, tn // 2), lambda i, j, te, nt, tv: (i, j)),
    )
    return pl.pallas_call(
        _moe_down_kernel,
        out_shape=jax.ShapeDtypeStruct((n_rows, D_MODEL // 2), jnp.uint32),
        grid_spec=gs,
        compiler_params=_cparams(("parallel", "arbitrary")),
        name="moe_down",
    )(tile_e, n_used, tile_valid, act, w_down, b_down3)


def _combine_kernel(pos_ref, yd_ref, x1_ref, tw_ref, g2_ref, lg_ref, lb_ref, o_ref, buf, sem):
    tm = buf.shape[1]

    def copy(t, k):
        return pltpu.make_async_copy(yd_ref.at[pl.ds(pos_ref[0, 0, t * TOP_K + k], 1), :],
                                     buf.at[k, pl.ds(t, 1), :], sem)

    def start(t, c):
        for k in range(TOP_K):
            copy(t, k).start(priority=k % 2)
        return c

    def wait(t, c):
        for k in range(TOP_K):
            copy(t, k).wait()
        return c

    lax.fori_loop(0, tm, start, 0, unroll=DMA_UNROLL // TOP_K)
    lax.fori_loop(0, tm, wait, 0, unroll=DMA_UNROLL // TOP_K)
    tw = tw_ref[0]
    hw = MOE_TN_DOWN // 2
    moe = None
    for k in range(TOP_K):
        parts = []
        for j in range(buf.shape[2] // hw):
            parts += list(_unpack_pairs(buf[k, :, j * hw:(j + 1) * hw]))
        term = jnp.concatenate(parts, axis=1) * tw[:, k:k + 1]
        moe = term if moe is None else moe + term
    o_ref[0] = _layernorm(DN_ALPHA * x1_ref[0] + g2_ref[0] * moe, lg_ref[...], lb_ref[...])


def _combine(pos, yd, x1, tw, g2, ln_g, ln_b, tok0):
    b, t, _ = x1.shape
    tm = min(ROUTE_TM, t)
    per = t // tm
    step0 = tok0 // tm
    row = lambda w: pl.BlockSpec((1, tm, w), lambda i, j: (i, j, 0))
    par = pl.BlockSpec((1, D_MODEL), lambda i, j: (0, 0))
    return pl.pallas_call(
        _combine_kernel,
        out_shape=jax.ShapeDtypeStruct((b, t, D_MODEL), F32),
        grid=(b, per),
        in_specs=[pl.BlockSpec((1, 1, tm * TOP_K), lambda i, j: (step0 + i * per + j, 0, 0),
                               memory_space=pltpu.SMEM),
                  pl.BlockSpec(memory_space=pl.ANY),
                  row(D_MODEL), row(128),
                  pl.BlockSpec((1, 1, D_MODEL), lambda i, j: (i, 0, 0)), par, par],
        out_specs=row(D_MODEL),
        scratch_shapes=[pltpu.VMEM((TOP_K, tm, D_MODEL // 2), jnp.uint32),
                        pltpu.SemaphoreType.DMA(())],
        compiler_params=_cparams(("arbitrary", "arbitrary"), no_bounds_checks=True),
        name="moe_combine",
    )(pos.reshape(-1, 1, tm * TOP_K), yd, x1, tw, g2, ln_g, ln_b)


IN_TM = 256
SSD_Q = 256
OUT_TM = 512


def kernel(x_prompt, x_sample, cache_kv, cache_win, state_conv, state_ssm, page_table,
           c_prompt, c_sample, w_ada, b_ada, w_in, cmp_pe, cmp_w1, cmp_b1, cmp_w2,
           conv_w, conv_b, dt_bias, a_log, d_skip, ssm_norm_g, w_out, ln1_g, ln1_b,
           router_w, router_b, w_gu, b_gu, w_down, b_down, ln2_g, ln2_b):
    bp, tp, _ = x_prompt.shape
    bs, ts, _ = x_sample.shape
    past = page_table.shape[1] * PAGE

    m = _ada(jnp.concatenate([c_prompt, c_sample], axis=0), w_ada[0], b_ada[0])
    mod = [m[:, None, k * D_MODEL:(k + 1) * D_MODEL] for k in range(6)]
    mod_p = [a[:bp] for a in mod]
    mod_s = [a[bp:] for a in mod]

    perm = _q_perm()
    inv_perm = np.argsort(perm)
    w_bf = _prep_w_in(w_in[0])
    cw = _prep_cmp_weights(cmp_pe[0], cmp_w1[0], cmp_b1[0], cmp_w2[0])
    sp = _prep_ssm_params(conv_w[0], conv_b[0], dt_bias[0], a_log[0], d_skip[0], ssm_norm_g[0])
    w_out_bf = jnp.concatenate([w_out[0][:D_ATT][perm], w_out[0][D_ATT:]], axis=0).astype(BF16)
    rw = jnp.concatenate([router_w[0], jnp.zeros((D_MODEL, 128 - N_EXPERTS), F32)], axis=1)
    rw_hi = rw.astype(BF16)
    rw = jnp.stack([rw_hi, (rw - rw_hi.astype(F32)).astype(BF16)])
    rb = jnp.concatenate([router_b[0], jnp.full((128 - N_EXPERTS,), NEG, F32)]).reshape(1, 128)
    ln1 = (ln1_g[0].reshape(1, D_MODEL), ln1_b[0].reshape(1, D_MODEL))
    ln2 = (ln2_g[0].reshape(1, D_MODEL), ln2_b[0].reshape(1, D_MODEL))

    tab_p = _rope_tables(jnp.arange(tp))
    qn, qr, kv_p, win_p, kvb, z, xbc, gd = _in_proj(x_prompt, mod_p[1], mod_p[0], w_bf, tab_p, IN_TM)
    kc, vc = _compress_prompt(kv_p, cw)
    o_att = _attn_prompt(qn, qr, kc, vc, kvb, gd)
    y_ssm, ssm_p = _ssd(xbc, z, gd, jnp.zeros((bp, 8, CONV_CH), F32),
                        jnp.zeros((bp, D_SSM, D_STATE), F32), sp, SSD_Q)
    x1_p, h2_p, ti_p, tw_p = _out_proj(o_att, y_ssm, x_prompt, mod_p[2], mod_p[4], mod_p[3],
                                       w_out_bf, *ln1, rw, rb, OUT_TM)

    tab_s = _rope_tables(past + jnp.arange(ts))
    n_s = bs * ts
    per_tok = lambda a: jnp.broadcast_to(a, (bs, ts, D_MODEL)).reshape(1, n_s, D_MODEL)
    s_outs = _in_proj(x_sample.reshape(1, n_s, D_MODEL), per_tok(mod_s[1]), per_tok(mod_s[0]),
                      w_bf, jnp.tile(tab_s, (bs, 1)), n_s)
    qn_s, qr_s, kv_s, win_s, _, z_s, xbc_s, gd_s = [a.reshape(bs, ts, a.shape[-1]) for a in s_outs]
    cache_t = jnp.transpose(cache_kv[0], (0, 2, 3, 4, 1)).reshape(
        cache_kv.shape[1], 4 * N_KV * HEAD_DIM, PAGE)
    kc_s, vc_s = _compress_sample(cache_t, page_table, cw)
    wb = cache_win.shape[2]
    cwin2 = cache_win[0].reshape(bs, wb, 2 * N_KV * HEAD_DIM)
    gt_rows = gd_s[:, :, :D_GT].reshape(bs, ts, N_KV, HPG, 3).transpose(0, 2, 1, 3, 4)
    gt_rows = jnp.concatenate([gt_rows.reshape(bs, N_ROWS_S, 3),
                               jnp.zeros((bs, N_ROWS_S, 125), F32)], axis=-1)
    o_rows = _attn_sample(cache_t, page_table, _expand_rows(qn_s[:, :, inv_perm]),
                          _expand_rows(qr_s[:, :, inv_perm]), kc_s, vc_s, kv_s, cwin2, win_s,
                          gt_rows, ts)
    o_att_s = _collapse_rows(o_rows, ts)[:, :, perm].astype(BF16)
    cprev = jnp.concatenate([jnp.zeros((bs, 5, CONV_CH), F32), state_conv[0]], axis=1)
    y_ssm_s, ssm_s = _ssd(xbc_s, z_s, gd_s, cprev, state_ssm[0].reshape(bs, D_SSM, D_STATE), sp, ts)
    x1_s, h2_s, ti_s, tw_s = _out_proj(o_att_s, y_ssm_s, x_sample, mod_s[2], mod_s[4], mod_s[3],
                                       w_out_bf, *ln1, rw, rb, ts)

    n_tok = bp * tp + bs * ts
    n_tiles = -(-(n_tok * TOP_K + N_EXPERTS * (MOE_TM - 1)) // MOE_TM)
    ti_all = jnp.concatenate([ti_p.reshape(-1, 128), ti_s.reshape(-1, 128)], axis=0)
    rank, cnt = _route(ti_all)
    counts = cnt[0, :N_EXPERTS].astype(jnp.int32)
    padded = (counts + MOE_TM - 1) // MOE_TM * MOE_TM
    ends = jnp.cumsum(padded)
    offs = ends - padded
    n_used = (ends[-1] // MOE_TM).astype(jnp.int32).reshape(1)
    tiles = jnp.minimum(jnp.arange(n_tiles, dtype=jnp.int32), n_used[0] - 1)
    tile_e = jnp.sum((tiles[:, None] * MOE_TM >= ends[None, :]).astype(jnp.int32), axis=1)
    tile_e = jnp.minimum(tile_e, N_EXPERTS - 1).astype(jnp.int32)
    first_tile = offs // MOE_TM
    tile_valid = jnp.clip(counts[tile_e] - (tiles - first_tile[tile_e]) * MOE_TM, 0, MOE_TM)
    tile_valid = jnp.where(jnp.arange(n_tiles) < n_used[0], tile_valid, 0).astype(jnp.int32)
    pos = (offs[ti_all[:, :TOP_K]] + rank[:, :TOP_K]).astype(jnp.int32)
    n_p = bp * tp
    chunk_start = ((offs + counts) // PAD_CHUNK).astype(jnp.int32)
    n_chunks = (ends // PAD_CHUNK - chunk_start).astype(jnp.int32)
    xs = _zero_pad(chunk_start, n_chunks, n_tiles * MOE_TM, D_MODEL // 2)
    xs = _dispatch(pos[:n_p], h2_p.reshape(n_p, D_MODEL // 2), xs)
    xs = _dispatch(pos[n_p:], h2_s.reshape(bs * ts, D_MODEL // 2), xs)
    act = _moe_up(tile_e, n_used, tile_valid, xs, w_gu[0], b_gu[0].reshape(N_EXPERTS, 1, 2 * D_FF))
    yd = _moe_down(tile_e, n_used, tile_valid, act, w_down[0],
                   b_down[0].reshape(N_EXPERTS, 1, D_MODEL))
    y_p = _combine(pos, yd, x1_p, tw_p, mod_p[5], *ln2, 0)
    y_s = _combine(pos, yd, x1_s, tw_s, mod_s[5], *ln2, bp * tp)

    kv_shape = (4, N_KV, HEAD_DIM)
    win_shape = (2, N_KV, HEAD_DIM)
    win_prompt = win_p[:, tp - min(WINDOW, tp):].reshape((1, bp, min(WINDOW, tp)) + win_shape)
    win_sample = jnp.concatenate([cwin2, win_s], axis=1)[:, -wb:].reshape((1, bs, wb) + win_shape)
    return (y_p, y_s,
            kv_p.reshape((1, bp, tp) + kv_shape), kv_s.reshape((1, bs, ts) + kv_shape),
            win_prompt, win_sample,
            xbc[None, :, tp - (CONV_W - 1):], xbc_s[None, :, ts - (CONV_W - 1):],
            ssm_p.reshape(1, bp, N_HEADS_SSM, HEAD_DIM, D_STATE),
            ssm_s.reshape(1, bs, N_HEADS_SSM, HEAD_DIM, D_STATE))
```
